```python
import math
import jax, jax.numpy as jnp
from jax import lax
import numpy as np

D_MODEL = 1024
BATCH = 8
SEQ = 2048
DEPTH = 4

HEAD_DIM = 64
S5_WIDTH = D_MODEL // 4
S5_GROUP = 16
S5_GROUPS = S5_WIDTH // S5_GROUP
S5_STATE = 64
MOBA_WIDTH = 3 * D_MODEL // 8
MOBA_HEADS = MOBA_WIDTH // HEAD_DIM
GDN_WIDTH = 3 * D_MODEL // 8
GDN_HEADS = GDN_WIDTH // HEAD_DIM
MIX_WIDTH = S5_WIDTH + MOBA_WIDTH + GDN_WIDTH
IN_SIZES = (S5_WIDTH, MOBA_WIDTH, MOBA_WIDTH, MOBA_WIDTH,
            GDN_WIDTH, GDN_WIDTH, GDN_WIDTH, GDN_WIDTH, GDN_HEADS, GDN_HEADS)
IN_COLS = sum(IN_SIZES)

MOBA_BLOCK = 256
MOBA_TOPK = 3
MOBA_QBLOCK = 32
GDN_CHUNK = 64
CONV_WIDTH = 4

D_FF = 7 * D_MODEL // 2
N_EXPERTS = 8
TOP_K = 2
N_DENSE = (DEPTH + 1) // 2
N_MOE = DEPTH // 2

DEEPNORM_ALPHA = (2 * DEPTH) ** 0.25
DEEPNORM_BETA = (8 * DEPTH) ** -0.25
LN_EPS = 1e-5
RMS_EPS = 1e-6

kernel_name = "hymba_s5_moba_gdn_deepnorm_moe"

F32 = jnp.float32


def _split(h, sizes):
    idx = np.cumsum(np.array(sizes))[:-1].tolist()
    return jnp.split(h, idx, axis=-1)


def layer_norm(x, g, b):
    xf = x.astype(F32)
    mu = jnp.mean(xf, axis=-1, keepdims=True)
    var = jnp.mean(jnp.square(xf - mu), axis=-1, keepdims=True)
    return ((xf - mu) * lax.rsqrt(var + LN_EPS) * g.astype(F32) + b.astype(F32)).astype(x.dtype)


def l2norm(t):
    return t * lax.rsqrt(jnp.sum(jnp.square(t), axis=-1, keepdims=True) + RMS_EPS)


def s5_mixer(u, lam_re, lam_im, log_dt, b_re, b_im, c_re, c_im, d_skip, glu_w, glu_b):
    bsz, seqlen, _ = u.shape
    lam = lax.complex(lam_re.astype(F32), lam_im.astype(F32))
    step = jnp.exp(log_dt.astype(F32))[:, None]
    lam_bar = jnp.exp(lam * step)
    b_mat = lax.complex(b_re.astype(F32), b_im.astype(F32))
    b_bar = ((lam_bar - 1.0) / lam)[..., None] * b_mat
    ug = u.astype(F32).reshape(bsz, seqlen, S5_GROUPS, S5_GROUP)
    bu = jnp.einsum('blgp,gnp->blgn', ug.astype(jnp.complex64), b_bar)
    a = jnp.broadcast_to(lam_bar, bu.shape)

    def combine(e1, e2):
        a1, s1 = e1
        a2, s2 = e2
        return a1 * a2, a2 * s1 + s2

    _, states = lax.associative_scan(combine, (a, bu), axis=1)
    c_mat = lax.complex(c_re.astype(F32), c_im.astype(F32))
    y = jnp.einsum('gpn,blgn->blgp', c_mat, states).real + d_skip.astype(F32) * ug
    y = jax.nn.gelu(y.reshape(bsz, seqlen, S5_WIDTH))
    y = y * jax.nn.sigmoid(y @ glu_w.astype(F32) + glu_b.astype(F32))
    return y.astype(u.dtype)


def moba_mixer(q, k, v):
    bsz, seqlen, _ = q.shape
    n_blocks = -(-seqlen // MOBA_BLOCK)
    lpad = n_blocks * MOBA_BLOCK

    def heads(t):
        t = t.reshape(bsz, seqlen, MOBA_HEADS, HEAD_DIM).transpose(0, 2, 1, 3).astype(F32)
        return jnp.pad(t, ((0, 0), (0, 0), (0, lpad - seqlen), (0, 0)))

    qh = heads(q) * HEAD_DIM ** -0.5
    kh, vh = heads(k), heads(v)
    kb = kh.reshape(bsz, MOBA_HEADS, n_blocks, MOBA_BLOCK, HEAD_DIM)
    vb = vh.reshape(bsz, MOBA_HEADS, n_blocks, MOBA_BLOCK, HEAD_DIM)
    kmean = jnp.mean(kb, axis=3)

    gate = jnp.einsum('bhld,bhnd->bhln', qh, kmean)
    qblk = jnp.arange(lpad) // MOBA_BLOCK
    past = jnp.arange(n_blocks)[None, :] < qblk[:, None]
    gate = jnp.where(past, gate, -jnp.inf)
    n_sel = min(MOBA_TOPK, n_blocks)
    sel_score, sel_idx = lax.top_k(gate, n_sel)
    sel_valid = jnp.isfinite(sel_score)

    n_qb = lpad // MOBA_QBLOCK

    def by_qblock(t):
        t = t.reshape(bsz, MOBA_HEADS, n_qb, MOBA_QBLOCK, *t.shape[3:])
        return jnp.moveaxis(t, 2, 0)

    b_ix = jnp.arange(bsz)[:, None, None, None]
    h_ix = jnp.arange(MOBA_HEADS)[None, :, None, None]

    def attend(args):
        qc, idx, valid, i = args
        qpos = i * MOBA_QBLOCK + jnp.arange(MOBA_QBLOCK)
        own = (i * MOBA_QBLOCK) // MOBA_BLOCK
        k_own = lax.dynamic_index_in_dim(kb, own, axis=2, keepdims=False)
        v_own = lax.dynamic_index_in_dim(vb, own, axis=2, keepdims=False)
        kpos = own * MOBA_BLOCK + jnp.arange(MOBA_BLOCK)
        s_own = jnp.einsum('bhqd,bhkd->bhqk', qc, k_own)
        s_own = jnp.where(kpos[None, :] <= qpos[:, None], s_own, -jnp.inf)
        k_sel = kb[b_ix, h_ix, idx]
        v_sel = vb[b_ix, h_ix, idx]
        s_sel = jnp.einsum('bhqd,bhqnkd->bhqnk', qc, k_sel)
        s_sel = jnp.where(valid[..., None], s_sel, -jnp.inf)
        s_sel = s_sel.reshape(*s_sel.shape[:3], n_sel * MOBA_BLOCK)
        p = jax.nn.softmax(jnp.concatenate([s_own, s_sel], axis=-1), axis=-1)
        p_own = p[..., :MOBA_BLOCK]
        p_sel = p[..., MOBA_BLOCK:].reshape(*p.shape[:3], n_sel, MOBA_BLOCK)
        return (jnp.einsum('bhqk,bhkd->bhqd', p_own, v_own)
                + jnp.einsum('bhqnk,bhqnkd->bhqd', p_sel, v_sel))

    o = lax.map(attend, (by_qblock(qh), by_qblock(sel_idx), by_qblock(sel_valid), jnp.arange(n_qb)))
    o = jnp.moveaxis(o, 0, 2).reshape(bsz, MOBA_HEADS, lpad, HEAD_DIM)[:, :, :seqlen]
    return o.transpose(0, 2, 1, 3).reshape(bsz, seqlen, MOBA_WIDTH).astype(q.dtype)


def causal_conv(x, w):
    ch = x.shape[-1]
    return lax.conv_general_dilated(x, w[:, None, :].astype(x.dtype), window_strides=(1,),
                                    padding=[(CONV_WIDTH - 1, 0)],
                                    dimension_numbers=('NWC', 'WIO', 'NWC'),
                                    feature_group_count=ch)


def chunk_gated_delta(q, k, v, g, beta):
    bsz, nh, seqlen, dk = q.shape
    dv = v.shape[-1]
    n = seqlen // GDN_CHUNK
    C = GDN_CHUNK
    q = q.reshape(bsz, nh, n, C, dk)
    k = k.reshape(bsz, nh, n, C, dk)
    v = v.reshape(bsz, nh, n, C, dv)
    g = jnp.cumsum(g.reshape(bsz, nh, n, C), axis=-1)
    beta = beta.reshape(bsz, nh, n, C)
    kb = k * beta[..., None]
    vb = v * beta[..., None]
    tril_incl = jnp.tril(jnp.ones((C, C), dtype=bool))
    tril_strict = jnp.tril(jnp.ones((C, C), dtype=bool), -1)
    diff = jnp.where(tril_incl, g[..., :, None] - g[..., None, :], 0.0)
    decay = jnp.where(tril_incl, jnp.exp(diff), 0.0)
    lmat = jnp.where(tril_strict, jnp.einsum('bhncd,bhnsd->bhncs', kb, k) * decay, 0.0)
    eye = jnp.eye(C, dtype=F32)
    rhs = jnp.concatenate([vb, kb * jnp.exp(g)[..., None]], axis=-1)
    sol = lax.linalg.triangular_solve(eye + lmat, rhs, left_side=True, lower=True,
                                      unit_diagonal=True)
    u_val, w_k = sol[..., :dv], sol[..., dv:]
    attn = jnp.where(tril_incl, jnp.einsum('bhncd,bhnsd->bhncs', q, k) * decay, 0.0)

    def step(S, inp):
        qc, kc, uc, wc, gc, ac = inp
        v_new = uc - jnp.einsum('bhcd,bhde->bhce', wc, S)
        o = (jnp.einsum('bhcd,bhde->bhce', qc * jnp.exp(gc)[..., None], S)
             + jnp.einsum('bhcs,bhse->bhce', ac, v_new))
        g_last = gc[..., -1:]
        S = (S * jnp.exp(g_last)[..., None]
             + jnp.einsum('bhcd,bhce->bhde', kc * jnp.exp(g_last - gc)[..., None], v_new))
        return S, o

    xs = tuple(jnp.moveaxis(t, 2, 0) for t in (q, k, u_val, w_k, g, attn))
    S0 = jnp.zeros((bsz, nh, dk, dv), F32)
    _, o = lax.scan(step, S0, xs)
    return jnp.moveaxis(o, 0, 2).reshape(bsz, nh, seqlen, dv)


def gdn_mixer(q, k, v, z, a, b, conv_w, a_log, dt_bias, norm_w):
    bsz, seqlen, _ = q.shape
    qkv = jax.nn.silu(causal_conv(jnp.concatenate([q, k, v], axis=-1), conv_w).astype(F32))
    q, k, v = _split(qkv, (GDN_WIDTH, GDN_WIDTH, GDN_WIDTH))

    def heads(t):
        return t.reshape(bsz, seqlen, GDN_HEADS, HEAD_DIM).transpose(0, 2, 1, 3)

    qh = l2norm(heads(q)) * HEAD_DIM ** -0.5
    kh = l2norm(heads(k))
    vh = heads(v)
    beta = jax.nn.sigmoid(b.astype(F32)).transpose(0, 2, 1)
    g = (-jnp.exp(a_log.astype(F32))
         * jax.nn.softplus(a.astype(F32) + dt_bias.astype(F32))).transpose(0, 2, 1)
    o = chunk_gated_delta(qh, kh, vh, g, beta).transpose(0, 2, 1, 3)
    zh = z.astype(F32).reshape(bsz, seqlen, GDN_HEADS, HEAD_DIM)
    o = (o * lax.rsqrt(jnp.mean(jnp.square(o), axis=-1, keepdims=True) + RMS_EPS)
         * norm_w.astype(F32) * jax.nn.silu(zh))
    return o.reshape(bsz, seqlen, GDN_WIDTH).astype(q.dtype)


def swiglu(x, w_gate, w_up, w_down):
    return (jax.nn.silu(x @ w_gate) * (x @ w_up)) @ w_down


def moe_ffn(x, router_w, w_gate, w_up, w_down):
    logits = (x @ router_w).astype(F32)
    top_val, top_idx = lax.top_k(logits, TOP_K)
    gates = jax.nn.softmax(top_val, axis=-1)
    combine = jnp.sum(jax.nn.one_hot(top_idx, N_EXPERTS, dtype=F32) * gates[..., None], axis=-2)
    out = jnp.zeros(x.shape, F32)
    for e in range(N_EXPERTS):
        out = out + combine[..., e:e + 1] * swiglu(x, w_gate[e], w_up[e], w_down[e]).astype(F32)
    return out.astype(x.dtype)


def setup_inputs(seed: int = 0) -> dict:
    key = jax.random.key(seed)
    ks = jax.random.split(key, 32)
    nrm = jax.random.normal
    G, N, P = S5_GROUPS, S5_STATE, S5_GROUP
    x = nrm(ks[0], (BATCH, SEQ, D_MODEL), F32)
    w_in = nrm(ks[1], (DEPTH, D_MODEL, IN_COLS), F32) * D_MODEL ** -0.5
    w_out = nrm(ks[2], (DEPTH, MIX_WIDTH, D_MODEL), F32) * MIX_WIDTH ** -0.5 * DEEPNORM_BETA
    s5_lam_re = -0.5 + 0.01 * nrm(ks[3], (DEPTH, G, N), F32)
    s5_lam_im = (jnp.pi * jnp.arange(N, dtype=F32))[None, None, :] + 0.01 * nrm(ks[4], (DEPTH, G, N), F32)
    s5_log_dt = jax.random.uniform(ks[5], (DEPTH, G), F32, math.log(1e-3), math.log(1e-1))
    s5_b_re = nrm(ks[6], (DEPTH, G, N, P), F32) * (2 * P) ** -0.5
    s5_b_im = nrm(ks[7], (DEPTH, G, N, P), F32) * (2 * P) ** -0.5
    s5_c_re = nrm(ks[8], (DEPTH, G, P, N), F32) * (2 * N) ** -0.5
    s5_c_im = nrm(ks[9], (DEPTH, G, P, N), F32) * (2 * N) ** -0.5
    s5_d = nrm(ks[10], (DEPTH, G, P), F32)
    s5_glu_w = nrm(ks[11], (DEPTH, S5_WIDTH, S5_WIDTH), F32) * S5_WIDTH ** -0.5
    s5_glu_b = 0.01 * nrm(ks[12], (DEPTH, S5_WIDTH), F32)
    gdn_conv_w = nrm(ks[13], (DEPTH, CONV_WIDTH, 3 * GDN_WIDTH), F32) * CONV_WIDTH ** -0.5
    gdn_a_log = jnp.log(jax.random.uniform(ks[14], (DEPTH, GDN_HEADS), F32, 1.0, 16.0))
    dt0 = jnp.exp(jax.random.uniform(ks[15], (DEPTH, GDN_HEADS), F32, math.log(1e-3), math.log(1e-1)))
    gdn_dt_bias = dt0 + jnp.log(-jnp.expm1(-dt0))
    gdn_norm_w = 1.0 + 0.01 * nrm(ks[16], (DEPTH, HEAD_DIM), F32)
    ln1_g = 1.0 + 0.01 * nrm(ks[17], (DEPTH, D_MODEL), F32)
    ln1_b = 0.01 * nrm(ks[18], (DEPTH, D_MODEL), F32)
    ln2_g = 1.0 + 0.01 * nrm(ks[19], (DEPTH, D_MODEL), F32)
    ln2_b = 0.01 * nrm(ks[20], (DEPTH, D_MODEL), F32)
    ffn_w_gate = nrm(ks[21], (N_DENSE, D_MODEL, D_FF), F32) * D_MODEL ** -0.5
    ffn_w_up = nrm(ks[22], (N_DENSE, D_MODEL, D_FF), F32) * D_MODEL ** -0.5
    ffn_w_down = nrm(ks[23], (N_DENSE, D_FF, D_MODEL), F32) * D_FF ** -0.5 * DEEPNORM_BETA
    moe_router = nrm(ks[24], (N_MOE, D_MODEL, N_EXPERTS), F32) * D_MODEL ** -0.5
    moe_w_gate = nrm(ks[25], (N_MOE, N_EXPERTS, D_MODEL, D_FF), F32) * D_MODEL ** -0.5
    moe_w_up = nrm(ks[26], (N_MOE, N_EXPERTS, D_MODEL, D_FF), F32) * D_MODEL ** -0.5
    moe_w_down = nrm(ks[27], (N_MOE, N_EXPERTS, D_FF, D_MODEL), F32) * D_FF ** -0.5 * DEEPNORM_BETA
    return {"x": x, "w_in": w_in, "w_out": w_out,
            "s5_lam_re": s5_lam_re, "s5_lam_im": s5_lam_im, "s5_log_dt": s5_log_dt,
            "s5_b_re": s5_b_re, "s5_b_im": s5_b_im, "s5_c_re": s5_c_re, "s5_c_im": s5_c_im,
            "s5_d": s5_d, "s5_glu_w": s5_glu_w, "s5_glu_b": s5_glu_b,
            "gdn_conv_w": gdn_conv_w, "gdn_a_log": gdn_a_log, "gdn_dt_bias": gdn_dt_bias,
            "gdn_norm_w": gdn_norm_w,
            "ln1_g": ln1_g, "ln1_b": ln1_b, "ln2_g": ln2_g, "ln2_b": ln2_b,
            "ffn_w_gate": ffn_w_gate, "ffn_w_up": ffn_w_up, "ffn_w_down": ffn_w_down,
            "moe_router": moe_router, "moe_w_gate": moe_w_gate, "moe_w_up": moe_w_up,
            "moe_w_down": moe_w_down}


def reference(x, w_in, w_out, s5_lam_re, s5_lam_im, s5_log_dt, s5_b_re, s5_b_im, s5_c_re,
              s5_c_im, s5_d, s5_glu_w, s5_glu_b, gdn_conv_w, gdn_a_log, gdn_dt_bias, gdn_norm_w,
              ln1_g, ln1_b, ln2_g, ln2_b, ffn_w_gate, ffn_w_up, ffn_w_down,
              moe_router, moe_w_gate, moe_w_up, moe_w_down):
    for l in range(DEPTH):
        h = x @ w_in[l]
        s5_u, mq, mk, mv, gq, gk, gv, gz, ga, gb = _split(h, IN_SIZES)
        y_s5 = s5_mixer(s5_u, s5_lam_re[l], s5_lam_im[l], s5_log_dt[l], s5_b_re[l], s5_b_im[l],
                        s5_c_re[l], s5_c_im[l], s5_d[l], s5_glu_w[l], s5_glu_b[l])
        y_moba = moba_mixer(mq, mk, mv)
        y_gdn = gdn_mixer(gq, gk, gv, gz, ga, gb, gdn_conv_w[l], gdn_a_log[l], gdn_dt_bias[l],
                          gdn_norm_w[l])
        mix = jnp.concatenate([y_s5, y_moba, y_gdn], axis=-1) @ w_out[l]
        x = layer_norm(DEEPNORM_ALPHA * x + mix, ln1_g[l], ln1_b[l])
        if l % 2 == 0:
            f = swiglu(x, ffn_w_gate[l // 2], ffn_w_up[l // 2], ffn_w_down[l // 2])
        else:
            f = moe_ffn(x, moe_router[l // 2], moe_w_gate[l // 2], moe_w_up[l // 2], moe_w_down[l // 2])
        x = layer_norm(DEEPNORM_ALPHA * x + f, ln2_g[l], ln2_b[l])
    return x
```

```python
import functools

import jax
import jax.numpy as jnp
from jax import lax
from jax.experimental import pallas as pl
from jax.experimental.pallas import tpu as pltpu

F32 = jnp.float32
BF16 = jnp.bfloat16
I32 = jnp.int32
HI = lax.Precision.HIGHEST

HEAD_DIM = 64
MOBA_BLOCK = 256
MOBA_TOPK = 3
GDN_CHUNK = 64
TOP_K = 2
LN_EPS = 1e-5
RMS_EPS = 1e-6
NEG_BIG = -1e30

V7X_VMEM_LIMIT_BYTES = 56 * 1024 * 1024
LANE = 128

FFN_ROWS = 1024
MOE_ROWS = 512
FFN_COLS = 512
GATHER_ROWS = 256


def _cparams(*sem):
    return pltpu.CompilerParams(dimension_semantics=sem, vmem_limit_bytes=V7X_VMEM_LIMIT_BYTES)


def _dot(a, b, precision=None):
    return jnp.dot(a, b, preferred_element_type=F32, precision=precision)


def _dot_nt(a, b, precision=None):
    return lax.dot_general(a, b, (((1,), (1,)), ((), ())), preferred_element_type=F32, precision=precision)


def _dot_tn(a, b, precision=None):
    return lax.dot_general(a, b, (((0,), (0,)), ((), ())), preferred_element_type=F32, precision=precision)


def _iota(shape, dim):
    return lax.broadcasted_iota(I32, shape, dim)


def _layer_norm(r, g, b):
    mu = jnp.mean(r, axis=-1, keepdims=True)
    c = r - mu
    var = jnp.mean(c * c, axis=-1, keepdims=True)
    return c * lax.rsqrt(var + LN_EPS) * g + b


def _inproj_kernel(x_ref, w_ref, wt_ref, u_ref, mqt_ref, mk_ref, mvt_ref, gq_ref, gk_ref, gv_ref, gz_ref,
                   ab_ref, *, s5w, mw, gw):
    xb = x_ref[...].astype(BF16)
    nh_m = mw // HEAD_DIM

    def cols(off, width):
        return _dot(xb, w_ref[:, off:off + width])

    u_ref[...] = cols(0, s5w)
    off = s5w + mw
    kk = cols(off, mw)
    for h in range(nh_m):
        mk_ref[0, h, 0] = kk[:, h * HEAD_DIM:(h + 1) * HEAD_DIM]
    off += 2 * mw
    for ref in (gq_ref, gk_ref, gv_ref, gz_ref):
        ref[0] = cols(off, gw)
        off += gw
    ab_ref[0] = cols(off, LANE)
    qt = _dot_nt(wt_ref[0:mw, :], xb)
    vt = _dot_nt(wt_ref[mw:2 * mw, :], xb)
    for h in range(nh_m):
        mqt_ref[0, h, 0] = qt[h * HEAD_DIM:(h + 1) * HEAD_DIM, :]
        mvt_ref[0, h, 0] = vt[h * HEAD_DIM:(h + 1) * HEAD_DIM, :]


def _pack_inproj(w_in, s5w, mw, gw):
    c = w_in.shape[1]
    main = s5w + 3 * mw + 4 * gw
    w = jnp.pad(w_in, ((0, 0), (0, main + LANE - c))).astype(BF16)
    q0, v0 = s5w, s5w + 2 * mw
    wt = jnp.concatenate([w_in[:, q0:q0 + mw], w_in[:, v0:v0 + mw]], axis=1).T.astype(BF16)
    return w, wt


def _inproj(x2, w, wt, *, bsz, seqlen, s5w, mw, gw):
    tl = MOBA_BLOCK
    nb = seqlen // tl
    d = x2.shape[1]
    nh_m = mw // HEAD_DIM
    hd = HEAD_DIM
    gdn_sds = jax.ShapeDtypeStruct((bsz, seqlen, gw), F32)
    out_shape = (
        jax.ShapeDtypeStruct((seqlen, bsz * s5w), F32),
        jax.ShapeDtypeStruct((bsz, nh_m, nb, hd, tl), F32),
        jax.ShapeDtypeStruct((bsz, nh_m, nb, tl, hd), F32),
        jax.ShapeDtypeStruct((bsz, nh_m, nb, hd, tl), F32),
        gdn_sds, gdn_sds, gdn_sds, gdn_sds,
        jax.ShapeDtypeStruct((bsz, seqlen, LANE), F32),
    )
    t5 = lambda b, j: (b, 0, j, 0, 0)
    t3 = lambda b, j: (b, j, 0)
    out_specs = (
        pl.BlockSpec((tl, s5w), lambda b, j: (j, b)),
        pl.BlockSpec((1, nh_m, 1, hd, tl), t5),
        pl.BlockSpec((1, nh_m, 1, tl, hd), t5),
        pl.BlockSpec((1, nh_m, 1, hd, tl), t5),
        pl.BlockSpec((1, tl, gw), t3),
        pl.BlockSpec((1, tl, gw), t3),
        pl.BlockSpec((1, tl, gw), t3),
        pl.BlockSpec((1, tl, gw), t3),
        pl.BlockSpec((1, tl, LANE), t3),
    )
    return pl.pallas_call(
        functools.partial(_inproj_kernel, s5w=s5w, mw=mw, gw=gw),
        out_shape=out_shape,
        grid=(bsz, nb),
        in_specs=[
            pl.BlockSpec((tl, d), lambda b, j: (b * nb + j, 0)),
            pl.BlockSpec(w.shape, lambda b, j: (0, 0)),
            pl.BlockSpec(wt.shape, lambda b, j: (0, 0)),
        ],
        out_specs=out_specs,
        compiler_params=_cparams("parallel", "arbitrary"),
        name="inproj",
    )(x2, w, wt)


def _s5_kernel(u_ref, bblk_ref, cblk_ref, lam_ref, d_ref, gw_ref, gb_ref, y_ref, st_ref, h_ref, *, bsz, tt, ns):
    @pl.when(pl.program_id(0) == 0)
    def _():
        h_ref[...] = jnp.zeros_like(h_ref)

    u = u_ref[...]
    st_ref[...] = _dot(u.astype(BF16), bblk_ref[...])
    lam_re = jnp.broadcast_to(lam_ref[0:1, :], (bsz, ns))
    lam_im = jnp.broadcast_to(lam_ref[1:2, :], (bsz, ns))

    def step(t, carry):
        h_re, h_im = carry
        r = pl.multiple_of(t * bsz, bsz)
        n_re = lam_re * h_re - lam_im * h_im + st_ref[pl.ds(r, bsz), 0:ns]
        n_im = lam_re * h_im + lam_im * h_re + st_ref[pl.ds(r, bsz), ns:2 * ns]
        st_ref[pl.ds(r, bsz), 0:ns] = n_re
        st_ref[pl.ds(r, bsz), ns:2 * ns] = n_im
        return n_re, n_im

    h_re, h_im = lax.fori_loop(0, tt, step, (h_ref[0], h_ref[1]), unroll=8)
    h_ref[0] = h_re
    h_ref[1] = h_im
    y = _dot(st_ref[...].astype(BF16), cblk_ref[...]) + d_ref[...] * u
    y = jax.nn.gelu(y)
    y_ref[...] = y * jax.nn.sigmoid(_dot(y.astype(BF16), gw_ref[...]) + gb_ref[...])


def _s5(u2, bblk, cblk, lam, d, glu_w, glu_b, *, bsz, seqlen):
    width = u2.shape[1]
    ns = lam.shape[1]
    tt = 64
    rows = tt * bsz
    const = lambda i: (0, 0)
    return pl.pallas_call(
        functools.partial(_s5_kernel, bsz=bsz, tt=tt, ns=ns),
        out_shape=jax.ShapeDtypeStruct(u2.shape, F32),
        grid=(seqlen // tt,),
        in_specs=[
            pl.BlockSpec((rows, width), lambda i: (i, 0)),
            pl.BlockSpec(bblk.shape, const),
            pl.BlockSpec(cblk.shape, const),
            pl.BlockSpec(lam.shape, const),
            pl.BlockSpec(d.shape, const),
            pl.BlockSpec(glu_w.shape, const),
            pl.BlockSpec(glu_b.shape, const),
        ],
        out_specs=pl.BlockSpec((rows, width), lambda i: (i, 0)),
        scratch_shapes=[pltpu.VMEM((rows, 2 * ns), F32), pltpu.VMEM((2, bsz, ns), F32)],
        compiler_params=_cparams("arbitrary"),
        name="s5",
    )(u2, bblk, cblk, lam, d, glu_w, glu_b)


def _s5_params(lam_re, lam_im, log_dt, b_re, b_im, c_re, c_im, d_skip):
    g, n = lam_re.shape
    p = b_re.shape[-1]
    lam = lax.complex(lam_re.astype(F32), lam_im.astype(F32))
    step = jnp.exp(log_dt.astype(F32))[:, None]
    lam_bar = jnp.exp(lam * step)
    b_bar = ((lam_bar - 1.0) / lam)[..., None] * lax.complex(b_re.astype(F32), b_im.astype(F32))
    eye = jnp.eye(g, dtype=F32)
    b_re_blk = jnp.einsum('gnp,gh->gphn', b_bar.real, eye).reshape(g * p, g * n)
    b_im_blk = jnp.einsum('gnp,gh->gphn', b_bar.imag, eye).reshape(g * p, g * n)
    bblk = jnp.concatenate([b_re_blk, b_im_blk], axis=1)
    c_re_blk = jnp.einsum('gpn,gh->gnhp', c_re.astype(F32), eye).reshape(g * n, g * p)
    c_im_blk = jnp.einsum('gpn,gh->gnhp', c_im.astype(F32), eye).reshape(g * n, g * p)
    cblk = jnp.concatenate([c_re_blk, -c_im_blk], axis=0)
    lam2 = jnp.stack([lam_bar.real.reshape(g * n), lam_bar.imag.reshape(g * n)])
    return bblk.astype(BF16), cblk.astype(BF16), lam2, d_skip.astype(F32).reshape(1, g * p)


def _moba_kernel(qt_ref, k_ref, vt_ref, o_ref, km_ref, ot_ref, *, nh, nb):
    i = pl.program_id(1)
    blk = MOBA_BLOCK
    hd = HEAD_DIM

    @pl.when(i == 0)
    def _():
        avg = jnp.full((1, blk), 1.0 / blk, F32)
        for h in range(nh):
            for n in range(nb):
                km_ref[h, n:n + 1, :] = _dot(avg, k_ref[0, h, n], precision=HI)

    n_iota = _iota((nb, blk), 0)
    kpos = _iota((blk, blk), 0)
    qpos = _iota((blk, blk), 1)
    causal = jnp.where(kpos <= qpos, 1.0, 0.0)
    for h in range(nh):
        qt = qt_ref[0, h, 0] * (hd ** -0.5)
        gate = _dot(km_ref[h], qt, precision=HI)
        gate = jnp.where(n_iota < i, gate, -jnp.inf)
        rank = jnp.zeros((nb, blk), I32)
        for m in range(nb):
            gm = gate[m:m + 1, :]
            ahead = (gm > gate) | ((gm == gate) & (m < n_iota))
            rank = rank + ahead.astype(I32)
        sel = jnp.where((n_iota < i) & (rank < MOBA_TOPK), 1.0, 0.0)
        qtb = qt.astype(BF16)

        def body(j, carry):
            m_run, l_run, acc = carry
            s = _dot(k_ref[0, h, j].astype(BF16), qtb)
            selj = jnp.max(jnp.where(n_iota == j, sel, 0.0), axis=0, keepdims=True)
            valid = jnp.where(j == i, causal, jnp.broadcast_to(selj, (blk, blk))) > 0.0
            s = jnp.where(valid, s, NEG_BIG)
            m_new = jnp.maximum(m_run, jnp.max(s, axis=0, keepdims=True))
            p = jnp.where(valid, jnp.exp(s - m_new), 0.0)
            corr = jnp.exp(m_run - m_new)
            l_new = l_run * corr + jnp.sum(p, axis=0, keepdims=True)
            acc = acc * corr + _dot(vt_ref[0, h, j].astype(BF16), p.astype(BF16))
            return m_new, l_new, acc

        init = (jnp.full((1, blk), NEG_BIG, F32), jnp.zeros((1, blk), F32), jnp.zeros((hd, blk), F32))
        _, l_run, acc = lax.fori_loop(0, i + 1, body, init)
        ot_ref[h * hd:(h + 1) * hd, :] = acc / l_run
    eye = (kpos == qpos).astype(F32)
    o_ref[0] = _dot_nt(eye, ot_ref[...], precision=HI)


def _moba(qt, k, vt, *, seqlen):
    bsz, nh, nb, hd, blk = qt.shape
    return pl.pallas_call(
        functools.partial(_moba_kernel, nh=nh, nb=nb),
        out_shape=jax.ShapeDtypeStruct((bsz, seqlen, nh * hd), F32),
        grid=(bsz, nb),
        in_specs=[
            pl.BlockSpec((1, nh, 1, hd, blk), lambda b, i: (b, 0, i, 0, 0)),
            pl.BlockSpec((1, nh, nb, blk, hd), lambda b, i: (b, 0, 0, 0, 0)),
            pl.BlockSpec((1, nh, nb, hd, blk), lambda b, i: (b, 0, 0, 0, 0)),
        ],
        out_specs=pl.BlockSpec((1, blk, nh * hd), lambda b, i: (b, i, 0)),
        scratch_shapes=[pltpu.VMEM((nh, nb, hd), F32), pltpu.VMEM((nh * hd, blk), F32)],
        compiler_params=_cparams("parallel", "arbitrary"),
        name="moba",
    )(qt, k, vt)


def _gdn_kernel(q_ref, k_ref, v_ref, z_ref, ab_ref, cw_ref, al_ref, dtb_ref, nw_ref, o_ref,
                qn_ref, kn_ref, vn_ref, g_ref, beta_ref, s_ref, *, nh, seqlen):
    npair = nh // 2
    ch = GDN_CHUNK
    hd = HEAD_DIM
    gw = nh * hd
    pw = 2 * hd

    rr = _iota((pw, pw), 0)
    cc = _iota((pw, pw), 1)
    same_head = (rr // hd) == (cc // hd)
    head_ones = same_head.astype(F32)

    a = ab_ref[0]
    ea = a + dtb_ref[...]
    softplus = jnp.maximum(ea, 0.0) + jnp.log1p(jnp.exp(-jnp.abs(ea)))
    g_ref[...] = -jnp.exp(al_ref[...]) * softplus
    beta_ref[...] = jax.nn.sigmoid(a)
    row = _iota((seqlen, pw), 0)
    for t, (src, dst) in enumerate(((q_ref, qn_ref), (k_ref, kn_ref), (v_ref, vn_ref))):
        for p in range(npair):
            x = src[0, :, p * pw:(p + 1) * pw]
            c0 = t * gw + p * pw
            acc = x * cw_ref[3:4, c0:c0 + pw]
            for tap in range(3):
                shift = 3 - tap
                xs = jnp.where(row >= shift, pltpu.roll(x, shift, 0), 0.0)
                acc = acc + xs * cw_ref[tap:tap + 1, c0:c0 + pw]
            y = acc * jax.nn.sigmoid(acc)
            if t < 2:
                y = y * lax.rsqrt(_dot(y * y, head_ones, precision=HI) + RMS_EPS)
            if t == 0:
                y = y * (hd ** -0.5)
            dst[p] = y

    tril_c = (_iota((ch, ch), 0) >= _iota((ch, ch), 1)).astype(F32)
    pos_r = rr % ch
    pos_c = cc % ch
    mask_incl = same_head & (pos_r >= pos_c)
    mask_strict = same_head & (pos_r > pos_c)
    eye = (rr == cc).astype(F32)
    ones = jnp.ones((pw, pw), F32)
    lane_c = _iota((ch, LANE), 1)
    first = lane_c < hd
    row_col = _iota((pw, 1), 0)
    lane_row = _iota((1, pw), 1)
    s_ref[...] = jnp.zeros_like(s_ref)

    def stack(x):
        return jnp.concatenate([jnp.where(first, x, 0.0), jnp.where(first, 0.0, x)], axis=0)

    def chunk(c, carry):
        r0 = pl.multiple_of(c * ch, ch)
        gcs = _dot(tril_c, g_ref[pl.ds(r0, ch), :], precision=HI)
        bch = beta_ref[pl.ds(r0, ch), :]
        for p in range(npair):
            ha, hb = 2 * p, 2 * p + 1
            gst = jnp.concatenate([jnp.where(lane_c == ha, gcs, 0.0), jnp.where(lane_c == hb, gcs, 0.0)], axis=0)
            bst = jnp.concatenate([jnp.where(lane_c == nh + ha, bch, 0.0),
                                   jnp.where(lane_c == nh + hb, bch, 0.0)], axis=0)
            gcol = jnp.sum(gst, axis=1, keepdims=True)
            bcol = jnp.sum(bst, axis=1, keepdims=True)
            grow = _dot_nt(ones, gst, precision=HI)
            decay = jnp.where(mask_incl, jnp.exp(jnp.where(mask_incl, gcol - grow, 0.0)), 0.0)
            ks = stack(kn_ref[p, pl.ds(r0, ch), :])
            qs = stack(qn_ref[p, pl.ds(r0, ch), :])
            vs = stack(vn_ref[p, pl.ds(r0, ch), :])
            kbs = ks * bcol
            vbs = vs * bcol
            ksb = ks.astype(BF16)
            lm = jnp.where(mask_strict, _dot_nt(kbs.astype(BF16), ksb) * decay, 0.0)
            pm = lm
            tm = eye - lm
            for _ in range(5):
                pmb = pm.astype(BF16)
                pm = _dot(pmb, pmb)
                tm = tm + _dot(tm.astype(BF16), pm.astype(BF16))
            eg = jnp.exp(gcol)
            rhs = jnp.concatenate([vbs, kbs * eg], axis=1).astype(BF16)
            sol = _dot(tm.astype(BF16), rhs)
            us = sol[:, :pw]
            ws = sol[:, pw:]
            attn = jnp.where(mask_incl, _dot_nt(qs.astype(BF16), ksb) * decay, 0.0)
            state = s_ref[p]
            wq = jnp.concatenate([ws, qs * eg], axis=0).astype(BF16)
            wqs = _dot(wq, state.astype(BF16))
            v_new = us - wqs[:pw]
            o_s = wqs[pw:] + _dot(attn.astype(BF16), v_new.astype(BF16))
            o_ref[0, pl.ds(r0, ch), p * pw:(p + 1) * pw] = o_s[:ch] + o_s[ch:]
            g_a = gcol[ch - 1:ch]
            g_b = gcol[2 * ch - 1:2 * ch]
            glast = jnp.where(row_col < ch, g_a, g_b)
            kdec = ks * jnp.exp(glast - gcol)
            e_last = jnp.where(lane_row < hd, jnp.exp(g_a), jnp.exp(g_b))
            s_ref[p] = state * e_last + _dot_tn(kdec.astype(BF16), v_new.astype(BF16))
        return carry

    lax.fori_loop(0, seqlen // ch, chunk, 0)

    for p in range(npair):
        o = o_ref[0, :, p * pw:(p + 1) * pw]
        ms = _dot(o * o, head_ones, precision=HI) * (1.0 / hd)
        z = z_ref[0, :, p * pw:(p + 1) * pw]
        o_ref[0, :, p * pw:(p + 1) * pw] = o * lax.rsqrt(ms + RMS_EPS) * nw_ref[...] * (z * jax.nn.sigmoid(z))


def _gdn_params(conv_w, a_log, dt_bias, norm_w):
    nh = a_log.shape[0]
    lane_pad = lambda v: jnp.pad(v.astype(F32), (0, LANE - nh)).reshape(1, LANE)
    return conv_w.astype(F32), lane_pad(a_log), lane_pad(dt_bias), jnp.tile(norm_w.astype(F32), 2).reshape(1, LANE)


def _gdn(q, k, v, z, ab, conv_w, al, dtb, nw):
    bsz, seqlen, gw = q.shape
    nh = gw // HEAD_DIM
    npair = nh // 2
    seq_spec = pl.BlockSpec((1, seqlen, gw), lambda b: (b, 0, 0))
    const = lambda b: (0, 0)
    return pl.pallas_call(
        functools.partial(_gdn_kernel, nh=nh, seqlen=seqlen),
        out_shape=jax.ShapeDtypeStruct((bsz, seqlen, gw), F32),
        grid=(bsz,),
        in_specs=[seq_spec, seq_spec, seq_spec, seq_spec,
                  pl.BlockSpec((1, seqlen, LANE), lambda b: (b, 0, 0)),
                  pl.BlockSpec(conv_w.shape, const),
                  pl.BlockSpec(al.shape, const),
                  pl.BlockSpec(dtb.shape, const),
                  pl.BlockSpec(nw.shape, const)],
        out_specs=seq_spec,
        scratch_shapes=[pltpu.VMEM((npair, seqlen, LANE), F32)] * 3
        + [pltpu.VMEM((seqlen, LANE), F32)] * 2
        + [pltpu.VMEM((npair, LANE, LANE), F32)],
        compiler_params=_cparams("parallel"),
        name="gdn",
    )(q, k, v, z, ab, conv_w, al, dtb, nw)


def _route(logits, n_exp):
    lane = _iota(logits.shape, 1)
    lg = jnp.where(lane < n_exp, logits, -jnp.inf)
    m1 = jnp.max(lg, axis=-1, keepdims=True)
    i1 = jnp.min(jnp.where(lg == m1, lane, LANE), axis=-1, keepdims=True)
    lg2 = jnp.where(lane == i1, -jnp.inf, lg)
    m2 = jnp.max(lg2, axis=-1, keepdims=True)
    i2 = jnp.min(jnp.where(lg2 == m2, lane, LANE), axis=-1, keepdims=True)
    e2 = jnp.exp(m2 - m1)
    g1 = 1.0 / (1.0 + e2)
    g2 = e2 / (1.0 + e2)
    gates = jnp.where(lane == i1, g1, 0.0) + jnp.where(lane == i2, g2, 0.0)
    mask = jnp.where((lane == i1 + n_exp) | (lane == i2 + n_exp), 1.0, 0.0)
    return gates + mask


def _outproj_kernel(x_ref, ys_ref, ym_ref, yg_ref, w_ref, g_ref, b_ref, *rest, alpha, s5w, mw, n_exp):
    mix = _dot(ys_ref[...].astype(BF16), w_ref[0:s5w, :])
    mix = mix + _dot(ym_ref[0].astype(BF16), w_ref[s5w:s5w + mw, :])
    mix = mix + _dot(yg_ref[0].astype(BF16), w_ref[s5w + mw:, :])
    x1 = _layer_norm(alpha * x_ref[...] + mix, g_ref[...], b_ref[...])
    if n_exp:
        rw_ref, o_ref, r_ref = rest
        o_ref[...] = x1
        r_ref[...] = _route(_dot(x1, rw_ref[...], precision=HI), n_exp)
    else:
        (o_ref,) = rest
        o_ref[...] = x1


def _outproj(x2, ys, ym, yg, w, g, b, router_w, *, bsz, seqlen, alpha, n_exp):
    tl = 512
    nt = seqlen // tl
    t, d = x2.shape
    s5w = ys.shape[1] // bsz
    mw = ym.shape[2]
    gw = yg.shape[2]
    const = lambda bi, j: (0, 0)
    row_spec = pl.BlockSpec((tl, d), lambda bi, j: (bi * nt + j, 0))
    in_specs = [row_spec,
                pl.BlockSpec((tl, s5w), lambda bi, j: (j, bi)),
                pl.BlockSpec((1, tl, mw), lambda bi, j: (bi, j, 0)),
                pl.BlockSpec((1, tl, gw), lambda bi, j: (bi, j, 0)),
                pl.BlockSpec(w.shape, const), pl.BlockSpec(g.shape, const), pl.BlockSpec(b.shape, const)]
    args = [x2, ys, ym, yg, w, g, b]
    out_shape = jax.ShapeDtypeStruct((t, d), F32)
    out_specs = row_spec
    if n_exp:
        in_specs.append(pl.BlockSpec(router_w.shape, const))
        args.append(router_w)
        out_shape = (out_shape, jax.ShapeDtypeStruct((t, LANE), F32))
        out_specs = (row_spec, pl.BlockSpec((tl, LANE), lambda bi, j: (bi * nt + j, 0)))
    return pl.pallas_call(
        functools.partial(_outproj_kernel, alpha=alpha, s5w=s5w, mw=mw, n_exp=n_exp),
        out_shape=out_shape,
        grid=(bsz, nt),
        in_specs=in_specs,
        out_specs=out_specs,
        compiler_params=_cparams("parallel", "arbitrary"),
        name="outproj_route" if n_exp else "outproj",
    )(*args)


def _ffn_body(x_ref, wg_ref, wu_ref, wd_ref, xb_ref, acc_ref):
    f = pl.program_id(1)

    @pl.when(f == 0)
    def _():
        xb_ref[...] = x_ref[...].astype(BF16)
        acc_ref[...] = jnp.zeros_like(acc_ref)

    xb = xb_ref[...]
    hg = _dot(xb, wg_ref[0])
    hu = _dot(xb, wu_ref[0])
    h = (hg * jax.nn.sigmoid(hg)) * hu
    acc_ref[...] += _dot(h.astype(BF16), wd_ref[0])


def _ffn_dense_kernel(x_ref, wg_ref, wu_ref, wd_ref, g_ref, b_ref, o_ref, xb_ref, acc_ref, *, alpha):
    _ffn_body(x_ref, wg_ref, wu_ref, wd_ref, xb_ref, acc_ref)

    @pl.when(pl.program_id(1) == pl.num_programs(1) - 1)
    def _():
        o_ref[...] = _layer_norm(alpha * x_ref[...] + acc_ref[...], g_ref[...], b_ref[...])


def _ffn_dense(x2, wg, wu, wd, g, b, *, alpha):
    t, d = x2.shape
    ff = wg.shape[2]
    tm, tf = FFN_ROWS, FFN_COLS
    const = lambda i, f: (0, 0)
    return pl.pallas_call(
        functools.partial(_ffn_dense_kernel, alpha=alpha),
        out_shape=jax.ShapeDtypeStruct((t, d), F32),
        grid=(t // tm, ff // tf),
        in_specs=[pl.BlockSpec((tm, d), lambda i, f: (i, 0)),
                  pl.BlockSpec((1, d, tf), lambda i, f: (0, 0, f)),
                  pl.BlockSpec((1, d, tf), lambda i, f: (0, 0, f)),
                  pl.BlockSpec((1, tf, d), lambda i, f: (0, f, 0)),
                  pl.BlockSpec(g.shape, const), pl.BlockSpec(b.shape, const)],
        out_specs=pl.BlockSpec((tm, d), lambda i, f: (i, 0)),
        scratch_shapes=[pltpu.VMEM((tm, d), BF16), pltpu.VMEM((tm, d), F32)],
        compiler_params=_cparams("parallel", "arbitrary"),
        name="ffn_dense",
    )(x2, wg, wu, wd, g, b)


def _ffn_moe_kernel(te_ref, tv_ref, x_ref, wg_ref, wu_ref, wd_ref, o_ref, xb_ref, acc_ref):
    i = pl.program_id(0)

    @pl.when(tv_ref[i] > 0)
    def _():
        _ffn_body(x_ref, wg_ref, wu_ref, wd_ref, xb_ref, acc_ref)

        @pl.when(pl.program_id(1) == pl.num_programs(1) - 1)
        def _():
            o_ref[...] = acc_ref[...]

    @pl.when((tv_ref[i] == 0) & (pl.program_id(1) == pl.num_programs(1) - 1))
    def _():
        o_ref[...] = jnp.zeros_like(o_ref)


def _ffn_moe(tile_expert, tile_valid, xs, wg, wu, wd):
    p_rows, d = xs.shape
    ff = wg.shape[2]
    tm, tf = MOE_ROWS, FFN_COLS
    nf = ff // tf

    def fsel(i, f, tv):
        return jnp.where(tv[i] > 0, f, nf - 1)

    grid_spec = pltpu.PrefetchScalarGridSpec(
        num_scalar_prefetch=2,
        grid=(p_rows // tm, nf),
        in_specs=[pl.BlockSpec((tm, d), lambda i, f, te, tv: (i, 0)),
                  pl.BlockSpec((1, d, tf), lambda i, f, te, tv: (te[i], 0, fsel(i, f, tv))),
                  pl.BlockSpec((1, d, tf), lambda i, f, te, tv: (te[i], 0, fsel(i, f, tv))),
                  pl.BlockSpec((1, tf, d), lambda i, f, te, tv: (te[i], fsel(i, f, tv), 0))],
        out_specs=pl.BlockSpec((tm, d), lambda i, f, te, tv: (i, 0)),
        scratch_shapes=[pltpu.VMEM((tm, d), BF16), pltpu.VMEM((tm, d), F32)],
    )
    return pl.pallas_call(
        _ffn_moe_kernel,
        out_shape=jax.ShapeDtypeStruct((p_rows, d), F32),
        grid_spec=grid_spec,
        compiler_params=_cparams("arbitrary", "arbitrary"),
        name="ffn_moe",
    )(tile_expert, tile_valid, xs, wg, wu, wd)


def _row_copy(src_hbm, idx, dst_ref, row, sem):
    return pltpu.make_async_copy(src_hbm.at[pl.ds(idx, 1)], dst_ref.at[pl.ds(row, 1)], sem)


def _gather_kernel(idx_ref, x_hbm, o_ref, sem, *, rows):
    def start(r, carry):
        _row_copy(x_hbm, idx_ref[0, 0, r], o_ref, r, sem).start()
        return carry

    lax.fori_loop(0, rows, start, 0, unroll=8)

    def wait(r, carry):
        _row_copy(x_hbm, 0, o_ref, r, sem).wait()
        return carry

    lax.fori_loop(0, rows, wait, 0, unroll=8)


def _gather_rows(idx, x2):
    nt, _, rows = idx.shape
    d = x2.shape[1]
    return pl.pallas_call(
        functools.partial(_gather_kernel, rows=rows),
        out_shape=jax.ShapeDtypeStruct((nt * rows, d), x2.dtype),
        grid=(nt,),
        in_specs=[pl.BlockSpec((1, 1, rows), lambda i: (i, 0, 0), memory_space=pltpu.SMEM),
                  pl.BlockSpec(memory_space=pl.ANY)],
        out_specs=pl.BlockSpec((rows, d), lambda i: (i, 0)),
        scratch_shapes=[pltpu.SemaphoreType.DMA(())],
        compiler_params=_cparams("arbitrary"),
        name="moe_gather",
    )(idx, x2)


def _combine_kernel(p0_ref, p1_ref, ys_hbm, x_ref, gt_ref, g_ref, b_ref, o_ref, y0_ref, y1_ref, sem, *, rows, alpha):
    def start(r, carry):
        _row_copy(ys_hbm, p0_ref[0, 0, r], y0_ref, r, sem.at[0]).start()
        _row_copy(ys_hbm, p1_ref[0, 0, r], y1_ref, r, sem.at[1]).start()
        return carry

    lax.fori_loop(0, rows, start, 0, unroll=8)

    def wait(r, carry):
        _row_copy(ys_hbm, 0, y0_ref, r, sem.at[0]).wait()
        _row_copy(ys_hbm, 0, y1_ref, r, sem.at[1]).wait()
        return carry

    lax.fori_loop(0, rows, wait, 0, unroll=8)
    gt = gt_ref[...]
    f = gt[:, 0:1] * y0_ref[...] + gt[:, 1:2] * y1_ref[...]
    o_ref[...] = _layer_norm(alpha * x_ref[...] + f, g_ref[...], b_ref[...])


def _combine(p0, p1, ys, x2, gates, g, b, *, alpha):
    nt, _, rows = p0.shape
    t, d = x2.shape
    const = lambda i: (0, 0)
    idx_spec = pl.BlockSpec((1, 1, rows), lambda i: (i, 0, 0), memory_space=pltpu.SMEM)
    return pl.pallas_call(
        functools.partial(_combine_kernel, rows=rows, alpha=alpha),
        out_shape=jax.ShapeDtypeStruct((t, d), F32),
        grid=(nt,),
        in_specs=[idx_spec, idx_spec,
                  pl.BlockSpec(memory_space=pl.ANY),
                  pl.BlockSpec((rows, d), lambda i: (i, 0)),
                  pl.BlockSpec((rows, LANE), lambda i: (i, 0)),
                  pl.BlockSpec(g.shape, const), pl.BlockSpec(b.shape, const)],
        out_specs=pl.BlockSpec((rows, d), lambda i: (i, 0)),
        scratch_shapes=[pltpu.VMEM((rows, d), F32), pltpu.VMEM((rows, d), F32), pltpu.SemaphoreType.DMA((2,))],
        compiler_params=_cparams("arbitrary"),
        name="moe_combine",
    )(p0, p1, ys, x2, gates, g, b)


def _dispatch_plan(route, n_exp):
    t = route.shape[0]
    tm = MOE_ROWS
    gates = route[:, :n_exp]
    sel = route[:, n_exp:2 * n_exp] > 0.5
    seli = sel.astype(I32)
    rank = jnp.cumsum(seli, axis=0) - seli
    counts = jnp.sum(seli, axis=0)
    padded = ((counts + tm - 1) // tm) * tm
    ends = jnp.cumsum(padded)
    starts = ends - padded
    pos = starts[None, :] + rank
    p_rows = t * TOP_K + n_exp * tm
    n_tiles = p_rows // tm
    tok = jnp.broadcast_to(jnp.arange(t, dtype=I32)[:, None], (t, n_exp))
    src = jnp.zeros((p_rows,), I32).at[jnp.where(sel, pos, p_rows)].set(tok, mode="drop")
    tile_start = jnp.arange(n_tiles, dtype=I32) * tm
    tile_expert = jnp.minimum(jnp.sum((tile_start[:, None] >= ends[None, :]).astype(I32), axis=1), n_exp - 1)
    tile_valid = (tile_start < starts[tile_expert] + counts[tile_expert]).astype(I32)
    order = jnp.cumsum(seli, axis=1) - seli
    first = sel & (order == 0)
    second = sel & (order == 1)
    pick = lambda m, v: jnp.sum(jnp.where(m, v, 0), axis=1)
    p0, p1 = pick(first, pos), pick(second, pos)
    g01 = jnp.stack([pick(first, gates), pick(second, gates)], axis=1)
    g01 = jnp.pad(g01, ((0, 0), (0, LANE - TOP_K)))
    shape3 = lambda v, rows: v.astype(I32).reshape(-1, 1, rows)
    return (shape3(src, GATHER_ROWS), tile_expert.astype(I32), tile_valid,
            shape3(p0, GATHER_ROWS), shape3(p1, GATHER_ROWS), g01)


def kernel(x, w_in, w_out, s5_lam_re, s5_lam_im, s5_log_dt, s5_b_re, s5_b_im, s5_c_re, s5_c_im, s5_d, s5_glu_w,
           s5_glu_b, gdn_conv_w, gdn_a_log, gdn_dt_bias, gdn_norm_w, ln1_g, ln1_b, ln2_g, ln2_b, ffn_w_gate,
           ffn_w_up, ffn_w_down, moe_router, moe_w_gate, moe_w_up, moe_w_down):
    bsz, seqlen, d = x.shape
    depth = w_in.shape[0]
    s5w = s5_glu_w.shape[1]
    gw = gdn_a_log.shape[1] * HEAD_DIM
    mw = w_out.shape[1] - s5w - gw
    n_exp = moe_router.shape[2]
    alpha = (2 * depth) ** 0.25
    row = lambda v: v.astype(F32).reshape(1, -1)

    x2 = x.reshape(bsz * seqlen, d)
    for l in range(depth):
        w, wt = _pack_inproj(w_in[l], s5w, mw, gw)
        u, mqt, mk, mvt, gq, gk, gv, gz, ab = _inproj(x2, w, wt, bsz=bsz, seqlen=seqlen, s5w=s5w, mw=mw, gw=gw)
        bblk, cblk, lam, dskip = _s5_params(s5_lam_re[l], s5_lam_im[l], s5_log_dt[l], s5_b_re[l], s5_b_im[l],
                                            s5_c_re[l], s5_c_im[l], s5_d[l])
        y_s5 = _s5(u.reshape(seqlen * bsz, s5w), bblk, cblk, lam, dskip, s5_glu_w[l].astype(BF16),
                   row(s5_glu_b[l]), bsz=bsz, seqlen=seqlen).reshape(seqlen, bsz * s5w)
        y_moba = _moba(mqt, mk, mvt, seqlen=seqlen)
        y_gdn = _gdn(gq, gk, gv, gz, ab, *_gdn_params(gdn_conv_w[l], gdn_a_log[l], gdn_dt_bias[l], gdn_norm_w[l]))
        routed = l % 2 == 1
        router_w = jnp.pad(moe_router[l // 2].astype(F32), ((0, 0), (0, LANE - n_exp))) if routed else None
        res = _outproj(x2, y_s5, y_moba, y_gdn, w_out[l].astype(BF16), row(ln1_g[l]), row(ln1_b[l]), router_w,
                       bsz=bsz, seqlen=seqlen, alpha=alpha, n_exp=n_exp if routed else 0)
        if routed:
            x1, route = res
            src, tile_expert, tile_valid, p0, p1, g01 = _dispatch_plan(route, n_exp)
            xs = _gather_rows(src, x1)
            ys = _ffn_moe(tile_expert, tile_valid, xs, moe_w_gate[l // 2].astype(BF16),
                          moe_w_up[l // 2].astype(BF16), moe_w_down[l // 2].astype(BF16))
            x2 = _combine(p0, p1, ys, x1, g01, row(ln2_g[l]), row(ln2_b[l]), alpha=alpha)
        else:
            x2 = _ffn_dense(res, ffn_w_gate[l // 2:l // 2 + 1].astype(BF16), ffn_w_up[l // 2:l // 2 + 1].astype(BF16),
                            ffn_w_down[l // 2:l // 2 + 1].astype(BF16), row(ln2_g[l]), row(ln2_b[l]), alpha=alpha)
    return x2.reshape(bsz, seqlen, d)
```

```python
import functools

import jax
import jax.numpy as jnp
from jax import lax
from jax.experimental import pallas as pl
from jax.experimental.pallas import tpu as pltpu

F32 = jnp.float32
BF16 = jnp.bfloat16
I32 = jnp.int32
HI = lax.Precision.HIGHEST

HEAD_DIM = 64
MOBA_BLOCK = 256
MOBA_TOPK = 3
GDN_CHUNK = 64
GDN_PREP_CHUNKS = 2
TOP_K = 2
LN_EPS = 1e-5
RMS_EPS = 1e-6
NEG_BIG = -1e30

V7X_VMEM_LIMIT_BYTES = 56 * 1024 * 1024
LANE = 128

FFN_ROWS = 1024
MOE_ROWS = 512
FFN_COLS = 512
GATHER_ROWS = 256


def _cparams(*sem):
    return pltpu.CompilerParams(dimension_semantics=sem, vmem_limit_bytes=V7X_VMEM_LIMIT_BYTES)


def _dot(a, b, precision=None):
    return jnp.dot(a, b, preferred_element_type=F32, precision=precision)


def _dot_nt(a, b, precision=None):
    return lax.dot_general(a, b, (((1,), (1,)), ((), ())), preferred_element_type=F32, precision=precision)


def _dot_tn(a, b, precision=None):
    return lax.dot_general(a, b, (((0,), (0,)), ((), ())), preferred_element_type=F32, precision=precision)


def _iota(shape, dim):
    return lax.broadcasted_iota(I32, shape, dim)


def _layer_norm(r, g, b):
    mu = jnp.mean(r, axis=-1, keepdims=True)
    c = r - mu
    var = jnp.mean(c * c, axis=-1, keepdims=True)
    return c * lax.rsqrt(var + LN_EPS) * g + b


def _inproj_kernel(x_ref, w_ref, wt_ref, u_ref, mqt_ref, mk_ref, mvt_ref, gq_ref, gk_ref, gv_ref, gz_ref,
                   ab_ref, *, s5w, mw, gw):
    xb = x_ref[...].astype(BF16)
    nh_m = mw // HEAD_DIM

    def cols(off, width):
        return _dot(xb, w_ref[:, off:off + width])

    u_ref[...] = cols(0, s5w)
    off = s5w + mw
    kk = cols(off, mw)
    for h in range(nh_m):
        mk_ref[0, h, 0] = kk[:, h * HEAD_DIM:(h + 1) * HEAD_DIM]
    off += 2 * mw
    for ref in (gq_ref, gk_ref, gv_ref, gz_ref):
        ref[0] = cols(off, gw)
        off += gw
    ab_ref[0] = cols(off, LANE)
    qt = _dot_nt(wt_ref[0:mw, :], xb)
    vt = _dot_nt(wt_ref[mw:2 * mw, :], xb)
    for h in range(nh_m):
        mqt_ref[0, h, 0] = qt[h * HEAD_DIM:(h + 1) * HEAD_DIM, :]
        mvt_ref[0, h, 0] = vt[h * HEAD_DIM:(h + 1) * HEAD_DIM, :]


def _pack_inproj(w_in, s5w, mw, gw):
    c = w_in.shape[1]
    main = s5w + 3 * mw + 4 * gw
    w = jnp.pad(w_in, ((0, 0), (0, main + LANE - c))).astype(BF16)
    q0, v0 = s5w, s5w + 2 * mw
    wt = jnp.concatenate([w_in[:, q0:q0 + mw], w_in[:, v0:v0 + mw]], axis=1).T.astype(BF16)
    return w, wt


def _inproj(x2, w, wt, *, bsz, seqlen, s5w, mw, gw):
    tl = MOBA_BLOCK
    nb = seqlen // tl
    d = x2.shape[1]
    nh_m = mw // HEAD_DIM
    hd = HEAD_DIM
    gdn_sds = jax.ShapeDtypeStruct((bsz, seqlen, gw), F32)
    out_shape = (
        jax.ShapeDtypeStruct((seqlen, bsz * s5w), F32),
        jax.ShapeDtypeStruct((bsz, nh_m, nb, hd, tl), F32),
        jax.ShapeDtypeStruct((bsz, nh_m, nb, tl, hd), F32),
        jax.ShapeDtypeStruct((bsz, nh_m, nb, hd, tl), F32),
        gdn_sds, gdn_sds, gdn_sds, gdn_sds,
        jax.ShapeDtypeStruct((bsz, seqlen, LANE), F32),
    )
    t5 = lambda b, j: (b, 0, j, 0, 0)
    t3 = lambda b, j: (b, j, 0)
    out_specs = (
        pl.BlockSpec((tl, s5w), lambda b, j: (j, b)),
        pl.BlockSpec((1, nh_m, 1, hd, tl), t5),
        pl.BlockSpec((1, nh_m, 1, tl, hd), t5),
        pl.BlockSpec((1, nh_m, 1, hd, tl), t5),
        pl.BlockSpec((1, tl, gw), t3),
        pl.BlockSpec((1, tl, gw), t3),
        pl.BlockSpec((1, tl, gw), t3),
        pl.BlockSpec((1, tl, gw), t3),
        pl.BlockSpec((1, tl, LANE), t3),
    )
    return pl.pallas_call(
        functools.partial(_inproj_kernel, s5w=s5w, mw=mw, gw=gw),
        out_shape=out_shape,
        grid=(bsz, nb),
        in_specs=[
            pl.BlockSpec((tl, d), lambda b, j: (b * nb + j, 0)),
            pl.BlockSpec(w.shape, lambda b, j: (0, 0)),
            pl.BlockSpec(wt.shape, lambda b, j: (0, 0)),
        ],
        out_specs=out_specs,
        compiler_params=_cparams("parallel", "arbitrary"),
        name="inproj",
    )(x2, w, wt)


def _s5_kernel(u_ref, bblk_ref, cblk_ref, lam_ref, d_ref, gw_ref, gb_ref, y_ref, st_ref, h_ref, *, bsz, tt, ns):
    @pl.when(pl.program_id(0) == 0)
    def _():
        h_ref[...] = jnp.zeros_like(h_ref)

    u = u_ref[...]
    st_ref[...] = _dot(u.astype(BF16), bblk_ref[...])
    lam_re = jnp.broadcast_to(lam_ref[0:1, :], (bsz, ns))
    lam_im = jnp.broadcast_to(lam_ref[1:2, :], (bsz, ns))

    def step(t, carry):
        h_re, h_im = carry
        r = pl.multiple_of(t * bsz, bsz)
        n_re = lam_re * h_re - lam_im * h_im + st_ref[pl.ds(r, bsz), 0:ns]
        n_im = lam_re * h_im + lam_im * h_re + st_ref[pl.ds(r, bsz), ns:2 * ns]
        st_ref[pl.ds(r, bsz), 0:ns] = n_re
        st_ref[pl.ds(r, bsz), ns:2 * ns] = n_im
        return n_re, n_im

    h_re, h_im = lax.fori_loop(0, tt, step, (h_ref[0], h_ref[1]), unroll=8)
    h_ref[0] = h_re
    h_ref[1] = h_im
    y = _dot(st_ref[...].astype(BF16), cblk_ref[...]) + d_ref[...] * u
    y = jax.nn.gelu(y)
    y_ref[...] = y * jax.nn.sigmoid(_dot(y.astype(BF16), gw_ref[...]) + gb_ref[...])


def _s5(u2, bblk, cblk, lam, d, glu_w, glu_b, *, bsz, seqlen):
    width = u2.shape[1]
    ns = lam.shape[1]
    tt = 64
    rows = tt * bsz
    const = lambda i: (0, 0)
    return pl.pallas_call(
        functools.partial(_s5_kernel, bsz=bsz, tt=tt, ns=ns),
        out_shape=jax.ShapeDtypeStruct(u2.shape, F32),
        grid=(seqlen // tt,),
        in_specs=[
            pl.BlockSpec((rows, width), lambda i: (i, 0)),
            pl.BlockSpec(bblk.shape, const),
            pl.BlockSpec(cblk.shape, const),
            pl.BlockSpec(lam.shape, const),
            pl.BlockSpec(d.shape, const),
            pl.BlockSpec(glu_w.shape, const),
            pl.BlockSpec(glu_b.shape, const),
        ],
        out_specs=pl.BlockSpec((rows, width), lambda i: (i, 0)),
        scratch_shapes=[pltpu.VMEM((rows, 2 * ns), F32), pltpu.VMEM((2, bsz, ns), F32)],
        compiler_params=_cparams("arbitrary"),
        name="s5",
    )(u2, bblk, cblk, lam, d, glu_w, glu_b)


def _s5_params(lam_re, lam_im, log_dt, b_re, b_im, c_re, c_im, d_skip):
    g, n = lam_re.shape
    p = b_re.shape[-1]
    lam = lax.complex(lam_re.astype(F32), lam_im.astype(F32))
    step = jnp.exp(log_dt.astype(F32))[:, None]
    lam_bar = jnp.exp(lam * step)
    b_bar = ((lam_bar - 1.0) / lam)[..., None] * lax.complex(b_re.astype(F32), b_im.astype(F32))
    eye = jnp.eye(g, dtype=F32)
    b_re_blk = jnp.einsum('gnp,gh->gphn', b_bar.real, eye).reshape(g * p, g * n)
    b_im_blk = jnp.einsum('gnp,gh->gphn', b_bar.imag, eye).reshape(g * p, g * n)
    bblk = jnp.concatenate([b_re_blk, b_im_blk], axis=1)
    c_re_blk = jnp.einsum('gpn,gh->gnhp', c_re.astype(F32), eye).reshape(g * n, g * p)
    c_im_blk = jnp.einsum('gpn,gh->gnhp', c_im.astype(F32), eye).reshape(g * n, g * p)
    cblk = jnp.concatenate([c_re_blk, -c_im_blk], axis=0)
    lam2 = jnp.stack([lam_bar.real.reshape(g * n), lam_bar.imag.reshape(g * n)])
    return bblk.astype(BF16), cblk.astype(BF16), lam2, d_skip.astype(F32).reshape(1, g * p)


def _moba_kernel(qt_ref, k_ref, vt_ref, o_ref, km_ref, qs_ref, bias_ref, acc_ref, *, nh, nb):
    i = pl.program_id(1)
    blk = MOBA_BLOCK
    hd = HEAD_DIM

    @pl.when(i == 0)
    def _():
        avg = jnp.full((1, blk), 1.0 / blk, F32)
        for h in range(nh):
            for n in range(nb):
                km_ref[h, n:n + 1, :] = _dot(avg, k_ref[0, h, n], precision=HI)

    n_iota = _iota((nb, blk), 0)
    kpos = _iota((blk, blk), 0)
    qpos = _iota((blk, blk), 1)
    causal = kpos <= qpos

    heads = range(nh)

    qts = [qt_ref[0, h, 0] * (hd ** -0.5) for h in heads]
    qtb = [qts[h].astype(BF16) for h in heads]
    s_own = [_dot(k_ref[0, h, i].astype(BF16), qtb[h]) for h in heads]
    gates = [_dot(km_ref[h], qts[h], precision=HI) for h in heads]
    vt_own = [vt_ref[0, h, i].astype(BF16) for h in heads]
    biases = []
    for h in heads:
        gate = jnp.where(n_iota < i, gates[h], -jnp.inf)
        rank = jnp.zeros((nb, blk), I32)
        for m in range(nb):
            gm = gate[m:m + 1, :]
            ahead = (gm > gate) | ((gm == gate) & (m < n_iota))
            rank = rank + ahead.astype(I32)
        biases.append(jnp.where((n_iota < i) & (rank < MOBA_TOPK), 0.0, NEG_BIG))
    m_own, l_own, acc_own = [], [], []
    for h in heads:
        s = jnp.where(causal, s_own[h], NEG_BIG)
        m0 = jnp.max(s, axis=0, keepdims=True)
        p = jnp.exp(s - m0)
        m_own.append(m0)
        l_own.append(jnp.sum(p, axis=0, keepdims=True))
        acc_own.append(_dot(vt_own[h], p.astype(BF16)))
    for h in heads:
        bias_ref[h] = biases[h]
        qs_ref[h] = qtb[h]
        acc_ref[h * hd:(h + 1) * hd, :] = acc_own[h]

    def body(j, carry):
        ms, ls = carry
        ss = [_dot(k_ref[0, h, j].astype(BF16), qs_ref[h]) + bias_ref[h, pl.ds(j, 1), :] for h in heads]
        vts = [vt_ref[0, h, j].astype(BF16) for h in heads]
        accs = [acc_ref[h * hd:(h + 1) * hd, :] for h in heads]
        new_ms, new_ls, new_accs = [], [], []
        for h in heads:
            m_new = jnp.maximum(ms[h], jnp.max(ss[h], axis=0, keepdims=True))
            p = jnp.exp(ss[h] - m_new)
            corr = jnp.exp(ms[h] - m_new)
            new_ms.append(m_new)
            new_ls.append(ls[h] * corr + jnp.sum(p, axis=0, keepdims=True))
            new_accs.append(accs[h] * corr + _dot(vts[h], p.astype(BF16)))
        for h in heads:
            acc_ref[h * hd:(h + 1) * hd, :] = new_accs[h]
        return tuple(new_ms), tuple(new_ls)

    _, ls = lax.fori_loop(0, i, body, (tuple(m_own), tuple(l_own)))
    for h in heads:
        rows = slice(h * hd, (h + 1) * hd)
        acc_ref[rows, :] = acc_ref[rows, :] / ls[h]
    eye = (kpos == qpos).astype(F32)
    o_ref[0] = _dot_nt(eye, acc_ref[...], precision=HI)


def _moba(qt, k, vt, *, seqlen):
    bsz, nh, nb, hd, blk = qt.shape
    return pl.pallas_call(
        functools.partial(_moba_kernel, nh=nh, nb=nb),
        out_shape=jax.ShapeDtypeStruct((bsz, seqlen, nh * hd), F32),
        grid=(bsz, nb),
        in_specs=[
            pl.BlockSpec((1, nh, 1, hd, blk), lambda b, i: (b, 0, i, 0, 0)),
            pl.BlockSpec((1, nh, nb, blk, hd), lambda b, i: (b, 0, 0, 0, 0)),
            pl.BlockSpec((1, nh, nb, hd, blk), lambda b, i: (b, 0, 0, 0, 0)),
        ],
        out_specs=pl.BlockSpec((1, blk, nh * hd), lambda b, i: (b, i, 0)),
        scratch_shapes=[pltpu.VMEM((nh, nb, hd), F32), pltpu.VMEM((nh, hd, blk), BF16),
                        pltpu.VMEM((nh, nb, blk), F32), pltpu.VMEM((nh * hd, blk), F32)],
        compiler_params=_cparams("parallel", "arbitrary"),
        name="moba",
    )(qt, k, vt)


def _gdn_kernel(q_ref, k_ref, v_ref, z_ref, ab_ref, cw_ref, al_ref, dtb_ref, nw_ref, o_ref,
                g_ref, beta_ref, us_ref, ws_ref, qe_ref, kd_ref, at_ref, el_ref, s_ref, *, nh, seqlen):
    npair = nh // 2
    ch = GDN_CHUNK
    hd = HEAD_DIM
    gw = nh * hd
    pw = 2 * hd
    pairs = range(npair)

    rr = _iota((pw, pw), 0)
    cc = _iota((pw, pw), 1)
    same_head = (rr // hd) == (cc // hd)
    mask_incl = same_head & ((rr % ch) >= (cc % ch))
    mask_strict = same_head & ((rr % ch) > (cc % ch))
    eye = (rr == cc).astype(F32)
    tril_c = (_iota((ch, ch), 0) >= _iota((ch, ch), 1)).astype(F32)
    lane_c = _iota((ch, LANE), 1)
    first = lane_c < hd
    row_col = _iota((pw, 1), 0)
    lane_row = _iota((1, pw), 1)

    def stack(x):
        return jnp.concatenate([jnp.where(first, x, 0.0), jnp.where(first, 0.0, x)], axis=0)

    def fold(x):
        return x[:ch] + x[ch:]

    def head_sumsq(y):
        y2 = y * y
        sa = jnp.sum(jnp.where(first, y2, 0.0), axis=1, keepdims=True)
        sb = jnp.sum(jnp.where(first, 0.0, y2), axis=1, keepdims=True)
        return jnp.where(first, sa, sb)

    a = ab_ref[0]
    ea = a + dtb_ref[...]
    g_ref[...] = -jnp.exp(al_ref[...]) * (jnp.maximum(ea, 0.0) + jnp.log1p(jnp.exp(-jnp.abs(ea))))
    beta_ref[...] = jax.nn.sigmoid(a)

    def conv_silu(src, t, p, c, r0):
        lanes = slice(p * pw, (p + 1) * pw)
        h0 = pl.multiple_of(jnp.maximum(r0 - 8, 0), 8)
        hist = jnp.where(c > 0, src[0, pl.ds(h0, 8), lanes], 0.0)
        x = jnp.concatenate([hist, src[0, pl.ds(r0, ch), lanes]], axis=0)
        c0 = t * gw + p * pw
        acc = x * cw_ref[3:4, c0:c0 + pw]
        for tap in range(3):
            acc = acc + pltpu.roll(x, 3 - tap, 0) * cw_ref[tap:tap + 1, c0:c0 + pw]
        y = acc[8:]
        return y * jax.nn.sigmoid(y)

    def prep(cc, carry):
        chunks = [cc * GDN_PREP_CHUNKS + u for u in range(GDN_PREP_CHUNKS)]
        r0s = [pl.multiple_of(c * ch, ch) for c in chunks]
        gcs = [_dot(tril_c, g_ref[pl.ds(r0, ch), :], precision=HI) for r0 in r0s]
        bch = [beta_ref[pl.ds(r0, ch), :] for r0 in r0s]
        streams = [(u, p) for u in range(GDN_PREP_CHUNKS) for p in pairs]
        ns = range(len(streams))
        qs, ks, vs, gcol, bcol, decay = [], [], [], [], [], []
        for u, p in streams:
            q = conv_silu(q_ref, 0, p, chunks[u], r0s[u])
            k = conv_silu(k_ref, 1, p, chunks[u], r0s[u])
            v = conv_silu(v_ref, 2, p, chunks[u], r0s[u])
            q = q * (lax.rsqrt(head_sumsq(q) + RMS_EPS) * (hd ** -0.5))
            k = k * lax.rsqrt(head_sumsq(k) + RMS_EPS)
            qs.append(stack(q))
            ks.append(stack(k))
            vs.append(stack(v))
            ha, hb = 2 * p, 2 * p + 1
            gst = jnp.concatenate([jnp.where(lane_c == ha, gcs[u], 0.0), jnp.where(lane_c == hb, gcs[u], 0.0)], axis=0)
            bst = jnp.concatenate([jnp.where(lane_c == nh + ha, bch[u], 0.0),
                                   jnp.where(lane_c == nh + hb, bch[u], 0.0)], axis=0)
            gc = jnp.sum(gst, axis=1, keepdims=True)
            gcol.append(gc)
            bcol.append(jnp.sum(bst, axis=1, keepdims=True))
            gmat = jnp.broadcast_to(gc, (pw, pw))
            decay.append(jnp.where(mask_incl, jnp.exp(jnp.where(mask_incl, gmat - gmat.T, 0.0)), 0.0))
        kbs = [ks[n] * bcol[n] for n in ns]
        ksb = [ks[n].astype(BF16) for n in ns]
        pm = [jnp.where(mask_strict, _dot_nt(kbs[n].astype(BF16), ksb[n]) * decay[n], 0.0) for n in ns]
        tm = [eye - pm[n] for n in ns]
        pmb = [pm[n].astype(BF16) for n in ns]
        pm = [_dot(pmb[n], pmb[n]) for n in ns]
        for _ in range(4):
            pmb = [pm[n].astype(BF16) for n in ns]
            tm = [tm[n] + _dot(tm[n].astype(BF16), pmb[n]) for n in ns]
            pm = [_dot(pmb[n], pmb[n]) for n in ns]
        tm = [tm[n] + _dot(tm[n].astype(BF16), pm[n].astype(BF16)) for n in ns]
        out = []
        for n in ns:
            eg = jnp.exp(gcol[n])
            rhs = jnp.concatenate([vs[n] * bcol[n], kbs[n] * eg], axis=1).astype(BF16)
            sol = _dot(tm[n].astype(BF16), rhs)
            attn = jnp.where(mask_incl, _dot_nt(qs[n].astype(BF16), ksb[n]) * decay[n], 0.0)
            g_a = gcol[n][ch - 1:ch]
            g_b = gcol[n][2 * ch - 1:2 * ch]
            glast = jnp.where(row_col < ch, g_a, g_b)
            e_last = jnp.where(lane_row < hd, jnp.exp(g_a), jnp.exp(g_b))
            out.append((fold(sol[:, :pw]), fold(sol[:, pw:]).astype(BF16), fold(qs[n] * eg).astype(BF16),
                        fold(ks[n] * jnp.exp(glast - gcol[n])).astype(BF16), fold(attn).astype(BF16),
                        jnp.broadcast_to(e_last, (8, pw))))
        for n, (u, p) in enumerate(streams):
            rows = pl.ds(r0s[u], ch)
            us_ref[p, rows, :], ws_ref[p, rows, :], qe_ref[p, rows, :], kd_ref[p, rows, :], at_ref[p, rows, :] = out[n][:5]
            el_ref[p, pl.ds(pl.multiple_of(chunks[u] * 8, 8), 8), :] = out[n][5]
        return carry

    lax.fori_loop(0, seqlen // (ch * GDN_PREP_CHUNKS), prep, 0)

    s_ref[...] = jnp.zeros_like(s_ref)

    def scan(c, carry):
        r0 = pl.multiple_of(c * ch, ch)
        rows = pl.ds(r0, ch)
        for p in pairs:
            lanes = slice(p * pw, (p + 1) * pw)
            state = s_ref[p]
            wq = jnp.concatenate([ws_ref[p, rows, :], qe_ref[p, rows, :]], axis=0)
            wqs = _dot(wq, state.astype(BF16))
            v_new = us_ref[p, rows, :] - wqs[:ch]
            o = wqs[ch:] + _dot(at_ref[p, rows, :], stack(v_new).astype(BF16))
            upd = _dot_tn(kd_ref[p, rows, :], v_new.astype(BF16))
            s_ref[p] = state * el_ref[p, pl.ds(c * 8, 1), :] + jnp.where(same_head, upd, 0.0)
            z = z_ref[0, rows, lanes]
            ms = head_sumsq(o) * (1.0 / hd)
            o_ref[0, rows, lanes] = o * lax.rsqrt(ms + RMS_EPS) * nw_ref[...] * (z * jax.nn.sigmoid(z))
        return carry

    lax.fori_loop(0, seqlen // ch, scan, 0)


def _gdn_params(conv_w, a_log, dt_bias, norm_w):
    nh = a_log.shape[0]
    lane_pad = lambda v: jnp.pad(v.astype(F32), (0, LANE - nh)).reshape(1, LANE)
    return conv_w.astype(F32), lane_pad(a_log), lane_pad(dt_bias), jnp.tile(norm_w.astype(F32), 2).reshape(1, LANE)


def _gdn(q, k, v, z, ab, conv_w, al, dtb, nw):
    bsz, seqlen, gw = q.shape
    nh = gw // HEAD_DIM
    npair = nh // 2
    seq_spec = pl.BlockSpec((1, seqlen, gw), lambda b: (b, 0, 0))
    const = lambda b: (0, 0)
    return pl.pallas_call(
        functools.partial(_gdn_kernel, nh=nh, seqlen=seqlen),
        out_shape=jax.ShapeDtypeStruct((bsz, seqlen, gw), F32),
        grid=(bsz,),
        in_specs=[seq_spec, seq_spec, seq_spec, seq_spec,
                  pl.BlockSpec((1, seqlen, LANE), lambda b: (b, 0, 0)),
                  pl.BlockSpec(conv_w.shape, const),
                  pl.BlockSpec(al.shape, const),
                  pl.BlockSpec(dtb.shape, const),
                  pl.BlockSpec(nw.shape, const)],
        out_specs=seq_spec,
        scratch_shapes=[pltpu.VMEM((seqlen, LANE), F32)] * 2
        + [pltpu.VMEM((npair, seqlen, LANE), F32)]
        + [pltpu.VMEM((npair, seqlen, LANE), BF16)] * 4
        + [pltpu.VMEM((npair, seqlen // GDN_CHUNK * 8, LANE), F32)]
        + [pltpu.VMEM((npair, LANE, LANE), F32)],
        compiler_params=_cparams("parallel"),
        name="gdn",
    )(q, k, v, z, ab, conv_w, al, dtb, nw)


def _route(logits, n_exp):
    lane = _iota(logits.shape, 1)
    lg = jnp.where(lane < n_exp, logits, -jnp.inf)
    m1 = jnp.max(lg, axis=-1, keepdims=True)
    i1 = jnp.min(jnp.where(lg == m1, lane, LANE), axis=-1, keepdims=True)
    lg2 = jnp.where(lane == i1, -jnp.inf, lg)
    m2 = jnp.max(lg2, axis=-1, keepdims=True)
    i2 = jnp.min(jnp.where(lg2 == m2, lane, LANE), axis=-1, keepdims=True)
    e2 = jnp.exp(m2 - m1)
    g1 = 1.0 / (1.0 + e2)
    g2 = e2 / (1.0 + e2)
    gates = jnp.where(lane == i1, g1, 0.0) + jnp.where(lane == i2, g2, 0.0)
    mask = jnp.where((lane == i1 + n_exp) | (lane == i2 + n_exp), 1.0, 0.0)
    return gates + mask


def _outproj_kernel(x_ref, ys_ref, ym_ref, yg_ref, w_ref, g_ref, b_ref, *rest, alpha, s5w, mw, n_exp):
    mix = _dot(ys_ref[...].astype(BF16), w_ref[0:s5w, :])
    mix = mix + _dot(ym_ref[0].astype(BF16), w_ref[s5w:s5w + mw, :])
    mix = mix + _dot(yg_ref[0].astype(BF16), w_ref[s5w + mw:, :])
    x1 = _layer_norm(alpha * x_ref[...] + mix, g_ref[...], b_ref[...])
    if n_exp:
        rw_ref, o_ref, r_ref = rest
        o_ref[...] = x1
        r_ref[...] = _route(_dot(x1, rw_ref[...], precision=HI), n_exp)
    else:
        (o_ref,) = rest
        o_ref[...] = x1


def _outproj(x2, ys, ym, yg, w, g, b, router_w, *, bsz, seqlen, alpha, n_exp):
    tl = 512
    nt = seqlen // tl
    t, d = x2.shape
    s5w = ys.shape[1] // bsz
    mw = ym.shape[2]
    gw = yg.shape[2]
    const = lambda bi, j: (0, 0)
    row_spec = pl.BlockSpec((tl, d), lambda bi, j: (bi * nt + j, 0))
    in_specs = [row_spec,
                pl.BlockSpec((tl, s5w), lambda bi, j: (j, bi)),
                pl.BlockSpec((1, tl, mw), lambda bi, j: (bi, j, 0)),
                pl.BlockSpec((1, tl, gw), lambda bi, j: (bi, j, 0)),
                pl.BlockSpec(w.shape, const), pl.BlockSpec(g.shape, const), pl.BlockSpec(b.shape, const)]
    args = [x2, ys, ym, yg, w, g, b]
    out_shape = jax.ShapeDtypeStruct((t, d), F32)
    out_specs = row_spec
    if n_exp:
        in_specs.append(pl.BlockSpec(router_w.shape, const))
        args.append(router_w)
        out_shape = (out_shape, jax.ShapeDtypeStruct((t, LANE), F32))
        out_specs = (row_spec, pl.BlockSpec((tl, LANE), lambda bi, j: (bi * nt + j, 0)))
    return pl.pallas_call(
        functools.partial(_outproj_kernel, alpha=alpha, s5w=s5w, mw=mw, n_exp=n_exp),
        out_shape=out_shape,
        grid=(bsz, nt),
        in_specs=in_specs,
        out_specs=out_specs,
        compiler_params=_cparams("parallel", "arbitrary"),
        name="outproj_route" if n_exp else "outproj",
    )(*args)


def _ffn_body(x_ref, wg_ref, wu_ref, wd_ref, xb_ref, acc_ref):
    f = pl.program_id(1)

    @pl.when(f == 0)
    def _():
        xb_ref[...] = x_ref[...].astype(BF16)
        acc_ref[...] = jnp.zeros_like(acc_ref)

    xb = xb_ref[...]
    hg = _dot(xb, wg_ref[0])
    hu = _dot(xb, wu_ref[0])
    h = (hg * jax.nn.sigmoid(hg)) * hu
    acc_ref[...] += _dot(h.astype(BF16), wd_ref[0])


def _ffn_dense_kernel(x_ref, wg_ref, wu_ref, wd_ref, g_ref, b_ref, o_ref, xb_ref, acc_ref, *, alpha):
    _ffn_body(x_ref, wg_ref, wu_ref, wd_ref, xb_ref, acc_ref)

    @pl.when(pl.program_id(1) == pl.num_programs(1) - 1)
    def _():
        o_ref[...] = _layer_norm(alpha * x_ref[...] + acc_ref[...], g_ref[...], b_ref[...])


def _ffn_dense(x2, wg, wu, wd, g, b, *, alpha):
    t, d = x2.shape
    ff = wg.shape[2]
    tm, tf = FFN_ROWS, FFN_COLS
    const = lambda i, f: (0, 0)
    return pl.pallas_call(
        functools.partial(_ffn_dense_kernel, alpha=alpha),
        out_shape=jax.ShapeDtypeStruct((t, d), F32),
        grid=(t // tm, ff // tf),
        in_specs=[pl.BlockSpec((tm, d), lambda i, f: (i, 0)),
                  pl.BlockSpec((1, d, tf), lambda i, f: (0, 0, f)),
                  pl.BlockSpec((1, d, tf), lambda i, f: (0, 0, f)),
                  pl.BlockSpec((1, tf, d), lambda i, f: (0, f, 0)),
                  pl.BlockSpec(g.shape, const), pl.BlockSpec(b.shape, const)],
        out_specs=pl.BlockSpec((tm, d), lambda i, f: (i, 0)),
        scratch_shapes=[pltpu.VMEM((tm, d), BF16), pltpu.VMEM((tm, d), F32)],
        compiler_params=_cparams("parallel", "arbitrary"),
        name="ffn_dense",
    )(x2, wg, wu, wd, g, b)


def _ffn_moe_kernel(te_ref, tv_ref, x_ref, wg_ref, wu_ref, wd_ref, o_ref, xb_ref, acc_ref):
    i = pl.program_id(0)

    @pl.when(tv_ref[i] > 0)
    def _():
        _ffn_body(x_ref, wg_ref, wu_ref, wd_ref, xb_ref, acc_ref)

        @pl.when(pl.program_id(1) == pl.num_programs(1) - 1)
        def _():
            o_ref[...] = acc_ref[...]

    @pl.when((tv_ref[i] == 0) & (pl.program_id(1) == pl.num_programs(1) - 1))
    def _():
        o_ref[...] = jnp.zeros_like(o_ref)


def _ffn_moe(tile_expert, tile_valid, xs, wg, wu, wd):
    p_rows, d = xs.shape
    ff = wg.shape[2]
    tm, tf = MOE_ROWS, FFN_COLS
    nf = ff // tf

    def fsel(i, f, tv):
        return jnp.where(tv[i] > 0, f, nf - 1)

    grid_spec = pltpu.PrefetchScalarGridSpec(
        num_scalar_prefetch=2,
        grid=(p_rows // tm, nf),
        in_specs=[pl.BlockSpec((tm, d), lambda i, f, te, tv: (i, 0)),
                  pl.BlockSpec((1, d, tf), lambda i, f, te, tv: (te[i], 0, fsel(i, f, tv))),
                  pl.BlockSpec((1, d, tf), lambda i, f, te, tv: (te[i], 0, fsel(i, f, tv))),
                  pl.BlockSpec((1, tf, d), lambda i, f, te, tv: (te[i], fsel(i, f, tv), 0))],
        out_specs=pl.BlockSpec((tm, d), lambda i, f, te, tv: (i, 0)),
        scratch_shapes=[pltpu.VMEM((tm, d), BF16), pltpu.VMEM((tm, d), F32)],
    )
    return pl.pallas_call(
        _ffn_moe_kernel,
        out_shape=jax.ShapeDtypeStruct((p_rows, d), F32),
        grid_spec=grid_spec,
        compiler_params=_cparams("arbitrary", "arbitrary"),
        name="ffn_moe",
    )(tile_expert, tile_valid, xs, wg, wu, wd)


def _row_copy(src_hbm, idx, dst_ref, row, sem):
    return pltpu.make_async_copy(src_hbm.at[pl.ds(idx, 1)], dst_ref.at[pl.ds(row, 1)], sem)


def _gather_kernel(idx_ref, x_hbm, o_ref, sem, *, rows):
    def start(r, carry):
        _row_copy(x_hbm, idx_ref[0, 0, r], o_ref, r, sem).start()
        return carry

    lax.fori_loop(0, rows, start, 0, unroll=8)

    def wait(r, carry):
        _row_copy(x_hbm, 0, o_ref, r, sem).wait()
        return carry

    lax.fori_loop(0, rows, wait, 0, unroll=8)


def _gather_rows(idx, x2):
    nt, _, rows = idx.shape
    d = x2.shape[1]
    return pl.pallas_call(
        functools.partial(_gather_kernel, rows=rows),
        out_shape=jax.ShapeDtypeStruct((nt * rows, d), x2.dtype),
        grid=(nt,),
        in_specs=[pl.BlockSpec((1, 1, rows), lambda i: (i, 0, 0), memory_space=pltpu.SMEM),
                  pl.BlockSpec(memory_space=pl.ANY)],
        out_specs=pl.BlockSpec((rows, d), lambda i: (i, 0)),
        scratch_shapes=[pltpu.SemaphoreType.DMA(())],
        compiler_params=_cparams("arbitrary"),
        name="moe_gather",
    )(idx, x2)


def _combine_kernel(p0_ref, p1_ref, ys_hbm, x_ref, gt_ref, g_ref, b_ref, o_ref, y0_ref, y1_ref, sem, *, rows, alpha):
    def start(r, carry):
        _row_copy(ys_hbm, p0_ref[0, 0, r], y0_ref, r, sem.at[0]).start()
        _row_copy(ys_hbm, p1_ref[0, 0, r], y1_ref, r, sem.at[1]).start()
        return carry

    lax.fori_loop(0, rows, start, 0, unroll=8)

    def wait(r, carry):
        _row_copy(ys_hbm, 0, y0_ref, r, sem.at[0]).wait()
        _row_copy(ys_hbm, 0, y1_ref, r, sem.at[1]).wait()
        return carry

    lax.fori_loop(0, rows, wait, 0, unroll=8)
    gt = gt_ref[...]
    f = gt[:, 0:1] * y0_ref[...] + gt[:, 1:2] * y1_ref[...]
    o_ref[...] = _layer_norm(alpha * x_ref[...] + f, g_ref[...], b_ref[...])


def _combine(p0, p1, ys, x2, gates, g, b, *, alpha):
    nt, _, rows = p0.shape
    t, d = x2.shape
    const = lambda i: (0, 0)
    idx_spec = pl.BlockSpec((1, 1, rows), lambda i: (i, 0, 0), memory_space=pltpu.SMEM)
    return pl.pallas_call(
        functools.partial(_combine_kernel, rows=rows, alpha=alpha),
        out_shape=jax.ShapeDtypeStruct((t, d), F32),
        grid=(nt,),
        in_specs=[idx_spec, idx_spec,
                  pl.BlockSpec(memory_space=pl.ANY),
                  pl.BlockSpec((rows, d), lambda i: (i, 0)),
                  pl.BlockSpec((rows, LANE), lambda i: (i, 0)),
                  pl.BlockSpec(g.shape, const), pl.BlockSpec(b.shape, const)],
        out_specs=pl.BlockSpec((rows, d), lambda i: (i, 0)),
        scratch_shapes=[pltpu.VMEM((rows, d), F32), pltpu.VMEM((rows, d), F32), pltpu.SemaphoreType.DMA((2,))],
        compiler_params=_cparams("arbitrary"),
        name="moe_combine",
    )(p0, p1, ys, x2, gates, g, b)


def _dispatch_plan(route, n_exp):
    t = route.shape[0]
    tm = MOE_ROWS
    gates = route[:, :n_exp]
    sel = route[:, n_exp:2 * n_exp] > 0.5
    seli = sel.astype(I32)
    rank = jnp.cumsum(seli, axis=0) - seli
    counts = jnp.sum(seli, axis=0)
    padded = ((counts + tm - 1) // tm) * tm
    ends = jnp.cumsum(padded)
    starts = ends - padded
    pos = starts[None, :] + rank
    p_rows = t * TOP_K + n_exp * tm
    n_tiles = p_rows // tm
    tok = jnp.broadcast_to(jnp.arange(t, dtype=I32)[:, None], (t, n_exp))
    src = jnp.zeros((p_rows,), I32).at[jnp.where(sel, pos, p_rows)].set(tok, mode="drop")
    tile_start = jnp.arange(n_tiles, dtype=I32) * tm
    tile_expert = jnp.minimum(jnp.sum((tile_start[:, None] >= ends[None, :]).astype(I32), axis=1), n_exp - 1)
    tile_valid = (tile_start < starts[tile_expert] + counts[tile_expert]).astype(I32)
    order = jnp.cumsum(seli, axis=1) - seli
    first = sel & (order == 0)
    second = sel & (order == 1)
    pick = lambda m, v: jnp.sum(jnp.where(m, v, 0), axis=1)
    p0, p1 = pick(first, pos), pick(second, pos)
    g01 = jnp.stack([pick(first, gates), pick(second, gates)], axis=1)
    g01 = jnp.pad(g01, ((0, 0), (0, LANE - TOP_K)))
    shape3 = lambda v, rows: v.astype(I32).reshape(-1, 1, rows)
    return (shape3(src, GATHER_ROWS), tile_expert.astype(I32), tile_valid,
            shape3(p0, GATHER_ROWS), shape3(p1, GATHER_ROWS), g01)


def kernel(x, w_in, w_out, s5_lam_re, s5_lam_im, s5_log_dt, s5_b_re, s5_b_im, s5_c_re, s5_c_im, s5_d, s5_glu_w,
           s5_glu_b, gdn_conv_w, gdn_a_log, gdn_dt_bias, gdn_norm_w, ln1_g, ln1_b, ln2_g, ln2_b, ffn_w_gate,
           ffn_w_up, ffn_w_down, moe_router, moe_w_gate, moe_w_up, moe_w_down):
    bsz, seqlen, d = x.shape
    depth = w_in.shape[0]
    s5w = s5_glu_w.shape[1]
    gw = gdn_a_log.shape[1] * HEAD_DIM
    mw = w_out.shape[1] - s5w - gw
    n_exp = moe_router.shape[2]
    alpha = (2 * depth) ** 0.25
    row = lambda v: v.astype(F32).reshape(1, -1)

    x2 = x.reshape(bsz * seqlen, d)
    for l in range(depth):
        w, wt = _pack_inproj(w_in[l], s5w, mw, gw)
        u, mqt, mk, mvt, gq, gk, gv, gz, ab = _inproj(x2, w, wt, bsz=bsz, seqlen=seqlen, s5w=s5w, mw=mw, gw=gw)
        bblk, cblk, lam, dskip = _s5_params(s5_lam_re[l], s5_lam_im[l], s5_log_dt[l], s5_b_re[l], s5_b_im[l],
                                            s5_c_re[l], s5_c_im[l], s5_d[l])
        y_s5 = _s5(u.reshape(seqlen * bsz, s5w), bblk, cblk, lam, dskip, s5_glu_w[l].astype(BF16),
                   row(s5_glu_b[l]), bsz=bsz, seqlen=seqlen).reshape(seqlen, bsz * s5w)
        y_moba = _moba(mqt, mk, mvt, seqlen=seqlen)
        y_gdn = _gdn(gq, gk, gv, gz, ab, *_gdn_params(gdn_conv_w[l], gdn_a_log[l], gdn_dt_bias[l], gdn_norm_w[l]))
        routed = l % 2 == 1
        router_w = jnp.pad(moe_router[l // 2].astype(F32), ((0, 0), (0, LANE - n_exp))) if routed else None
        res = _outproj(x2, y_s5, y_moba, y_gdn, w_out[l].astype(BF16), row(ln1_g[l]), row(ln1_b[l]), router_w,
                       bsz=bsz, seqlen=seqlen, alpha=alpha, n_exp=n_exp if routed else 0)
        if routed:
            x1, route = res
            src, tile_expert, tile_valid, p0, p1, g01 = _dispatch_plan(route, n_exp)
            xs = _gather_rows(src, x1)
            ys = _ffn_moe(tile_expert, tile_valid, xs, moe_w_gate[l // 2].astype(BF16),
                          moe_w_up[l // 2].astype(BF16), moe_w_down[l // 2].astype(BF16))
            x2 = _combine(p0, p1, ys, x1, g01, row(ln2_g[l]), row(ln2_b[l]), alpha=alpha)
        else:
            x2 = _ffn_dense(res, ffn_w_gate[l // 2:l // 2 + 1].astype(BF16), ffn_w_up[l // 2:l // 2 + 1].astype(BF16),
                            ffn_w_down[l // 2:l // 2 + 1].astype(BF16), row(ln2_g[l]), row(ln2_b[l]), alpha=alpha)
    return x2.reshape(bsz, seqlen, d)
```

```python
import functools

import jax
import jax.numpy as jnp
from jax import lax
from jax.experimental import pallas as pl
from jax.experimental.pallas import tpu as pltpu

F32 = jnp.float32
BF16 = jnp.bfloat16
I32 = jnp.int32
HI = lax.Precision.HIGHEST

HEAD_DIM = 64
MOBA_BLOCK = 256
MOBA_TOPK = 3
GDN_CHUNK = 64
GDN_PREP_CHUNKS = 2
TOP_K = 2
LN_EPS = 1e-5
RMS_EPS = 1e-6
NEG_BIG = -1e30

V7X_VMEM_LIMIT_BYTES = 56 * 1024 * 1024
LANE = 128
TOKEN_TILE_ROWS = 8

FFN_ROWS = 1024
MOE_ROWS = 512
FFN_COLS = 512
GATHER_ROWS = 256


def _cparams(*sem):
    return pltpu.CompilerParams(dimension_semantics=sem, vmem_limit_bytes=V7X_VMEM_LIMIT_BYTES)


def _dot(a, b, precision=None):
    return jnp.dot(a, b, preferred_element_type=F32, precision=precision)


def _dot_nt(a, b, precision=None):
    return lax.dot_general(a, b, (((1,), (1,)), ((), ())), preferred_element_type=F32, precision=precision)


def _dot_tn(a, b, precision=None):
    return lax.dot_general(a, b, (((0,), (0,)), ((), ())), preferred_element_type=F32, precision=precision)


def _iota(shape, dim):
    return lax.broadcasted_iota(I32, shape, dim)


def _layer_norm(r, g, b):
    mu = jnp.mean(r, axis=-1, keepdims=True)
    c = r - mu
    var = jnp.mean(c * c, axis=-1, keepdims=True)
    return c * lax.rsqrt(var + LN_EPS) * g + b


def _inproj_kernel(x_ref, w_ref, wt_ref, u_ref, mqt_ref, mk_ref, mvt_ref, gq_ref, gk_ref, gv_ref, gz_ref,
                   ab_ref, *, s5w, mw, gw):
    xb = x_ref[...].astype(BF16)
    nh_m = mw // HEAD_DIM

    def cols(off, width):
        return _dot(xb, w_ref[:, off:off + width])

    u_ref[...] = cols(0, s5w)
    off = s5w + mw
    kk = cols(off, mw)
    for h in range(nh_m):
        mk_ref[0, h, 0] = kk[:, h * HEAD_DIM:(h + 1) * HEAD_DIM]
    off += 2 * mw
    for ref in (gq_ref, gk_ref, gv_ref, gz_ref):
        ref[0] = cols(off, gw)
        off += gw
    ab_ref[0] = cols(off, LANE)
    qt = _dot_nt(wt_ref[0:mw, :], xb)
    vt = _dot_nt(wt_ref[mw:2 * mw, :], xb)
    for h in range(nh_m):
        mqt_ref[0, h, 0] = qt[h * HEAD_DIM:(h + 1) * HEAD_DIM, :]
        mvt_ref[0, h, 0] = vt[h * HEAD_DIM:(h + 1) * HEAD_DIM, :]


def _pack_inproj(w_in, s5w, mw, gw):
    c = w_in.shape[1]
    main = s5w + 3 * mw + 4 * gw
    w = jnp.pad(w_in, ((0, 0), (0, main + LANE - c))).astype(BF16)
    q0, v0 = s5w, s5w + 2 * mw
    wt = jnp.concatenate([w_in[:, q0:q0 + mw], w_in[:, v0:v0 + mw]], axis=1).T.astype(BF16)
    return w, wt


def _inproj(x2, w, wt, *, bsz, seqlen, s5w, mw, gw):
    tl = MOBA_BLOCK
    nb = seqlen // tl
    d = x2.shape[1]
    nh_m = mw // HEAD_DIM
    hd = HEAD_DIM
    gdn_sds = jax.ShapeDtypeStruct((bsz, seqlen, gw), F32)
    out_shape = (
        jax.ShapeDtypeStruct((seqlen, bsz * s5w), F32),
        jax.ShapeDtypeStruct((bsz, nh_m, nb, hd, tl), F32),
        jax.ShapeDtypeStruct((bsz, nh_m, nb, tl, hd), F32),
        jax.ShapeDtypeStruct((bsz, nh_m, nb, hd, tl), F32),
        gdn_sds, gdn_sds, gdn_sds, gdn_sds,
        jax.ShapeDtypeStruct((bsz, seqlen, LANE), F32),
    )
    t5 = lambda b, j: (b, 0, j, 0, 0)
    t3 = lambda b, j: (b, j, 0)
    out_specs = (
        pl.BlockSpec((tl, s5w), lambda b, j: (j, b)),
        pl.BlockSpec((1, nh_m, 1, hd, tl), t5),
        pl.BlockSpec((1, nh_m, 1, tl, hd), t5),
        pl.BlockSpec((1, nh_m, 1, hd, tl), t5),
        pl.BlockSpec((1, tl, gw), t3),
        pl.BlockSpec((1, tl, gw), t3),
        pl.BlockSpec((1, tl, gw), t3),
        pl.BlockSpec((1, tl, gw), t3),
        pl.BlockSpec((1, tl, LANE), t3),
    )
    return pl.pallas_call(
        functools.partial(_inproj_kernel, s5w=s5w, mw=mw, gw=gw),
        out_shape=out_shape,
        grid=(bsz, nb),
        in_specs=[
            pl.BlockSpec((tl, d), lambda b, j: (b * nb + j, 0)),
            pl.BlockSpec(w.shape, lambda b, j: (0, 0)),
            pl.BlockSpec(wt.shape, lambda b, j: (0, 0)),
        ],
        out_specs=out_specs,
        compiler_params=_cparams("parallel", "arbitrary"),
        name="inproj",
    )(x2, w, wt)


def _s5_kernel(u_ref, bblk_ref, cblk_ref, lam_ref, d_ref, gw_ref, gb_ref, y_ref, st_ref, h_ref, *, bsz, tt, ns):
    @pl.when(pl.program_id(0) == 0)
    def _():
        h_ref[...] = jnp.zeros_like(h_ref)

    u = u_ref[...]
    st_ref[...] = _dot(u.astype(BF16), bblk_ref[...])
    lam_re = jnp.broadcast_to(lam_ref[0:1, :], (bsz, ns))
    lam_im = jnp.broadcast_to(lam_ref[1:2, :], (bsz, ns))

    def step(t, carry):
        h_re, h_im = carry
        r = pl.multiple_of(t * bsz, bsz)
        n_re = lam_re * h_re - lam_im * h_im + st_ref[pl.ds(r, bsz), 0:ns]
        n_im = lam_re * h_im + lam_im * h_re + st_ref[pl.ds(r, bsz), ns:2 * ns]
        st_ref[pl.ds(r, bsz), 0:ns] = n_re
        st_ref[pl.ds(r, bsz), ns:2 * ns] = n_im
        return n_re, n_im

    h_re, h_im = lax.fori_loop(0, tt, step, (h_ref[0], h_ref[1]), unroll=8)
    h_ref[0] = h_re
    h_ref[1] = h_im
    y = _dot(st_ref[...].astype(BF16), cblk_ref[...]) + d_ref[...] * u
    y = jax.nn.gelu(y)
    y_ref[...] = y * jax.nn.sigmoid(_dot(y.astype(BF16), gw_ref[...]) + gb_ref[...])


def _s5(u2, bblk, cblk, lam, d, glu_w, glu_b, *, bsz, seqlen):
    width = u2.shape[1]
    ns = lam.shape[1]
    tt = 64
    rows = tt * bsz
    const = lambda i: (0, 0)
    return pl.pallas_call(
        functools.partial(_s5_kernel, bsz=bsz, tt=tt, ns=ns),
        out_shape=jax.ShapeDtypeStruct(u2.shape, F32),
        grid=(seqlen // tt,),
        in_specs=[
            pl.BlockSpec((rows, width), lambda i: (i, 0)),
            pl.BlockSpec(bblk.shape, const),
            pl.BlockSpec(cblk.shape, const),
            pl.BlockSpec(lam.shape, const),
            pl.BlockSpec(d.shape, const),
            pl.BlockSpec(glu_w.shape, const),
            pl.BlockSpec(glu_b.shape, const),
        ],
        out_specs=pl.BlockSpec((rows, width), lambda i: (i, 0)),
        scratch_shapes=[pltpu.VMEM((rows, 2 * ns), F32), pltpu.VMEM((2, bsz, ns), F32)],
        compiler_params=_cparams("arbitrary"),
        name="s5",
    )(u2, bblk, cblk, lam, d, glu_w, glu_b)


def _s5_params(lam_re, lam_im, log_dt, b_re, b_im, c_re, c_im, d_skip):
    g, n = lam_re.shape
    p = b_re.shape[-1]
    lam = lax.complex(lam_re.astype(F32), lam_im.astype(F32))
    step = jnp.exp(log_dt.astype(F32))[:, None]
    lam_bar = jnp.exp(lam * step)
    b_bar = ((lam_bar - 1.0) / lam)[..., None] * lax.complex(b_re.astype(F32), b_im.astype(F32))
    eye = jnp.eye(g, dtype=F32)
    b_re_blk = jnp.einsum('gnp,gh->gphn', b_bar.real, eye).reshape(g * p, g * n)
    b_im_blk = jnp.einsum('gnp,gh->gphn', b_bar.imag, eye).reshape(g * p, g * n)
    bblk = jnp.concatenate([b_re_blk, b_im_blk], axis=1)
    c_re_blk = jnp.einsum('gpn,gh->gnhp', c_re.astype(F32), eye).reshape(g * n, g * p)
    c_im_blk = jnp.einsum('gpn,gh->gnhp', c_im.astype(F32), eye).reshape(g * n, g * p)
    cblk = jnp.concatenate([c_re_blk, -c_im_blk], axis=0)
    lam2 = jnp.stack([lam_bar.real.reshape(g * n), lam_bar.imag.reshape(g * n)])
    return bblk.astype(BF16), cblk.astype(BF16), lam2, d_skip.astype(F32).reshape(1, g * p)


def _moba_kernel(qt_ref, k_ref, vt_ref, o_ref, km_ref, qs_ref, bias_ref, acc_ref, *, nh, nb):
    i = pl.program_id(1)
    blk = MOBA_BLOCK
    hd = HEAD_DIM

    @pl.when(i == 0)
    def _():
        avg = jnp.full((1, blk), 1.0 / blk, F32)
        for h in range(nh):
            for n in range(nb):
                km_ref[h, n:n + 1, :] = _dot(avg, k_ref[0, h, n], precision=HI)

    n_iota = _iota((nb, blk), 0)
    kpos = _iota((blk, blk), 0)
    qpos = _iota((blk, blk), 1)
    causal = kpos <= qpos

    heads = range(nh)

    qts = [qt_ref[0, h, 0] * (hd ** -0.5) for h in heads]
    qtb = [qts[h].astype(BF16) for h in heads]
    s_own = [_dot(k_ref[0, h, i].astype(BF16), qtb[h]) for h in heads]
    gates = [_dot(km_ref[h], qts[h], precision=HI) for h in heads]
    vt_own = [vt_ref[0, h, i].astype(BF16) for h in heads]
    biases = []
    for h in heads:
        gate = jnp.where(n_iota < i, gates[h], -jnp.inf)
        rank = jnp.zeros((nb, blk), I32)
        for m in range(nb):
            gm = gate[m:m + 1, :]
            ahead = (gm > gate) | ((gm == gate) & (m < n_iota))
            rank = rank + ahead.astype(I32)
        biases.append(jnp.where((n_iota < i) & (rank < MOBA_TOPK), 0.0, NEG_BIG))
    m_own, l_own, acc_own = [], [], []
    for h in heads:
        s = jnp.where(causal, s_own[h], NEG_BIG)
        m0 = jnp.max(s, axis=0, keepdims=True)
        p = jnp.exp(s - m0)
        m_own.append(m0)
        l_own.append(jnp.sum(p, axis=0, keepdims=True))
        acc_own.append(_dot(vt_own[h], p.astype(BF16)))
    for h in heads:
        bias_ref[h] = biases[h]
        qs_ref[h] = qtb[h]
        acc_ref[h * hd:(h + 1) * hd, :] = acc_own[h]

    def body(j, carry):
        ms, ls = carry
        ss = [_dot(k_ref[0, h, j].astype(BF16), qs_ref[h]) + bias_ref[h, pl.ds(j, 1), :] for h in heads]
        vts = [vt_ref[0, h, j].astype(BF16) for h in heads]
        accs = [acc_ref[h * hd:(h + 1) * hd, :] for h in heads]
        new_ms, new_ls, new_accs = [], [], []
        for h in heads:
            m_new = jnp.maximum(ms[h], jnp.max(ss[h], axis=0, keepdims=True))
            p = jnp.exp(ss[h] - m_new)
            corr = jnp.exp(ms[h] - m_new)
            new_ms.append(m_new)
            new_ls.append(ls[h] * corr + jnp.sum(p, axis=0, keepdims=True))
            new_accs.append(accs[h] * corr + _dot(vts[h], p.astype(BF16)))
        for h in heads:
            acc_ref[h * hd:(h + 1) * hd, :] = new_accs[h]
        return tuple(new_ms), tuple(new_ls)

    _, ls = lax.fori_loop(0, i, body, (tuple(m_own), tuple(l_own)))
    for h in heads:
        rows = slice(h * hd, (h + 1) * hd)
        acc_ref[rows, :] = acc_ref[rows, :] / ls[h]
    eye = (kpos == qpos).astype(F32)
    o_ref[0] = _dot_nt(eye, acc_ref[...], precision=HI)


def _moba(qt, k, vt, *, seqlen):
    bsz, nh, nb, hd, blk = qt.shape
    return pl.pallas_call(
        functools.partial(_moba_kernel, nh=nh, nb=nb),
        out_shape=jax.ShapeDtypeStruct((bsz, seqlen, nh * hd), F32),
        grid=(bsz, nb),
        in_specs=[
            pl.BlockSpec((1, nh, 1, hd, blk), lambda b, i: (b, 0, i, 0, 0)),
            pl.BlockSpec((1, nh, nb, blk, hd), lambda b, i: (b, 0, 0, 0, 0)),
            pl.BlockSpec((1, nh, nb, hd, blk), lambda b, i: (b, 0, 0, 0, 0)),
        ],
        out_specs=pl.BlockSpec((1, blk, nh * hd), lambda b, i: (b, i, 0)),
        scratch_shapes=[pltpu.VMEM((nh, nb, hd), F32), pltpu.VMEM((nh, hd, blk), BF16),
                        pltpu.VMEM((nh, nb, blk), F32), pltpu.VMEM((nh * hd, blk), F32)],
        compiler_params=_cparams("parallel", "arbitrary"),
        name="moba",
    )(qt, k, vt)


def _gdn_kernel(q_ref, k_ref, v_ref, z_ref, ab_ref, cw_ref, al_ref, dtb_ref, nw_ref, o_ref,
                g_ref, beta_ref, us_ref, ws_ref, qe_ref, kd_ref, at_ref, el_ref, s_ref, *, nh, seqlen):
    npair = nh // 2
    ch = GDN_CHUNK
    hd = HEAD_DIM
    gw = nh * hd
    pw = 2 * hd
    pairs = range(npair)

    rr = _iota((pw, pw), 0)
    cc = _iota((pw, pw), 1)
    same_head = (rr // hd) == (cc // hd)
    mask_incl = same_head & ((rr % ch) >= (cc % ch))
    mask_strict = same_head & ((rr % ch) > (cc % ch))
    eye = (rr == cc).astype(F32)
    tril_c = (_iota((ch, ch), 0) >= _iota((ch, ch), 1)).astype(F32)
    lane_c = _iota((ch, LANE), 1)
    first = lane_c < hd
    row_col = _iota((pw, 1), 0)
    lane_row = _iota((1, pw), 1)

    def stack(x):
        return jnp.concatenate([jnp.where(first, x, 0.0), jnp.where(first, 0.0, x)], axis=0)

    def fold(x):
        return x[:ch] + x[ch:]

    def head_sumsq(y):
        y2 = y * y
        sa = jnp.sum(jnp.where(first, y2, 0.0), axis=1, keepdims=True)
        sb = jnp.sum(jnp.where(first, 0.0, y2), axis=1, keepdims=True)
        return jnp.where(first, sa, sb)

    a = ab_ref[0]
    ea = a + dtb_ref[...]
    g_ref[...] = -jnp.exp(al_ref[...]) * (jnp.maximum(ea, 0.0) + jnp.log1p(jnp.exp(-jnp.abs(ea))))
    beta_ref[...] = jax.nn.sigmoid(a)

    def conv_silu(src, t, p, c, r0):
        lanes = slice(p * pw, (p + 1) * pw)
        h0 = pl.multiple_of(jnp.maximum(r0 - 8, 0), 8)
        hist = jnp.where(c > 0, src[0, pl.ds(h0, 8), lanes], 0.0)
        x = jnp.concatenate([hist, src[0, pl.ds(r0, ch), lanes]], axis=0)
        c0 = t * gw + p * pw
        acc = x * cw_ref[3:4, c0:c0 + pw]
        for tap in range(3):
            acc = acc + pltpu.roll(x, 3 - tap, 0) * cw_ref[tap:tap + 1, c0:c0 + pw]
        y = acc[8:]
        return y * jax.nn.sigmoid(y)

    def prep(cc, carry):
        chunks = [cc * GDN_PREP_CHUNKS + u for u in range(GDN_PREP_CHUNKS)]
        r0s = [pl.multiple_of(c * ch, ch) for c in chunks]
        gcs = [_dot(tril_c, g_ref[pl.ds(r0, ch), :], precision=HI) for r0 in r0s]
        bch = [beta_ref[pl.ds(r0, ch), :] for r0 in r0s]
        streams = [(u, p) for u in range(GDN_PREP_CHUNKS) for p in pairs]
        ns = range(len(streams))
        qs, ks, vs, gcol, bcol, decay = [], [], [], [], [], []
        for u, p in streams:
            q = conv_silu(q_ref, 0, p, chunks[u], r0s[u])
            k = conv_silu(k_ref, 1, p, chunks[u], r0s[u])
            v = conv_silu(v_ref, 2, p, chunks[u], r0s[u])
            q = q * (lax.rsqrt(head_sumsq(q) + RMS_EPS) * (hd ** -0.5))
            k = k * lax.rsqrt(head_sumsq(k) + RMS_EPS)
            qs.append(stack(q))
            ks.append(stack(k))
            vs.append(stack(v))
            ha, hb = 2 * p, 2 * p + 1
            gst = jnp.concatenate([jnp.where(lane_c == ha, gcs[u], 0.0), jnp.where(lane_c == hb, gcs[u], 0.0)], axis=0)
            bst = jnp.concatenate([jnp.where(lane_c == nh + ha, bch[u], 0.0),
                                   jnp.where(lane_c == nh + hb, bch[u], 0.0)], axis=0)
            gc = jnp.sum(gst, axis=1, keepdims=True)
            gcol.append(gc)
            bcol.append(jnp.sum(bst, axis=1, keepdims=True))
            gmat = jnp.broadcast_to(gc, (pw, pw))
            decay.append(jnp.where(mask_incl, jnp.exp(jnp.where(mask_incl, gmat - gmat.T, 0.0)), 0.0))
        kbs = [ks[n] * bcol[n] for n in ns]
        ksb = [ks[n].astype(BF16) for n in ns]
        pm = [jnp.where(mask_strict, _dot_nt(kbs[n].astype(BF16), ksb[n]) * decay[n], 0.0) for n in ns]
        tm = [eye - pm[n] for n in ns]
        pmb = [pm[n].astype(BF16) for n in ns]
        pm = [_dot(pmb[n], pmb[n]) for n in ns]
        for _ in range(4):
            pmb = [pm[n].astype(BF16) for n in ns]
            tm = [tm[n] + _dot(tm[n].astype(BF16), pmb[n]) for n in ns]
            pm = [_dot(pmb[n], pmb[n]) for n in ns]
        tm = [tm[n] + _dot(tm[n].astype(BF16), pm[n].astype(BF16)) for n in ns]
        out = []
        for n in ns:
            eg = jnp.exp(gcol[n])
            rhs = jnp.concatenate([vs[n] * bcol[n], kbs[n] * eg], axis=1).astype(BF16)
            sol = _dot(tm[n].astype(BF16), rhs)
            attn = jnp.where(mask_incl, _dot_nt(qs[n].astype(BF16), ksb[n]) * decay[n], 0.0)
            g_a = gcol[n][ch - 1:ch]
            g_b = gcol[n][2 * ch - 1:2 * ch]
            glast = jnp.where(row_col < ch, g_a, g_b)
            e_last = jnp.where(lane_row < hd, jnp.exp(g_a), jnp.exp(g_b))
            out.append((fold(sol[:, :pw]), fold(sol[:, pw:]).astype(BF16), fold(qs[n] * eg).astype(BF16),
                        fold(ks[n] * jnp.exp(glast - gcol[n])).astype(BF16), fold(attn).astype(BF16),
                        jnp.broadcast_to(e_last, (8, pw))))
        for n, (u, p) in enumerate(streams):
            rows = pl.ds(r0s[u], ch)
            us_ref[p, rows, :], ws_ref[p, rows, :], qe_ref[p, rows, :], kd_ref[p, rows, :], at_ref[p, rows, :] = out[n][:5]
            el_ref[p, pl.ds(pl.multiple_of(chunks[u] * 8, 8), 8), :] = out[n][5]
        return carry

    lax.fori_loop(0, seqlen // (ch * GDN_PREP_CHUNKS), prep, 0)

    s_ref[...] = jnp.zeros_like(s_ref)

    def scan(c, carry):
        r0 = pl.multiple_of(c * ch, ch)
        rows = pl.ds(r0, ch)
        for p in pairs:
            lanes = slice(p * pw, (p + 1) * pw)
            state = s_ref[p]
            wq = jnp.concatenate([ws_ref[p, rows, :], qe_ref[p, rows, :]], axis=0)
            wqs = _dot(wq, state.astype(BF16))
            v_new = us_ref[p, rows, :] - wqs[:ch]
            o = wqs[ch:] + _dot(at_ref[p, rows, :], stack(v_new).astype(BF16))
            upd = _dot_tn(kd_ref[p, rows, :], v_new.astype(BF16))
            s_ref[p] = state * el_ref[p, pl.ds(c * 8, 1), :] + jnp.where(same_head, upd, 0.0)
            z = z_ref[0, rows, lanes]
            ms = head_sumsq(o) * (1.0 / hd)
            o_ref[0, rows, lanes] = o * lax.rsqrt(ms + RMS_EPS) * nw_ref[...] * (z * jax.nn.sigmoid(z))
        return carry

    lax.fori_loop(0, seqlen // ch, scan, 0)


def _gdn_params(conv_w, a_log, dt_bias, norm_w):
    nh = a_log.shape[0]
    lane_pad = lambda v: jnp.pad(v.astype(F32), (0, LANE - nh)).reshape(1, LANE)
    return conv_w.astype(F32), lane_pad(a_log), lane_pad(dt_bias), jnp.tile(norm_w.astype(F32), 2).reshape(1, LANE)


def _gdn(q, k, v, z, ab, conv_w, al, dtb, nw):
    bsz, seqlen, gw = q.shape
    nh = gw // HEAD_DIM
    npair = nh // 2
    seq_spec = pl.BlockSpec((1, seqlen, gw), lambda b: (b, 0, 0))
    const = lambda b: (0, 0)
    return pl.pallas_call(
        functools.partial(_gdn_kernel, nh=nh, seqlen=seqlen),
        out_shape=jax.ShapeDtypeStruct((bsz, seqlen, gw), F32),
        grid=(bsz,),
        in_specs=[seq_spec, seq_spec, seq_spec, seq_spec,
                  pl.BlockSpec((1, seqlen, LANE), lambda b: (b, 0, 0)),
                  pl.BlockSpec(conv_w.shape, const),
                  pl.BlockSpec(al.shape, const),
                  pl.BlockSpec(dtb.shape, const),
                  pl.BlockSpec(nw.shape, const)],
        out_specs=seq_spec,
        scratch_shapes=[pltpu.VMEM((seqlen, LANE), F32)] * 2
        + [pltpu.VMEM((npair, seqlen, LANE), F32)]
        + [pltpu.VMEM((npair, seqlen, LANE), BF16)] * 4
        + [pltpu.VMEM((npair, seqlen // GDN_CHUNK * 8, LANE), F32)]
        + [pltpu.VMEM((npair, LANE, LANE), F32)],
        compiler_params=_cparams("parallel"),
        name="gdn",
    )(q, k, v, z, ab, conv_w, al, dtb, nw)


def _route(logits, n_exp):
    lane = _iota(logits.shape, 1)
    lg = jnp.where(lane < n_exp, logits, -jnp.inf)
    m1 = jnp.max(lg, axis=-1, keepdims=True)
    i1 = jnp.min(jnp.where(lg == m1, lane, LANE), axis=-1, keepdims=True)
    lg2 = jnp.where(lane == i1, -jnp.inf, lg)
    m2 = jnp.max(lg2, axis=-1, keepdims=True)
    i2 = jnp.min(jnp.where(lg2 == m2, lane, LANE), axis=-1, keepdims=True)
    e2 = jnp.exp(m2 - m1)
    g1 = 1.0 / (1.0 + e2)
    g2 = e2 / (1.0 + e2)
    gates = jnp.where(lane == i1, g1, 0.0) + jnp.where(lane == i2, g2, 0.0)
    mask = jnp.where((lane == i1 + n_exp) | (lane == i2 + n_exp), 1.0, 0.0)
    return gates + mask


def _outproj_kernel(x_ref, ys_ref, ym_ref, yg_ref, w_ref, g_ref, b_ref, *rest, alpha, s5w, mw, n_exp):
    mix = _dot(ys_ref[...].astype(BF16), w_ref[0:s5w, :])
    mix = mix + _dot(ym_ref[0].astype(BF16), w_ref[s5w:s5w + mw, :])
    mix = mix + _dot(yg_ref[0].astype(BF16), w_ref[s5w + mw:, :])
    x1 = _layer_norm(alpha * x_ref[...] + mix, g_ref[...], b_ref[...])
    if n_exp:
        rw_ref, o_ref, r_ref, ot_ref = rest
        o_ref[...] = x1
        r_ref[...] = _route(_dot(x1, rw_ref[...], precision=HI), n_exp)
        for s in range(TOKEN_TILE_ROWS):
            ot_ref[:, s, :] = x1[:, s * LANE:(s + 1) * LANE]
    else:
        (o_ref,) = rest
        o_ref[...] = x1


def _outproj(x2, ys, ym, yg, w, g, b, router_w, *, bsz, seqlen, alpha, n_exp):
    tl = 512
    nt = seqlen // tl
    t, d = x2.shape
    s5w = ys.shape[1] // bsz
    mw = ym.shape[2]
    gw = yg.shape[2]
    const = lambda bi, j: (0, 0)
    row_spec = pl.BlockSpec((tl, d), lambda bi, j: (bi * nt + j, 0))
    in_specs = [row_spec,
                pl.BlockSpec((tl, s5w), lambda bi, j: (j, bi)),
                pl.BlockSpec((1, tl, mw), lambda bi, j: (bi, j, 0)),
                pl.BlockSpec((1, tl, gw), lambda bi, j: (bi, j, 0)),
                pl.BlockSpec(w.shape, const), pl.BlockSpec(g.shape, const), pl.BlockSpec(b.shape, const)]
    args = [x2, ys, ym, yg, w, g, b]
    out_shape = jax.ShapeDtypeStruct((t, d), F32)
    out_specs = row_spec
    if n_exp:
        in_specs.append(pl.BlockSpec(router_w.shape, const))
        args.append(router_w)
        assert d == TOKEN_TILE_ROWS * LANE
        out_shape = (out_shape, jax.ShapeDtypeStruct((t, LANE), F32),
                     jax.ShapeDtypeStruct((t, TOKEN_TILE_ROWS, LANE), F32))
        out_specs = (row_spec, pl.BlockSpec((tl, LANE), lambda bi, j: (bi * nt + j, 0)),
                     pl.BlockSpec((tl, TOKEN_TILE_ROWS, LANE), lambda bi, j: (bi * nt + j, 0, 0)))
    return pl.pallas_call(
        functools.partial(_outproj_kernel, alpha=alpha, s5w=s5w, mw=mw, n_exp=n_exp),
        out_shape=out_shape,
        grid=(bsz, nt),
        in_specs=in_specs,
        out_specs=out_specs,
        compiler_params=_cparams("parallel", "arbitrary"),
        name="outproj_route" if n_exp else "outproj",
    )(*args)


def _ffn_body(xb_ref, wg_ref, wu_ref, wd_ref, acc_ref):
    xb = xb_ref[...]
    hg = _dot(xb, wg_ref[0])
    hu = _dot(xb, wu_ref[0])
    h = (hg * jax.nn.sigmoid(hg)) * hu
    part = _dot(h.astype(BF16), wd_ref[0])

    @pl.when(pl.program_id(1) == 0)
    def _():
        acc_ref[...] = part

    @pl.when(pl.program_id(1) > 0)
    def _():
        acc_ref[...] += part


def _ffn_dense_kernel(x_ref, wg_ref, wu_ref, wd_ref, g_ref, b_ref, o_ref, xb_ref, acc_ref, *, alpha):
    @pl.when(pl.program_id(1) == 0)
    def _():
        xb_ref[...] = x_ref[...].astype(BF16)

    _ffn_body(xb_ref, wg_ref, wu_ref, wd_ref, acc_ref)

    @pl.when(pl.program_id(1) == pl.num_programs(1) - 1)
    def _():
        o_ref[...] = _layer_norm(alpha * x_ref[...] + acc_ref[...], g_ref[...], b_ref[...])


def _ffn_dense(x2, wg, wu, wd, g, b, *, alpha):
    t, d = x2.shape
    ff = wg.shape[2]
    tm, tf = FFN_ROWS, FFN_COLS
    const = lambda i, f: (0, 0)
    return pl.pallas_call(
        functools.partial(_ffn_dense_kernel, alpha=alpha),
        out_shape=jax.ShapeDtypeStruct((t, d), F32),
        grid=(t // tm, ff // tf),
        in_specs=[pl.BlockSpec((tm, d), lambda i, f: (i, 0)),
                  pl.BlockSpec((1, d, tf), lambda i, f: (0, 0, f)),
                  pl.BlockSpec((1, d, tf), lambda i, f: (0, 0, f)),
                  pl.BlockSpec((1, tf, d), lambda i, f: (0, f, 0)),
                  pl.BlockSpec(g.shape, const), pl.BlockSpec(b.shape, const)],
        out_specs=pl.BlockSpec((tm, d), lambda i, f: (i, 0)),
        scratch_shapes=[pltpu.VMEM((tm, d), BF16), pltpu.VMEM((tm, d), F32)],
        compiler_params=_cparams("parallel", "arbitrary"),
        name="ffn_dense",
    )(x2, wg, wu, wd, g, b)


def _ffn_moe_kernel(tile_ref, exp_ref, lo_ref, hi_ref, first_ref, x_ref, wg_ref, wu_ref, wd_ref, o_ref, xt_ref, acc_ref):
    k = pl.program_id(0)
    f = pl.program_id(1)
    lo, hi = lo_ref[k], hi_ref[k]

    @pl.when(hi > lo)
    def _():
        @pl.when(f == 0)
        def _():
            for s in range(TOKEN_TILE_ROWS):
                xt_ref[:, s * LANE:(s + 1) * LANE] = x_ref[:, s, :].astype(BF16)

        _ffn_body(xt_ref, wg_ref, wu_ref, wd_ref, acc_ref)

        @pl.when(f == pl.num_programs(1) - 1)
        def _():
            row = _iota((x_ref.shape[0], LANE), 0)
            mine = (row >= lo) & (row < hi)

            @pl.when(first_ref[k] > 0)
            def _():
                for s in range(TOKEN_TILE_ROWS):
                    o_ref[:, s, :] = jnp.where(mine, acc_ref[:, s * LANE:(s + 1) * LANE], 0.0)

            @pl.when(first_ref[k] == 0)
            def _():
                for s in range(TOKEN_TILE_ROWS):
                    o_ref[:, s, :] = jnp.where(mine, acc_ref[:, s * LANE:(s + 1) * LANE], o_ref[:, s, :])


def _ffn_moe(items, xs, wg, wu, wd):
    p_rows = xs.shape[0]
    d = wg.shape[1]
    ff = wg.shape[2]
    tm, tf = MOE_ROWS, FFN_COLS
    nf = ff // tf
    n_items = items[0].shape[0]

    def fsel(k, f, lo, hi):
        return jnp.where(hi[k] > lo[k], f, nf - 1)

    tile_map = lambda k, f, tile, exp, lo, hi, first: (tile[k], 0, 0)
    grid_spec = pltpu.PrefetchScalarGridSpec(
        num_scalar_prefetch=5,
        grid=(n_items, nf),
        in_specs=[pl.BlockSpec((tm, TOKEN_TILE_ROWS, LANE), tile_map),
                  pl.BlockSpec((1, d, tf), lambda k, f, tile, exp, lo, hi, first: (exp[k], 0, fsel(k, f, lo, hi))),
                  pl.BlockSpec((1, d, tf), lambda k, f, tile, exp, lo, hi, first: (exp[k], 0, fsel(k, f, lo, hi))),
                  pl.BlockSpec((1, tf, d), lambda k, f, tile, exp, lo, hi, first: (exp[k], fsel(k, f, lo, hi), 0))],
        out_specs=pl.BlockSpec((tm, TOKEN_TILE_ROWS, LANE), tile_map),
        scratch_shapes=[pltpu.VMEM((tm, d), BF16), pltpu.VMEM((tm, d), F32)],
    )
    return pl.pallas_call(
        _ffn_moe_kernel,
        out_shape=jax.ShapeDtypeStruct((p_rows, TOKEN_TILE_ROWS, LANE), F32),
        grid_spec=grid_spec,
        compiler_params=_cparams("arbitrary", "arbitrary"),
        name="ffn_moe",
    )(*items, xs, wg, wu, wd)


def _tile_copy(src_ref, src_row, dst_ref, dst_row, sem):
    return pltpu.make_async_copy(src_ref.at[pl.ds(src_row, 1)], dst_ref.at[pl.ds(dst_row, 1)], sem)


def _dispatch_kernel(p0_ref, p1_ref, x_ref, xs_hbm, sem, *, rows):
    def start(r, carry):
        _tile_copy(x_ref, r, xs_hbm, p0_ref[0, 0, r], sem.at[0]).start()
        _tile_copy(x_ref, r, xs_hbm, p1_ref[0, 0, r], sem.at[1]).start()
        return carry

    lax.fori_loop(0, rows, start, 0, unroll=8)

    def wait(r, carry):
        _tile_copy(x_ref, r, xs_hbm, 0, sem.at[0]).wait()
        _tile_copy(x_ref, r, xs_hbm, 0, sem.at[1]).wait()
        return carry

    lax.fori_loop(0, rows, wait, 0, unroll=8)


def _dispatch(p0, p1, xt):
    nt, _, rows = p0.shape
    t = xt.shape[0]
    idx_spec = pl.BlockSpec((1, 1, rows), lambda i: (i, 0, 0), memory_space=pltpu.SMEM)
    return pl.pallas_call(
        functools.partial(_dispatch_kernel, rows=rows),
        out_shape=jax.ShapeDtypeStruct((t * TOP_K, TOKEN_TILE_ROWS, LANE), xt.dtype),
        grid=(nt,),
        in_specs=[idx_spec, idx_spec, pl.BlockSpec((rows, TOKEN_TILE_ROWS, LANE), lambda i: (i, 0, 0))],
        out_specs=pl.BlockSpec(memory_space=pl.ANY),
        scratch_shapes=[pltpu.SemaphoreType.DMA((2,))],
        compiler_params=_cparams("arbitrary"),
        name="moe_dispatch",
    )(p0, p1, xt)


def _combine_kernel(p0_ref, p1_ref, ys_hbm, x_ref, gt_ref, g_ref, b_ref, o_ref, y0_ref, y1_ref, sem, *, rows, alpha):
    def start(r, carry):
        _tile_copy(ys_hbm, p0_ref[0, 0, r], y0_ref, r, sem.at[0]).start()
        _tile_copy(ys_hbm, p1_ref[0, 0, r], y1_ref, r, sem.at[1]).start()
        return carry

    lax.fori_loop(0, rows, start, 0, unroll=8)

    def wait(r, carry):
        _tile_copy(ys_hbm, 0, y0_ref, r, sem.at[0]).wait()
        _tile_copy(ys_hbm, 0, y1_ref, r, sem.at[1]).wait()
        return carry

    lax.fori_loop(0, rows, wait, 0, unroll=8)
    gt = gt_ref[...]
    g0, g1 = gt[:, 0:1], gt[:, 1:2]
    for s in range(TOKEN_TILE_ROWS):
        cols = slice(s * LANE, (s + 1) * LANE)
        o_ref[:, cols] = alpha * x_ref[:, cols] + (g0 * y0_ref[:, s, :] + g1 * y1_ref[:, s, :])
    o_ref[...] = _layer_norm(o_ref[...], g_ref[...], b_ref[...])


def _combine(p0, p1, ys, x2, gates, g, b, *, alpha):
    nt, _, rows = p0.shape
    t, d = x2.shape
    const = lambda i: (0, 0)
    idx_spec = pl.BlockSpec((1, 1, rows), lambda i: (i, 0, 0), memory_space=pltpu.SMEM)
    tile_buf = pltpu.VMEM((rows, TOKEN_TILE_ROWS, LANE), F32)
    return pl.pallas_call(
        functools.partial(_combine_kernel, rows=rows, alpha=alpha),
        out_shape=jax.ShapeDtypeStruct((t, d), F32),
        grid=(nt,),
        in_specs=[idx_spec, idx_spec,
                  pl.BlockSpec(memory_space=pl.ANY),
                  pl.BlockSpec((rows, d), lambda i: (i, 0)),
                  pl.BlockSpec((rows, LANE), lambda i: (i, 0)),
                  pl.BlockSpec(g.shape, const), pl.BlockSpec(b.shape, const)],
        out_specs=pl.BlockSpec((rows, d), lambda i: (i, 0)),
        scratch_shapes=[tile_buf, tile_buf, pltpu.SemaphoreType.DMA((2,))],
        compiler_params=_cparams("arbitrary"),
        name="moe_combine",
    )(p0, p1, ys, x2, gates, g, b)


def _dispatch_plan(route, n_exp):
    t = route.shape[0]
    tm = MOE_ROWS
    gates = route[:, :n_exp]
    sel = route[:, n_exp:2 * n_exp] > 0.5
    seli = sel.astype(I32)
    rank = jnp.cumsum(seli, axis=0) - seli
    counts = jnp.sum(seli, axis=0)
    ends = jnp.cumsum(counts)
    starts = ends - counts
    pos = starts[None, :] + rank
    order = jnp.cumsum(seli, axis=1) - seli
    first = sel & (order == 0)
    second = sel & (order == 1)
    pick = lambda m, v: jnp.sum(jnp.where(m, v, 0), axis=1)
    p0, p1 = pick(first, pos), pick(second, pos)
    g01 = jnp.stack([pick(first, gates), pick(second, gates)], axis=1)
    g01 = jnp.pad(g01, ((0, 0), (0, LANE - TOP_K)))
    n_tiles = t * TOP_K // tm
    n_items = n_tiles + n_exp - 1
    tile_lo = jnp.arange(n_tiles, dtype=I32)[:, None] * tm
    ov_lo = jnp.maximum(starts[None, :], tile_lo)
    ov_hi = jnp.minimum(ends[None, :], tile_lo + tm)
    live = (ov_hi > ov_lo).reshape(-1)
    flat = jnp.arange(n_tiles * n_exp, dtype=I32)
    take = jnp.argsort(jnp.where(live, flat, n_tiles * n_exp))[:n_items]
    n_live = jnp.sum(live.astype(I32))
    idle = jnp.arange(n_items, dtype=I32) >= n_live
    last = take[jnp.maximum(n_live - 1, 0)]
    take = jnp.where(idle, last, take)
    item_tile = take // n_exp
    item_exp = take % n_exp
    item_lo = jnp.where(idle, 0, (ov_lo - tile_lo).reshape(-1)[take])
    item_hi = jnp.where(idle, 0, (ov_hi - tile_lo).reshape(-1)[take])
    prev_tile = jnp.concatenate([jnp.full((1,), -1, I32), item_tile[:-1]])
    item_first = (item_tile != prev_tile).astype(I32)
    items = tuple(v.astype(I32) for v in (item_tile, item_exp, item_lo, item_hi, item_first))
    shape3 = lambda v: v.astype(I32).reshape(-1, 1, GATHER_ROWS)
    return shape3(p0), shape3(p1), g01, items


def kernel(x, w_in, w_out, s5_lam_re, s5_lam_im, s5_log_dt, s5_b_re, s5_b_im, s5_c_re, s5_c_im, s5_d, s5_glu_w,
           s5_glu_b, gdn_conv_w, gdn_a_log, gdn_dt_bias, gdn_norm_w, ln1_g, ln1_b, ln2_g, ln2_b, ffn_w_gate,
           ffn_w_up, ffn_w_down, moe_router, moe_w_gate, moe_w_up, moe_w_down):
    bsz, seqlen, d = x.shape
    depth = w_in.shape[0]
    s5w = s5_glu_w.shape[1]
    gw = gdn_a_log.shape[1] * HEAD_DIM
    mw = w_out.shape[1] - s5w - gw
    n_exp = moe_router.shape[2]
    alpha = (2 * depth) ** 0.25
    row = lambda v: v.astype(F32).reshape(1, -1)

    x2 = x.reshape(bsz * seqlen, d)
    for l in range(depth):
        w, wt = _pack_inproj(w_in[l], s5w, mw, gw)
        u, mqt, mk, mvt, gq, gk, gv, gz, ab = _inproj(x2, w, wt, bsz=bsz, seqlen=seqlen, s5w=s5w, mw=mw, gw=gw)
        bblk, cblk, lam, dskip = _s5_params(s5_lam_re[l], s5_lam_im[l], s5_log_dt[l], s5_b_re[l], s5_b_im[l],
                                            s5_c_re[l], s5_c_im[l], s5_d[l])
        y_s5 = _s5(u.reshape(seqlen * bsz, s5w), bblk, cblk, lam, dskip, s5_glu_w[l].astype(BF16),
                   row(s5_glu_b[l]), bsz=bsz, seqlen=seqlen).reshape(seqlen, bsz * s5w)
        y_moba = _moba(mqt, mk, mvt, seqlen=seqlen)
        y_gdn = _gdn(gq, gk, gv, gz, ab, *_gdn_params(gdn_conv_w[l], gdn_a_log[l], gdn_dt_bias[l], gdn_norm_w[l]))
        routed = l % 2 == 1
        router_w = jnp.pad(moe_router[l // 2].astype(F32), ((0, 0), (0, LANE - n_exp))) if routed else None
        res = _outproj(x2, y_s5, y_moba, y_gdn, w_out[l].astype(BF16), row(ln1_g[l]), row(ln1_b[l]), router_w,
                       bsz=bsz, seqlen=seqlen, alpha=alpha, n_exp=n_exp if routed else 0)
        if routed:
            x1, route, x1t = res
            p0, p1, g01, items = _dispatch_plan(route, n_exp)
            xs = _dispatch(p0, p1, x1t)
            ys = _ffn_moe(items, xs, moe_w_gate[l // 2].astype(BF16),
                          moe_w_up[l // 2].astype(BF16), moe_w_down[l // 2].astype(BF16))
            x2 = _combine(p0, p1, ys, x1, g01, row(ln2_g[l]), row(ln2_b[l]), alpha=alpha)
        else:
            x2 = _ffn_dense(res, ffn_w_gate[l // 2:l // 2 + 1].astype(BF16), ffn_w_up[l // 2:l // 2 + 1].astype(BF16),
                            ffn_w_down[l // 2:l // 2 + 1].astype(BF16), row(ln2_g[l]), row(ln2_b[l]), alpha=alpha)
    return x2.reshape(bsz, seqlen, d)
```

```python
import functools

import jax
import jax.numpy as jnp
from jax import lax
from jax.experimental import pallas as pl
from jax.experimental.pallas import tpu as pltpu

F32 = jnp.float32
BF16 = jnp.bfloat16
I32 = jnp.int32
HI = lax.Precision.HIGHEST

HEAD_DIM = 64
MOBA_BLOCK = 256
MOBA_TOPK = 3
GDN_CHUNK = 64
GDN_PREP_CHUNKS = 2
TOP_K = 2
LN_EPS = 1e-5
RMS_EPS = 1e-6
NEG_BIG = -1e30

V7X_VMEM_LIMIT_BYTES = 56 * 1024 * 1024
LANE = 128
TOKEN_TILE_ROWS = 8

FFN_ROWS = 1024
MOE_ROWS = 1024
MOE_SUB_ROWS = 256
FFN_COLS = 512
GATHER_ROWS = 256


def _cparams(*sem):
    return pltpu.CompilerParams(dimension_semantics=sem, vmem_limit_bytes=V7X_VMEM_LIMIT_BYTES)


def _dot(a, b, precision=None):
    return jnp.dot(a, b, preferred_element_type=F32, precision=precision)


def _dot_nt(a, b, precision=None):
    return lax.dot_general(a, b, (((1,), (1,)), ((), ())), preferred_element_type=F32, precision=precision)


def _dot_tn(a, b, precision=None):
    return lax.dot_general(a, b, (((0,), (0,)), ((), ())), preferred_element_type=F32, precision=precision)


def _iota(shape, dim):
    return lax.broadcasted_iota(I32, shape, dim)


def _layer_norm(r, g, b):
    mu = jnp.mean(r, axis=-1, keepdims=True)
    c = r - mu
    var = jnp.mean(c * c, axis=-1, keepdims=True)
    return c * lax.rsqrt(var + LN_EPS) * g + b


def _inproj_kernel(x_ref, w_ref, wt_ref, u_ref, mqt_ref, mk_ref, mvt_ref, gq_ref, gk_ref, gv_ref, gz_ref,
                   ab_ref, *, s5w, mw, gw):
    xb = x_ref[...].astype(BF16)
    nh_m = mw // HEAD_DIM

    def cols(off, width):
        return _dot(xb, w_ref[:, off:off + width])

    u_ref[...] = cols(0, s5w)
    off = s5w + mw
    kk = cols(off, mw)
    for h in range(nh_m):
        mk_ref[0, h, 0] = kk[:, h * HEAD_DIM:(h + 1) * HEAD_DIM]
    off += 2 * mw
    for ref in (gq_ref, gk_ref, gv_ref, gz_ref):
        ref[0] = cols(off, gw)
        off += gw
    ab_ref[0] = cols(off, LANE)
    qt = _dot_nt(wt_ref[0:mw, :], xb)
    vt = _dot_nt(wt_ref[mw:2 * mw, :], xb)
    for h in range(nh_m):
        mqt_ref[0, h, 0] = qt[h * HEAD_DIM:(h + 1) * HEAD_DIM, :]
        mvt_ref[0, h, 0] = vt[h * HEAD_DIM:(h + 1) * HEAD_DIM, :]


def _pack_inproj(w_in, s5w, mw, gw):
    c = w_in.shape[1]
    main = s5w + 3 * mw + 4 * gw
    w = jnp.pad(w_in, ((0, 0), (0, main + LANE - c))).astype(BF16)
    q0, v0 = s5w, s5w + 2 * mw
    wt = jnp.concatenate([w_in[:, q0:q0 + mw], w_in[:, v0:v0 + mw]], axis=1).T.astype(BF16)
    return w, wt


def _inproj(x2, w, wt, *, bsz, seqlen, s5w, mw, gw):
    tl = MOBA_BLOCK
    nb = seqlen // tl
    d = x2.shape[1]
    nh_m = mw // HEAD_DIM
    hd = HEAD_DIM
    gdn_sds = jax.ShapeDtypeStruct((bsz, seqlen, gw), F32)
    out_shape = (
        jax.ShapeDtypeStruct((seqlen, bsz * s5w), F32),
        jax.ShapeDtypeStruct((bsz, nh_m, nb, hd, tl), F32),
        jax.ShapeDtypeStruct((bsz, nh_m, nb, tl, hd), F32),
        jax.ShapeDtypeStruct((bsz, nh_m, nb, hd, tl), F32),
        gdn_sds, gdn_sds, gdn_sds, gdn_sds,
        jax.ShapeDtypeStruct((bsz, seqlen, LANE), F32),
    )
    t5 = lambda b, j: (b, 0, j, 0, 0)
    t3 = lambda b, j: (b, j, 0)
    out_specs = (
        pl.BlockSpec((tl, s5w), lambda b, j: (j, b)),
        pl.BlockSpec((1, nh_m, 1, hd, tl), t5),
        pl.BlockSpec((1, nh_m, 1, tl, hd), t5),
        pl.BlockSpec((1, nh_m, 1, hd, tl), t5),
        pl.BlockSpec((1, tl, gw), t3),
        pl.BlockSpec((1, tl, gw), t3),
        pl.BlockSpec((1, tl, gw), t3),
        pl.BlockSpec((1, tl, gw), t3),
        pl.BlockSpec((1, tl, LANE), t3),
    )
    return pl.pallas_call(
        functools.partial(_inproj_kernel, s5w=s5w, mw=mw, gw=gw),
        out_shape=out_shape,
        grid=(bsz, nb),
        in_specs=[
            pl.BlockSpec((tl, d), lambda b, j: (b * nb + j, 0)),
            pl.BlockSpec(w.shape, lambda b, j: (0, 0)),
            pl.BlockSpec(wt.shape, lambda b, j: (0, 0)),
        ],
        out_specs=out_specs,
        compiler_params=_cparams("parallel", "arbitrary"),
        name="inproj",
    )(x2, w, wt)


def _s5_kernel(u_ref, bblk_ref, cblk_ref, lam_ref, d_ref, gw_ref, gb_ref, y_ref, st_ref, h_ref, *, bsz, tt, ns):
    @pl.when(pl.program_id(0) == 0)
    def _():
        h_ref[...] = jnp.zeros_like(h_ref)

    u = u_ref[...]
    st_ref[...] = _dot(u.astype(BF16), bblk_ref[...])
    lam_re = jnp.broadcast_to(lam_ref[0:1, :], (bsz, ns))
    lam_im = jnp.broadcast_to(lam_ref[1:2, :], (bsz, ns))

    def step(t, carry):
        h_re, h_im = carry
        r = pl.multiple_of(t * bsz, bsz)
        n_re = lam_re * h_re - lam_im * h_im + st_ref[pl.ds(r, bsz), 0:ns]
        n_im = lam_re * h_im + lam_im * h_re + st_ref[pl.ds(r, bsz), ns:2 * ns]
        st_ref[pl.ds(r, bsz), 0:ns] = n_re
        st_ref[pl.ds(r, bsz), ns:2 * ns] = n_im
        return n_re, n_im

    h_re, h_im = lax.fori_loop(0, tt, step, (h_ref[0], h_ref[1]), unroll=8)
    h_ref[0] = h_re
    h_ref[1] = h_im
    y = _dot(st_ref[...].astype(BF16), cblk_ref[...]) + d_ref[...] * u
    y = jax.nn.gelu(y)
    y_ref[...] = y * jax.nn.sigmoid(_dot(y.astype(BF16), gw_ref[...]) + gb_ref[...])


def _s5(u2, bblk, cblk, lam, d, glu_w, glu_b, *, bsz, seqlen):
    width = u2.shape[1]
    ns = lam.shape[1]
    tt = 64
    rows = tt * bsz
    const = lambda i: (0, 0)
    return pl.pallas_call(
        functools.partial(_s5_kernel, bsz=bsz, tt=tt, ns=ns),
        out_shape=jax.ShapeDtypeStruct(u2.shape, F32),
        grid=(seqlen // tt,),
        in_specs=[
            pl.BlockSpec((rows, width), lambda i: (i, 0)),
            pl.BlockSpec(bblk.shape, const),
            pl.BlockSpec(cblk.shape, const),
            pl.BlockSpec(lam.shape, const),
            pl.BlockSpec(d.shape, const),
            pl.BlockSpec(glu_w.shape, const),
            pl.BlockSpec(glu_b.shape, const),
        ],
        out_specs=pl.BlockSpec((rows, width), lambda i: (i, 0)),
        scratch_shapes=[pltpu.VMEM((rows, 2 * ns), F32), pltpu.VMEM((2, bsz, ns), F32)],
        compiler_params=_cparams("arbitrary"),
        name="s5",
    )(u2, bblk, cblk, lam, d, glu_w, glu_b)


def _s5_params(lam_re, lam_im, log_dt, b_re, b_im, c_re, c_im, d_skip):
    g, n = lam_re.shape
    p = b_re.shape[-1]
    lam = lax.complex(lam_re.astype(F32), lam_im.astype(F32))
    step = jnp.exp(log_dt.astype(F32))[:, None]
    lam_bar = jnp.exp(lam * step)
    b_bar = ((lam_bar - 1.0) / lam)[..., None] * lax.complex(b_re.astype(F32), b_im.astype(F32))
    eye = jnp.eye(g, dtype=F32)
    b_re_blk = jnp.einsum('gnp,gh->gphn', b_bar.real, eye).reshape(g * p, g * n)
    b_im_blk = jnp.einsum('gnp,gh->gphn', b_bar.imag, eye).reshape(g * p, g * n)
    bblk = jnp.concatenate([b_re_blk, b_im_blk], axis=1)
    c_re_blk = jnp.einsum('gpn,gh->gnhp', c_re.astype(F32), eye).reshape(g * n, g * p)
    c_im_blk = jnp.einsum('gpn,gh->gnhp', c_im.astype(F32), eye).reshape(g * n, g * p)
    cblk = jnp.concatenate([c_re_blk, -c_im_blk], axis=0)
    lam2 = jnp.stack([lam_bar.real.reshape(g * n), lam_bar.imag.reshape(g * n)])
    return bblk.astype(BF16), cblk.astype(BF16), lam2, d_skip.astype(F32).reshape(1, g * p)


def _moba_kernel(qt_ref, k_ref, vt_ref, o_ref, km_ref, qs_ref, bias_ref, acc_ref, *, nh, nb):
    i = pl.program_id(1)
    blk = MOBA_BLOCK
    hd = HEAD_DIM

    @pl.when(i == 0)
    def _():
        avg = jnp.full((1, blk), 1.0 / blk, F32)
        for h in range(nh):
            for n in range(nb):
                km_ref[h, n:n + 1, :] = _dot(avg, k_ref[0, h, n], precision=HI)

    n_iota = _iota((nb, blk), 0)
    kpos = _iota((blk, blk), 0)
    qpos = _iota((blk, blk), 1)
    causal = kpos <= qpos

    heads = range(nh)

    qts = [qt_ref[0, h, 0] * (hd ** -0.5) for h in heads]
    qtb = [qts[h].astype(BF16) for h in heads]
    s_own = [_dot(k_ref[0, h, i].astype(BF16), qtb[h]) for h in heads]
    gates = [_dot(km_ref[h], qts[h], precision=HI) for h in heads]
    vt_own = [vt_ref[0, h, i].astype(BF16) for h in heads]
    biases = []
    for h in heads:
        gate = jnp.where(n_iota < i, gates[h], -jnp.inf)
        rank = jnp.zeros((nb, blk), I32)
        for m in range(nb):
            gm = gate[m:m + 1, :]
            ahead = (gm > gate) | ((gm == gate) & (m < n_iota))
            rank = rank + ahead.astype(I32)
        biases.append(jnp.where((n_iota < i) & (rank < MOBA_TOPK), 0.0, NEG_BIG))
    m_own, l_own, acc_own = [], [], []
    for h in heads:
        s = jnp.where(causal, s_own[h], NEG_BIG)
        m0 = jnp.max(s, axis=0, keepdims=True)
        p = jnp.exp(s - m0)
        m_own.append(m0)
        l_own.append(jnp.sum(p, axis=0, keepdims=True))
        acc_own.append(_dot(vt_own[h], p.astype(BF16)))
    for h in heads:
        bias_ref[h] = biases[h]
        qs_ref[h] = qtb[h]
        acc_ref[h * hd:(h + 1) * hd, :] = acc_own[h]

    def body(j, carry):
        ms, ls = carry
        ss = [_dot(k_ref[0, h, j].astype(BF16), qs_ref[h]) + bias_ref[h, pl.ds(j, 1), :] for h in heads]
        vts = [vt_ref[0, h, j].astype(BF16) for h in heads]
        accs = [acc_ref[h * hd:(h + 1) * hd, :] for h in heads]
        new_ms, new_ls, new_accs = [], [], []
        for h in heads:
            m_new = jnp.maximum(ms[h], jnp.max(ss[h], axis=0, keepdims=True))
            p = jnp.exp(ss[h] - m_new)
            corr = jnp.exp(ms[h] - m_new)
            new_ms.append(m_new)
            new_ls.append(ls[h] * corr + jnp.sum(p, axis=0, keepdims=True))
            new_accs.append(accs[h] * corr + _dot(vts[h], p.astype(BF16)))
        for h in heads:
            acc_ref[h * hd:(h + 1) * hd, :] = new_accs[h]
        return tuple(new_ms), tuple(new_ls)

    _, ls = lax.fori_loop(0, i, body, (tuple(m_own), tuple(l_own)))
    for h in heads:
        rows = slice(h * hd, (h + 1) * hd)
        acc_ref[rows, :] = acc_ref[rows, :] / ls[h]
    eye = (kpos == qpos).astype(F32)
    o_ref[0] = _dot_nt(eye, acc_ref[...], precision=HI)


def _moba(qt, k, vt, *, seqlen):
    bsz, nh, nb, hd, blk = qt.shape
    return pl.pallas_call(
        functools.partial(_moba_kernel, nh=nh, nb=nb),
        out_shape=jax.ShapeDtypeStruct((bsz, seqlen, nh * hd), F32),
        grid=(bsz, nb),
        in_specs=[
            pl.BlockSpec((1, nh, 1, hd, blk), lambda b, i: (b, 0, i, 0, 0)),
            pl.BlockSpec((1, nh, nb, blk, hd), lambda b, i: (b, 0, 0, 0, 0)),
            pl.BlockSpec((1, nh, nb, hd, blk), lambda b, i: (b, 0, 0, 0, 0)),
        ],
        out_specs=pl.BlockSpec((1, blk, nh * hd), lambda b, i: (b, i, 0)),
        scratch_shapes=[pltpu.VMEM((nh, nb, hd), F32), pltpu.VMEM((nh, hd, blk), BF16),
                        pltpu.VMEM((nh, nb, blk), F32), pltpu.VMEM((nh * hd, blk), F32)],
        compiler_params=_cparams("parallel", "arbitrary"),
        name="moba",
    )(qt, k, vt)


def _gdn_kernel(q_ref, k_ref, v_ref, z_ref, ab_ref, cw_ref, al_ref, dtb_ref, nw_ref, o_ref,
                g_ref, beta_ref, us_ref, ws_ref, qe_ref, kd_ref, at_ref, el_ref, s_ref, *, nh, seqlen):
    npair = nh // 2
    ch = GDN_CHUNK
    hd = HEAD_DIM
    gw = nh * hd
    pw = 2 * hd
    pairs = range(npair)

    rr = _iota((pw, pw), 0)
    cc = _iota((pw, pw), 1)
    same_head = (rr // hd) == (cc // hd)
    mask_incl = same_head & ((rr % ch) >= (cc % ch))
    mask_strict = same_head & ((rr % ch) > (cc % ch))
    eye = (rr == cc).astype(F32)
    tril_c = (_iota((ch, ch), 0) >= _iota((ch, ch), 1)).astype(F32)
    lane_c = _iota((ch, LANE), 1)
    first = lane_c < hd
    row_col = _iota((pw, 1), 0)
    lane_row = _iota((1, pw), 1)

    def stack(x):
        return jnp.concatenate([jnp.where(first, x, 0.0), jnp.where(first, 0.0, x)], axis=0)

    def fold(x):
        return x[:ch] + x[ch:]

    def head_sumsq(y):
        y2 = y * y
        sa = jnp.sum(jnp.where(first, y2, 0.0), axis=1, keepdims=True)
        sb = jnp.sum(jnp.where(first, 0.0, y2), axis=1, keepdims=True)
        return jnp.where(first, sa, sb)

    a = ab_ref[0]
    ea = a + dtb_ref[...]
    g_ref[...] = -jnp.exp(al_ref[...]) * (jnp.maximum(ea, 0.0) + jnp.log1p(jnp.exp(-jnp.abs(ea))))
    beta_ref[...] = jax.nn.sigmoid(a)

    def conv_silu(src, t, p, c, r0):
        lanes = slice(p * pw, (p + 1) * pw)
        h0 = pl.multiple_of(jnp.maximum(r0 - 8, 0), 8)
        hist = jnp.where(c > 0, src[0, pl.ds(h0, 8), lanes], 0.0)
        x = jnp.concatenate([hist, src[0, pl.ds(r0, ch), lanes]], axis=0)
        c0 = t * gw + p * pw
        acc = x * cw_ref[3:4, c0:c0 + pw]
        for tap in range(3):
            acc = acc + pltpu.roll(x, 3 - tap, 0) * cw_ref[tap:tap + 1, c0:c0 + pw]
        y = acc[8:]
        return y * jax.nn.sigmoid(y)

    def prep(cc, carry):
        chunks = [cc * GDN_PREP_CHUNKS + u for u in range(GDN_PREP_CHUNKS)]
        r0s = [pl.multiple_of(c * ch, ch) for c in chunks]
        gcs = [_dot(tril_c, g_ref[pl.ds(r0, ch), :], precision=HI) for r0 in r0s]
        bch = [beta_ref[pl.ds(r0, ch), :] for r0 in r0s]
        streams = [(u, p) for u in range(GDN_PREP_CHUNKS) for p in pairs]
        ns = range(len(streams))
        qs, ks, vs, gcol, bcol, decay = [], [], [], [], [], []
        for u, p in streams:
            q = conv_silu(q_ref, 0, p, chunks[u], r0s[u])
            k = conv_silu(k_ref, 1, p, chunks[u], r0s[u])
            v = conv_silu(v_ref, 2, p, chunks[u], r0s[u])
            q = q * (lax.rsqrt(head_sumsq(q) + RMS_EPS) * (hd ** -0.5))
            k = k * lax.rsqrt(head_sumsq(k) + RMS_EPS)
            qs.append(stack(q))
            ks.append(stack(k))
            vs.append(stack(v))
            ha, hb = 2 * p, 2 * p + 1
            gst = jnp.concatenate([jnp.where(lane_c == ha, gcs[u], 0.0), jnp.where(lane_c == hb, gcs[u], 0.0)], axis=0)
            bst = jnp.concatenate([jnp.where(lane_c == nh + ha, bch[u], 0.0),
                                   jnp.where(lane_c == nh + hb, bch[u], 0.0)], axis=0)
            gc = jnp.sum(gst, axis=1, keepdims=True)
            gcol.append(gc)
            bcol.append(jnp.sum(bst, axis=1, keepdims=True))
            gmat = jnp.broadcast_to(gc, (pw, pw))
            decay.append(jnp.where(mask_incl, jnp.exp(jnp.where(mask_incl, gmat - gmat.T, 0.0)), 0.0))
        kbs = [ks[n] * bcol[n] for n in ns]
        ksb = [ks[n].astype(BF16) for n in ns]
        pm = [jnp.where(mask_strict, _dot_nt(kbs[n].astype(BF16), ksb[n]) * decay[n], 0.0) for n in ns]
        tm = [eye - pm[n] for n in ns]
        pmb = [pm[n].astype(BF16) for n in ns]
        pm = [_dot(pmb[n], pmb[n]) for n in ns]
        for _ in range(4):
            pmb = [pm[n].astype(BF16) for n in ns]
            tm = [tm[n] + _dot(tm[n].astype(BF16), pmb[n]) for n in ns]
            pm = [_dot(pmb[n], pmb[n]) for n in ns]
        tm = [tm[n] + _dot(tm[n].astype(BF16), pm[n].astype(BF16)) for n in ns]
        out = []
        for n in ns:
            eg = jnp.exp(gcol[n])
            rhs = jnp.concatenate([vs[n] * bcol[n], kbs[n] * eg], axis=1).astype(BF16)
            sol = _dot(tm[n].astype(BF16), rhs)
            attn = jnp.where(mask_incl, _dot_nt(qs[n].astype(BF16), ksb[n]) * decay[n], 0.0)
            g_a = gcol[n][ch - 1:ch]
            g_b = gcol[n][2 * ch - 1:2 * ch]
            glast = jnp.where(row_col < ch, g_a, g_b)
            e_last = jnp.where(lane_row < hd, jnp.exp(g_a), jnp.exp(g_b))
            out.append((fold(sol[:, :pw]), fold(sol[:, pw:]).astype(BF16), fold(qs[n] * eg).astype(BF16),
                        fold(ks[n] * jnp.exp(glast - gcol[n])).astype(BF16), fold(attn).astype(BF16),
                        jnp.broadcast_to(e_last, (8, pw))))
        for n, (u, p) in enumerate(streams):
            rows = pl.ds(r0s[u], ch)
            us_ref[p, rows, :], ws_ref[p, rows, :], qe_ref[p, rows, :], kd_ref[p, rows, :], at_ref[p, rows, :] = out[n][:5]
            el_ref[p, pl.ds(pl.multiple_of(chunks[u] * 8, 8), 8), :] = out[n][5]
        return carry

    lax.fori_loop(0, seqlen // (ch * GDN_PREP_CHUNKS), prep, 0)

    s_ref[...] = jnp.zeros_like(s_ref)

    def scan(c, carry):
        r0 = pl.multiple_of(c * ch, ch)
        rows = pl.ds(r0, ch)
        for p in pairs:
            lanes = slice(p * pw, (p + 1) * pw)
            state = s_ref[p]
            wq = jnp.concatenate([ws_ref[p, rows, :], qe_ref[p, rows, :]], axis=0)
            wqs = _dot(wq, state.astype(BF16))
            v_new = us_ref[p, rows, :] - wqs[:ch]
            o = wqs[ch:] + _dot(at_ref[p, rows, :], stack(v_new).astype(BF16))
            upd = _dot_tn(kd_ref[p, rows, :], v_new.astype(BF16))
            s_ref[p] = state * el_ref[p, pl.ds(c * 8, 1), :] + jnp.where(same_head, upd, 0.0)
            z = z_ref[0, rows, lanes]
            ms = head_sumsq(o) * (1.0 / hd)
            o_ref[0, rows, lanes] = o * lax.rsqrt(ms + RMS_EPS) * nw_ref[...] * (z * jax.nn.sigmoid(z))
        return carry

    lax.fori_loop(0, seqlen // ch, scan, 0)


def _gdn_params(conv_w, a_log, dt_bias, norm_w):
    nh = a_log.shape[0]
    lane_pad = lambda v: jnp.pad(v.astype(F32), (0, LANE - nh)).reshape(1, LANE)
    return conv_w.astype(F32), lane_pad(a_log), lane_pad(dt_bias), jnp.tile(norm_w.astype(F32), 2).reshape(1, LANE)


def _gdn(q, k, v, z, ab, conv_w, al, dtb, nw):
    bsz, seqlen, gw = q.shape
    nh = gw // HEAD_DIM
    npair = nh // 2
    seq_spec = pl.BlockSpec((1, seqlen, gw), lambda b: (b, 0, 0))
    const = lambda b: (0, 0)
    return pl.pallas_call(
        functools.partial(_gdn_kernel, nh=nh, seqlen=seqlen),
        out_shape=jax.ShapeDtypeStruct((bsz, seqlen, gw), F32),
        grid=(bsz,),
        in_specs=[seq_spec, seq_spec, seq_spec, seq_spec,
                  pl.BlockSpec((1, seqlen, LANE), lambda b: (b, 0, 0)),
                  pl.BlockSpec(conv_w.shape, const),
                  pl.BlockSpec(al.shape, const),
                  pl.BlockSpec(dtb.shape, const),
                  pl.BlockSpec(nw.shape, const)],
        out_specs=seq_spec,
        scratch_shapes=[pltpu.VMEM((seqlen, LANE), F32)] * 2
        + [pltpu.VMEM((npair, seqlen, LANE), F32)]
        + [pltpu.VMEM((npair, seqlen, LANE), BF16)] * 4
        + [pltpu.VMEM((npair, seqlen // GDN_CHUNK * 8, LANE), F32)]
        + [pltpu.VMEM((npair, LANE, LANE), F32)],
        compiler_params=_cparams("parallel"),
        name="gdn",
    )(q, k, v, z, ab, conv_w, al, dtb, nw)


def _route(logits):
    n_exp = logits.shape[0]
    eidx = _iota(logits.shape, 0)
    m1 = jnp.max(logits, axis=0, keepdims=True)
    i1 = jnp.min(jnp.where(logits == m1, eidx, n_exp), axis=0, keepdims=True)
    lg2 = jnp.where(eidx == i1, -jnp.inf, logits)
    m2 = jnp.max(lg2, axis=0, keepdims=True)
    i2 = jnp.min(jnp.where(lg2 == m2, eidx, n_exp), axis=0, keepdims=True)
    e2 = jnp.exp(m2 - m1)
    g1 = 1.0 / (1.0 + e2)
    g2 = e2 / (1.0 + e2)
    gates = jnp.where(eidx == i1, g1, 0.0) + jnp.where(eidx == i2, g2, 0.0)
    mask = jnp.where((eidx == i1) | (eidx == i2), 1.0, 0.0)
    return jnp.concatenate([gates, mask], axis=0)


def _outproj_kernel(x_ref, ys_ref, ym_ref, yg_ref, w_ref, g_ref, b_ref, *rest, alpha, s5w, mw, n_exp):
    mix = _dot(ys_ref[...].astype(BF16), w_ref[0:s5w, :])
    mix = mix + _dot(ym_ref[0].astype(BF16), w_ref[s5w:s5w + mw, :])
    mix = mix + _dot(yg_ref[0].astype(BF16), w_ref[s5w + mw:, :])
    x1 = _layer_norm(alpha * x_ref[...] + mix, g_ref[...], b_ref[...])
    if n_exp:
        rw_ref, o_ref, r_ref, ot_ref = rest
        o_ref[...] = x1
        r_ref[...] = _route(_dot_nt(rw_ref[...], x1, precision=HI))
        for s in range(TOKEN_TILE_ROWS):
            ot_ref[:, s, :] = x1[:, s * LANE:(s + 1) * LANE]
    else:
        (o_ref,) = rest
        o_ref[...] = x1


def _outproj(x2, ys, ym, yg, w, g, b, router_w, *, bsz, seqlen, alpha, n_exp):
    tl = 512
    nt = seqlen // tl
    t, d = x2.shape
    s5w = ys.shape[1] // bsz
    mw = ym.shape[2]
    gw = yg.shape[2]
    const = lambda bi, j: (0, 0)
    row_spec = pl.BlockSpec((tl, d), lambda bi, j: (bi * nt + j, 0))
    in_specs = [row_spec,
                pl.BlockSpec((tl, s5w), lambda bi, j: (j, bi)),
                pl.BlockSpec((1, tl, mw), lambda bi, j: (bi, j, 0)),
                pl.BlockSpec((1, tl, gw), lambda bi, j: (bi, j, 0)),
                pl.BlockSpec(w.shape, const), pl.BlockSpec(g.shape, const), pl.BlockSpec(b.shape, const)]
    args = [x2, ys, ym, yg, w, g, b]
    out_shape = jax.ShapeDtypeStruct((t, d), F32)
    out_specs = row_spec
    if n_exp:
        in_specs.append(pl.BlockSpec(router_w.shape, const))
        args.append(router_w)
        assert d == TOKEN_TILE_ROWS * LANE
        out_shape = (out_shape, jax.ShapeDtypeStruct((2 * n_exp, t), F32),
                     jax.ShapeDtypeStruct((t, TOKEN_TILE_ROWS, LANE), F32))
        out_specs = (row_spec, pl.BlockSpec((2 * n_exp, tl), lambda bi, j: (0, bi * nt + j)),
                     pl.BlockSpec((tl, TOKEN_TILE_ROWS, LANE), lambda bi, j: (bi * nt + j, 0, 0)))
    return pl.pallas_call(
        functools.partial(_outproj_kernel, alpha=alpha, s5w=s5w, mw=mw, n_exp=n_exp),
        out_shape=out_shape,
        grid=(bsz, nt),
        in_specs=in_specs,
        out_specs=out_specs,
        compiler_params=_cparams("parallel", "arbitrary"),
        name="outproj_route" if n_exp else "outproj",
    )(*args)


def _swiglu_part(xb, wg, wu, wd):
    hg = _dot(xb, wg)
    hu = _dot(xb, wu)
    h = (hg * jax.nn.sigmoid(hg)) * hu
    return _dot(h.astype(BF16), wd)


def _ffn_dense_kernel(x_ref, wg_ref, wu_ref, wd_ref, g_ref, b_ref, o_ref, xb_ref, acc_ref, *, alpha):
    @pl.when(pl.program_id(1) == 0)
    def _():
        xb_ref[...] = x_ref[...].astype(BF16)
        acc_ref[...] = jnp.zeros_like(acc_ref)

    acc_ref[...] += _swiglu_part(xb_ref[...], wg_ref[0].astype(BF16), wu_ref[0].astype(BF16),
                                 wd_ref[0].astype(BF16))

    @pl.when(pl.program_id(1) == pl.num_programs(1) - 1)
    def _():
        o_ref[...] = _layer_norm(alpha * x_ref[...] + acc_ref[...], g_ref[...], b_ref[...])


def _ffn_dense(x2, wg, wu, wd, layer, g, b, *, alpha):
    t, d = x2.shape
    ff = wg.shape[2]
    tm, tf = FFN_ROWS, FFN_COLS
    const = lambda i, f: (0, 0)
    return pl.pallas_call(
        functools.partial(_ffn_dense_kernel, alpha=alpha),
        out_shape=jax.ShapeDtypeStruct((t, d), F32),
        grid=(t // tm, ff // tf),
        in_specs=[pl.BlockSpec((tm, d), lambda i, f: (i, 0)),
                  pl.BlockSpec((1, d, tf), lambda i, f: (layer, 0, f)),
                  pl.BlockSpec((1, d, tf), lambda i, f: (layer, 0, f)),
                  pl.BlockSpec((1, tf, d), lambda i, f: (layer, f, 0)),
                  pl.BlockSpec(g.shape, const), pl.BlockSpec(b.shape, const)],
        out_specs=pl.BlockSpec((tm, d), lambda i, f: (i, 0)),
        scratch_shapes=[pltpu.VMEM((tm, d), BF16), pltpu.VMEM((tm, d), F32)],
        compiler_params=_cparams("parallel", "arbitrary"),
        name="ffn_dense",
    )(x2, wg, wu, wd, g, b)


def _ffn_moe_kernel(tile_ref, exp_ref, lo_ref, hi_ref, first_ref, x_ref, wg_ref, wu_ref, wd_ref, o_ref,
                    xt_ref, xb_ref, acc_ref, wgb_ref, wub_ref, wdb_ref):
    k = pl.program_id(0)
    f = pl.program_id(1)
    lo, hi = lo_ref[k], hi_ref[k]
    sub = MOE_SUB_ROWS
    d = o_ref.shape[1]

    def each_live_sub_block(fn):
        for sb in range(o_ref.shape[0] // sub):
            pl.when((lo < (sb + 1) * sub) & (hi > sb * sub))(functools.partial(fn, sb, slice(sb * sub, (sb + 1) * sub)))

    @pl.when(hi > lo)
    def _():
        @pl.when((f == 0) & (first_ref[k] > 0))
        def _():
            o_ref[...] = jnp.zeros_like(o_ref)

        def load_rows(sb, rows):
            for s in range(TOKEN_TILE_ROWS):
                xt_ref[rows, s * LANE:(s + 1) * LANE] = x_ref[rows, s, :]
            xb_ref[rows, :] = xt_ref[rows, :].astype(BF16)
            acc_ref[rows, :] = jnp.zeros((sub, d), F32)

        pl.when(f == 0)(functools.partial(each_live_sub_block, load_rows))

        wgb_ref[...] = wg_ref[0, 0].astype(BF16)
        wub_ref[...] = wu_ref[0, 0].astype(BF16)
        wdb_ref[...] = wd_ref[0, 0].astype(BF16)

        def compute(sb, rows):
            acc_ref[rows, :] += _swiglu_part(xb_ref[rows, :], wgb_ref[...], wub_ref[...], wdb_ref[...])

        each_live_sub_block(compute)

        def emit(sb, rows):
            row = _iota((sub, d), 0) + sb * sub
            o_ref[rows, :] = jnp.where((row >= lo) & (row < hi), acc_ref[rows, :], o_ref[rows, :])

        pl.when(f == pl.num_programs(1) - 1)(functools.partial(each_live_sub_block, emit))


def _ffn_moe(items, xs, wg, wu, wd, layer):
    p_rows = xs.shape[0]
    d = wg.shape[2]
    ff = wg.shape[3]
    tm, tf = MOE_ROWS, FFN_COLS
    nf = ff // tf
    n_items = items[0].shape[0]

    def fsel(k, f, lo, hi):
        return jnp.where(hi[k] > lo[k], f, nf - 1)

    grid_spec = pltpu.PrefetchScalarGridSpec(
        num_scalar_prefetch=5,
        grid=(n_items, nf),
        in_specs=[pl.BlockSpec((tm, TOKEN_TILE_ROWS, LANE), lambda k, f, tile, exp, lo, hi, first: (tile[k], 0, 0)),
                  pl.BlockSpec((1, 1, d, tf),
                               lambda k, f, tile, exp, lo, hi, first: (layer, exp[k], 0, fsel(k, f, lo, hi))),
                  pl.BlockSpec((1, 1, d, tf),
                               lambda k, f, tile, exp, lo, hi, first: (layer, exp[k], 0, fsel(k, f, lo, hi))),
                  pl.BlockSpec((1, 1, tf, d),
                               lambda k, f, tile, exp, lo, hi, first: (layer, exp[k], fsel(k, f, lo, hi), 0))],
        out_specs=pl.BlockSpec((tm, d), lambda k, f, tile, exp, lo, hi, first: (tile[k], 0)),
        scratch_shapes=[pltpu.VMEM((tm, d), F32), pltpu.VMEM((tm, d), BF16), pltpu.VMEM((tm, d), F32),
                        pltpu.VMEM((d, tf), BF16), pltpu.VMEM((d, tf), BF16), pltpu.VMEM((tf, d), BF16)],
    )
    return pl.pallas_call(
        _ffn_moe_kernel,
        out_shape=jax.ShapeDtypeStruct((p_rows, d), F32),
        grid_spec=grid_spec,
        compiler_params=_cparams("arbitrary", "arbitrary"),
        name="ffn_moe",
    )(*items, xs, wg, wu, wd)


def _tile_copy(src_ref, src_row, dst_ref, dst_row, sem):
    return pltpu.make_async_copy(src_ref.at[pl.ds(src_row, 1)], dst_ref.at[pl.ds(dst_row, 1)], sem)


def _dispatch_kernel(p0_ref, p1_ref, x_ref, xs_hbm, sem, *, rows):
    def start(r, carry):
        _tile_copy(x_ref, r, xs_hbm, p0_ref[0, 0, r], sem.at[0]).start(priority=0)
        _tile_copy(x_ref, r, xs_hbm, p1_ref[0, 0, r], sem.at[1]).start(priority=1)
        return carry

    lax.fori_loop(0, rows, start, 0, unroll=8)

    def wait(r, carry):
        _tile_copy(x_ref, r, xs_hbm, 0, sem.at[0]).wait()
        _tile_copy(x_ref, r, xs_hbm, 0, sem.at[1]).wait()
        return carry

    lax.fori_loop(0, rows, wait, 0, unroll=8)


def _dispatch(p0, p1, xt):
    nt, _, rows = p0.shape
    t = xt.shape[0]
    idx_spec = pl.BlockSpec((1, 1, rows), lambda i: (i, 0, 0), memory_space=pltpu.SMEM)
    return pl.pallas_call(
        functools.partial(_dispatch_kernel, rows=rows),
        out_shape=jax.ShapeDtypeStruct((t * TOP_K, TOKEN_TILE_ROWS, LANE), xt.dtype),
        grid=(nt,),
        in_specs=[idx_spec, idx_spec, pl.BlockSpec((rows, TOKEN_TILE_ROWS, LANE), lambda i: (i, 0, 0))],
        out_specs=pl.BlockSpec(memory_space=pl.ANY),
        scratch_shapes=[pltpu.SemaphoreType.DMA((2,))],
        compiler_params=_cparams("arbitrary"),
        name="moe_dispatch",
    )(p0, p1, xt)


def _combine_kernel(p0_ref, p1_ref, ys_hbm, x_ref, gt_ref, g_ref, b_ref, o_ref, y0_ref, y1_ref, sem, *, rows, alpha):
    def start(r, carry):
        _tile_copy(ys_hbm, p0_ref[0, 0, r], y0_ref, r, sem.at[0]).start(priority=0)
        _tile_copy(ys_hbm, p1_ref[0, 0, r], y1_ref, r, sem.at[1]).start(priority=1)
        return carry

    lax.fori_loop(0, rows, start, 0, unroll=8)

    def wait(r, carry):
        _tile_copy(ys_hbm, 0, y0_ref, r, sem.at[0]).wait()
        _tile_copy(ys_hbm, 0, y1_ref, r, sem.at[1]).wait()
        return carry

    lax.fori_loop(0, rows, wait, 0, unroll=8)
    gt = gt_ref[...]
    f = gt[:, 0:1] * y0_ref[...] + gt[:, 1:2] * y1_ref[...]
    o_ref[...] = _layer_norm(alpha * x_ref[...] + f, g_ref[...], b_ref[...])


def _combine(p0, p1, ys, x2, gates, g, b, *, alpha):
    nt, _, rows = p0.shape
    t, d = x2.shape
    const = lambda i: (0, 0)
    idx_spec = pl.BlockSpec((1, 1, rows), lambda i: (i, 0, 0), memory_space=pltpu.SMEM)
    tile_buf = pltpu.VMEM((rows, d), F32)
    return pl.pallas_call(
        functools.partial(_combine_kernel, rows=rows, alpha=alpha),
        out_shape=jax.ShapeDtypeStruct((t, d), F32),
        grid=(nt,),
        in_specs=[idx_spec, idx_spec,
                  pl.BlockSpec(memory_space=pl.ANY),
                  pl.BlockSpec((rows, d), lambda i: (i, 0)),
                  pl.BlockSpec((rows, LANE), lambda i: (i, 0)),
                  pl.BlockSpec(g.shape, const), pl.BlockSpec(b.shape, const)],
        out_specs=pl.BlockSpec((rows, d), lambda i: (i, 0)),
        scratch_shapes=[tile_buf, tile_buf, pltpu.SemaphoreType.DMA((2,))],
        compiler_params=_cparams("arbitrary"),
        name="moe_combine",
    )(p0, p1, ys, x2, gates, g, b)


def _dispatch_plan(route, n_exp):
    t = route.shape[1]
    tm = MOE_ROWS
    gates = route[:n_exp].T
    sel = route[n_exp:].T > 0.5
    seli = sel.astype(I32)
    rank = jnp.cumsum(seli, axis=0) - seli
    counts = jnp.sum(seli, axis=0)
    ends = jnp.cumsum(counts)
    starts = ends - counts
    pos = starts[None, :] + rank
    order = jnp.cumsum(seli, axis=1) - seli
    first = sel & (order == 0)
    second = sel & (order == 1)
    pick = lambda m, v: jnp.sum(jnp.where(m, v, 0), axis=1)
    p0, p1 = pick(first, pos), pick(second, pos)
    g01 = jnp.stack([pick(first, gates), pick(second, gates)], axis=1)
    g01 = jnp.pad(g01, ((0, 0), (0, LANE - TOP_K)))
    n_tiles = t * TOP_K // tm
    n_items = n_tiles + n_exp - 1
    tile_lo = jnp.arange(n_tiles, dtype=I32)[:, None] * tm
    ov_lo = jnp.maximum(starts[None, :], tile_lo)
    ov_hi = jnp.minimum(ends[None, :], tile_lo + tm)
    live = (ov_hi > ov_lo).reshape(-1)
    flat = jnp.arange(n_tiles * n_exp, dtype=I32)
    take = jnp.argsort(jnp.where(live, flat, n_tiles * n_exp))[:n_items]
    n_live = jnp.sum(live.astype(I32))
    idle = jnp.arange(n_items, dtype=I32) >= n_live
    last = take[jnp.maximum(n_live - 1, 0)]
    take = jnp.where(idle, last, take)
    item_tile = take // n_exp
    item_exp = take % n_exp
    item_lo = jnp.where(idle, 0, (ov_lo - tile_lo).reshape(-1)[take])
    item_hi = jnp.where(idle, 0, (ov_hi - tile_lo).reshape(-1)[take])
    prev_tile = jnp.concatenate([jnp.full((1,), -1, I32), item_tile[:-1]])
    item_first = (item_tile != prev_tile).astype(I32)
    items = tuple(v.astype(I32) for v in (item_tile, item_exp, item_lo, item_hi, item_first))
    shape3 = lambda v: v.astype(I32).reshape(-1, 1, GATHER_ROWS)
    return shape3(p0), shape3(p1), g01, items


def kernel(x, w_in, w_out, s5_lam_re, s5_lam_im, s5_log_dt, s5_b_re, s5_b_im, s5_c_re, s5_c_im, s5_d, s5_glu_w,
           s5_glu_b, gdn_conv_w, gdn_a_log, gdn_dt_bias, gdn_norm_w, ln1_g, ln1_b, ln2_g, ln2_b, ffn_w_gate,
           ffn_w_up, ffn_w_down, moe_router, moe_w_gate, moe_w_up, moe_w_down):
    bsz, seqlen, d = x.shape
    depth = w_in.shape[0]
    s5w = s5_glu_w.shape[1]
    gw = gdn_a_log.shape[1] * HEAD_DIM
    mw = w_out.shape[1] - s5w - gw
    n_exp = moe_router.shape[2]
    alpha = (2 * depth) ** 0.25
    row = lambda v: v.astype(F32).reshape(1, -1)

    x2 = x.reshape(bsz * seqlen, d)
    for l in range(depth):
        w, wt = _pack_inproj(w_in[l], s5w, mw, gw)
        u, mqt, mk, mvt, gq, gk, gv, gz, ab = _inproj(x2, w, wt, bsz=bsz, seqlen=seqlen, s5w=s5w, mw=mw, gw=gw)
        bblk, cblk, lam, dskip = _s5_params(s5_lam_re[l], s5_lam_im[l], s5_log_dt[l], s5_b_re[l], s5_b_im[l],
                                            s5_c_re[l], s5_c_im[l], s5_d[l])
        y_s5 = _s5(u.reshape(seqlen * bsz, s5w), bblk, cblk, lam, dskip, s5_glu_w[l].astype(BF16),
                   row(s5_glu_b[l]), bsz=bsz, seqlen=seqlen).reshape(seqlen, bsz * s5w)
        y_moba = _moba(mqt, mk, mvt, seqlen=seqlen)
        y_gdn = _gdn(gq, gk, gv, gz, ab, *_gdn_params(gdn_conv_w[l], gdn_a_log[l], gdn_dt_bias[l], gdn_norm_w[l]))
        routed = l % 2 == 1
        router_w = moe_router[l // 2].astype(F32).T if routed else None
        res = _outproj(x2, y_s5, y_moba, y_gdn, w_out[l].astype(BF16), row(ln1_g[l]), row(ln1_b[l]), router_w,
                       bsz=bsz, seqlen=seqlen, alpha=alpha, n_exp=n_exp if routed else 0)
        if routed:
            x1, route, x1t = res
            p0, p1, g01, items = _dispatch_plan(route, n_exp)
            xs = _dispatch(p0, p1, x1t)
            ys = _ffn_moe(items, xs, moe_w_gate, moe_w_up, moe_w_down, l // 2)
            x2 = _combine(p0, p1, ys, x1, g01, row(ln2_g[l]), row(ln2_b[l]), alpha=alpha)
        else:
            x2 = _ffn_dense(res, ffn_w_gate, ffn_w_up, ffn_w_down, l // 2, row(ln2_g[l]), row(ln2_b[l]), alpha=alpha)
    return x2.reshape(bsz, seqlen, d)
```

```python
import functools

import jax
import jax.numpy as jnp
from jax import lax
from jax.experimental import pallas as pl
from jax.experimental.pallas import tpu as pltpu

F32 = jnp.float32
BF16 = jnp.bfloat16
I32 = jnp.int32
HI = lax.Precision.HIGHEST

HEAD_DIM = 64
MOBA_BLOCK = 256
MOBA_TOPK = 3
GDN_CHUNK = 64
GDN_PREP_CHUNKS = 4
TOP_K = 2
LN_EPS = 1e-5
RMS_EPS = 1e-6
NEG_BIG = -1e30

V7X_VMEM_LIMIT_BYTES = 56 * 1024 * 1024
LANE = 128
TOKEN_TILE_ROWS = 8

FFN_ROWS = 1024
MOE_ROWS = 1024
MOE_SUB_ROWS = 256
FFN_COLS = 512
GATHER_ROWS = 256


def _cparams(*sem):
    return pltpu.CompilerParams(dimension_semantics=sem, vmem_limit_bytes=V7X_VMEM_LIMIT_BYTES)


def _dot(a, b, precision=None):
    return jnp.dot(a, b, preferred_element_type=F32, precision=precision)


def _dot_nt(a, b, precision=None):
    return lax.dot_general(a, b, (((1,), (1,)), ((), ())), preferred_element_type=F32, precision=precision)


def _dot_tn(a, b, precision=None):
    return lax.dot_general(a, b, (((0,), (0,)), ((), ())), preferred_element_type=F32, precision=precision)


def _iota(shape, dim):
    return lax.broadcasted_iota(I32, shape, dim)


def _layer_norm(r, g, b):
    mu = jnp.mean(r, axis=-1, keepdims=True)
    c = r - mu
    var = jnp.mean(c * c, axis=-1, keepdims=True)
    return c * lax.rsqrt(var + LN_EPS) * g + b


def _inproj_kernel(x_ref, w_ref, wt_ref, u_ref, mqt_ref, mk_ref, mvt_ref, gq_ref, gk_ref, gv_ref, gz_ref,
                   ab_ref, *, s5w, mw, gw):
    xb = x_ref[...].astype(BF16)
    nh_m = mw // HEAD_DIM

    def cols(off, width):
        return _dot(xb, w_ref[:, off:off + width])

    u_ref[...] = cols(0, s5w)
    off = s5w + mw
    kk = cols(off, mw)
    for h in range(nh_m):
        mk_ref[0, h, 0] = kk[:, h * HEAD_DIM:(h + 1) * HEAD_DIM]
    off += 2 * mw
    for ref in (gq_ref, gk_ref, gv_ref, gz_ref):
        ref[0] = cols(off, gw)
        off += gw
    ab_ref[0] = cols(off, LANE)
    qt = _dot_nt(wt_ref[0:mw, :], xb)
    vt = _dot_nt(wt_ref[mw:2 * mw, :], xb)
    for h in range(nh_m):
        mqt_ref[0, h, 0] = qt[h * HEAD_DIM:(h + 1) * HEAD_DIM, :]
        mvt_ref[0, h, 0] = vt[h * HEAD_DIM:(h + 1) * HEAD_DIM, :]


def _pack_inproj(w_in, s5w, mw, gw):
    c = w_in.shape[1]
    main = s5w + 3 * mw + 4 * gw
    w = jnp.pad(w_in, ((0, 0), (0, main + LANE - c))).astype(BF16)
    q0, v0 = s5w, s5w + 2 * mw
    wt = jnp.concatenate([w_in[:, q0:q0 + mw], w_in[:, v0:v0 + mw]], axis=1).T.astype(BF16)
    return w, wt


def _inproj(x2, w, wt, *, bsz, seqlen, s5w, mw, gw):
    tl = MOBA_BLOCK
    nb = seqlen // tl
    d = x2.shape[1]
    nh_m = mw // HEAD_DIM
    hd = HEAD_DIM
    gdn_sds = jax.ShapeDtypeStruct((bsz, seqlen, gw), F32)
    out_shape = (
        jax.ShapeDtypeStruct((seqlen, bsz * s5w), F32),
        jax.ShapeDtypeStruct((bsz, nh_m, nb, hd, tl), F32),
        jax.ShapeDtypeStruct((bsz, nh_m, nb, tl, hd), F32),
        jax.ShapeDtypeStruct((bsz, nh_m, nb, hd, tl), F32),
        gdn_sds, gdn_sds, gdn_sds, gdn_sds,
        jax.ShapeDtypeStruct((bsz, seqlen, LANE), F32),
    )
    t5 = lambda b, j: (b, 0, j, 0, 0)
    t3 = lambda b, j: (b, j, 0)
    out_specs = (
        pl.BlockSpec((tl, s5w), lambda b, j: (j, b)),
        pl.BlockSpec((1, nh_m, 1, hd, tl), t5),
        pl.BlockSpec((1, nh_m, 1, tl, hd), t5),
        pl.BlockSpec((1, nh_m, 1, hd, tl), t5),
        pl.BlockSpec((1, tl, gw), t3),
        pl.BlockSpec((1, tl, gw), t3),
        pl.BlockSpec((1, tl, gw), t3),
        pl.BlockSpec((1, tl, gw), t3),
        pl.BlockSpec((1, tl, LANE), t3),
    )
    return pl.pallas_call(
        functools.partial(_inproj_kernel, s5w=s5w, mw=mw, gw=gw),
        out_shape=out_shape,
        grid=(bsz, nb),
        in_specs=[
            pl.BlockSpec((tl, d), lambda b, j: (b * nb + j, 0)),
            pl.BlockSpec(w.shape, lambda b, j: (0, 0)),
            pl.BlockSpec(wt.shape, lambda b, j: (0, 0)),
        ],
        out_specs=out_specs,
        compiler_params=_cparams("parallel", "arbitrary"),
        name="inproj",
    )(x2, w, wt)


def _s5_kernel(u_ref, bblk_ref, cblk_ref, lam_ref, d_ref, gw_ref, gb_ref, y_ref, st_ref, h_ref, *, bsz, tt, ns):
    @pl.when(pl.program_id(0) == 0)
    def _():
        h_ref[...] = jnp.zeros_like(h_ref)

    u = u_ref[...]
    st_ref[...] = _dot(u.astype(BF16), bblk_ref[...])
    lam_re = jnp.broadcast_to(lam_ref[0:1, :], (bsz, ns))
    lam_im = jnp.broadcast_to(lam_ref[1:2, :], (bsz, ns))

    def step(t, carry):
        h_re, h_im = carry
        r = pl.multiple_of(t * bsz, bsz)
        n_re = lam_re * h_re - lam_im * h_im + st_ref[pl.ds(r, bsz), 0:ns]
        n_im = lam_re * h_im + lam_im * h_re + st_ref[pl.ds(r, bsz), ns:2 * ns]
        st_ref[pl.ds(r, bsz), 0:ns] = n_re
        st_ref[pl.ds(r, bsz), ns:2 * ns] = n_im
        return n_re, n_im

    h_re, h_im = lax.fori_loop(0, tt, step, (h_ref[0], h_ref[1]), unroll=8)
    h_ref[0] = h_re
    h_ref[1] = h_im
    y = _dot(st_ref[...].astype(BF16), cblk_ref[...]) + d_ref[...] * u
    y = jax.nn.gelu(y)
    y_ref[...] = y * jax.nn.sigmoid(_dot(y.astype(BF16), gw_ref[...]) + gb_ref[...])


def _s5(u2, bblk, cblk, lam, d, glu_w, glu_b, *, bsz, seqlen):
    width = u2.shape[1]
    ns = lam.shape[1]
    tt = 64
    rows = tt * bsz
    const = lambda i: (0, 0)
    return pl.pallas_call(
        functools.partial(_s5_kernel, bsz=bsz, tt=tt, ns=ns),
        out_shape=jax.ShapeDtypeStruct(u2.shape, F32),
        grid=(seqlen // tt,),
        in_specs=[
            pl.BlockSpec((rows, width), lambda i: (i, 0)),
            pl.BlockSpec(bblk.shape, const),
            pl.BlockSpec(cblk.shape, const),
            pl.BlockSpec(lam.shape, const),
            pl.BlockSpec(d.shape, const),
            pl.BlockSpec(glu_w.shape, const),
            pl.BlockSpec(glu_b.shape, const),
        ],
        out_specs=pl.BlockSpec((rows, width), lambda i: (i, 0)),
        scratch_shapes=[pltpu.VMEM((rows, 2 * ns), F32), pltpu.VMEM((2, bsz, ns), F32)],
        compiler_params=_cparams("arbitrary"),
        name="s5",
    )(u2, bblk, cblk, lam, d, glu_w, glu_b)


def _s5_params(lam_re, lam_im, log_dt, b_re, b_im, c_re, c_im, d_skip):
    g, n = lam_re.shape
    p = b_re.shape[-1]
    lam = lax.complex(lam_re.astype(F32), lam_im.astype(F32))
    step = jnp.exp(log_dt.astype(F32))[:, None]
    lam_bar = jnp.exp(lam * step)
    b_bar = ((lam_bar - 1.0) / lam)[..., None] * lax.complex(b_re.astype(F32), b_im.astype(F32))
    eye = jnp.eye(g, dtype=F32)
    b_re_blk = jnp.einsum('gnp,gh->gphn', b_bar.real, eye).reshape(g * p, g * n)
    b_im_blk = jnp.einsum('gnp,gh->gphn', b_bar.imag, eye).reshape(g * p, g * n)
    bblk = jnp.concatenate([b_re_blk, b_im_blk], axis=1)
    c_re_blk = jnp.einsum('gpn,gh->gnhp', c_re.astype(F32), eye).reshape(g * n, g * p)
    c_im_blk = jnp.einsum('gpn,gh->gnhp', c_im.astype(F32), eye).reshape(g * n, g * p)
    cblk = jnp.concatenate([c_re_blk, -c_im_blk], axis=0)
    lam2 = jnp.stack([lam_bar.real.reshape(g * n), lam_bar.imag.reshape(g * n)])
    return bblk.astype(BF16), cblk.astype(BF16), lam2, d_skip.astype(F32).reshape(1, g * p)


def _moba_kernel(qt_ref, k_ref, vt_ref, o_ref, km_ref, qs_ref, bias_ref, acc_ref, *, nh, nb):
    i = pl.program_id(1)
    blk = MOBA_BLOCK
    hd = HEAD_DIM

    @pl.when(i == 0)
    def _():
        avg = jnp.full((1, blk), 1.0 / blk, F32)
        for h in range(nh):
            for n in range(nb):
                km_ref[h, n:n + 1, :] = _dot(avg, k_ref[0, h, n], precision=HI)

    n_iota = _iota((nb, blk), 0)
    kpos = _iota((blk, blk), 0)
    qpos = _iota((blk, blk), 1)
    causal = kpos <= qpos

    heads = range(nh)

    qts = [qt_ref[0, h, 0] * (hd ** -0.5) for h in heads]
    qtb = [qts[h].astype(BF16) for h in heads]
    s_own = [_dot(k_ref[0, h, i].astype(BF16), qtb[h]) for h in heads]
    gates = [_dot(km_ref[h], qts[h], precision=HI) for h in heads]
    vt_own = [vt_ref[0, h, i].astype(BF16) for h in heads]
    biases = []
    for h in heads:
        gate = jnp.where(n_iota < i, gates[h], -jnp.inf)
        rank = jnp.zeros((nb, blk), I32)
        for m in range(nb):
            gm = gate[m:m + 1, :]
            ahead = (gm > gate) | ((gm == gate) & (m < n_iota))
            rank = rank + ahead.astype(I32)
        biases.append(jnp.where((n_iota < i) & (rank < MOBA_TOPK), 0.0, NEG_BIG))
    m_own, l_own, acc_own = [], [], []
    for h in heads:
        s = jnp.where(causal, s_own[h], NEG_BIG)
        m0 = jnp.max(s, axis=0, keepdims=True)
        p = jnp.exp(s - m0)
        m_own.append(m0)
        l_own.append(jnp.sum(p, axis=0, keepdims=True))
        acc_own.append(_dot(vt_own[h], p.astype(BF16)))
    for h in heads:
        bias_ref[h] = biases[h]
        qs_ref[h] = qtb[h]
        acc_ref[h * hd:(h + 1) * hd, :] = acc_own[h]

    def body(j, carry):
        ms, ls = carry
        ss = [_dot(k_ref[0, h, j].astype(BF16), qs_ref[h]) + bias_ref[h, pl.ds(j, 1), :] for h in heads]
        vts = [vt_ref[0, h, j].astype(BF16) for h in heads]
        accs = [acc_ref[h * hd:(h + 1) * hd, :] for h in heads]
        new_ms, new_ls, new_accs = [], [], []
        for h in heads:
            m_new = jnp.maximum(ms[h], jnp.max(ss[h], axis=0, keepdims=True))
            p = jnp.exp(ss[h] - m_new)
            corr = jnp.exp(ms[h] - m_new)
            new_ms.append(m_new)
            new_ls.append(ls[h] * corr + jnp.sum(p, axis=0, keepdims=True))
            new_accs.append(accs[h] * corr + _dot(vts[h], p.astype(BF16)))
        for h in heads:
            acc_ref[h * hd:(h + 1) * hd, :] = new_accs[h]
        return tuple(new_ms), tuple(new_ls)

    _, ls = lax.fori_loop(0, i, body, (tuple(m_own), tuple(l_own)))
    for h in heads:
        rows = slice(h * hd, (h + 1) * hd)
        acc_ref[rows, :] = acc_ref[rows, :] / ls[h]
    eye = (kpos == qpos).astype(F32)
    o_ref[0] = _dot_nt(eye, acc_ref[...], precision=HI)


def _moba(qt, k, vt, *, seqlen):
    bsz, nh, nb, hd, blk = qt.shape
    return pl.pallas_call(
        functools.partial(_moba_kernel, nh=nh, nb=nb),
        out_shape=jax.ShapeDtypeStruct((bsz, seqlen, nh * hd), F32),
        grid=(bsz, nb),
        in_specs=[
            pl.BlockSpec((1, nh, 1, hd, blk), lambda b, i: (b, 0, i, 0, 0)),
            pl.BlockSpec((1, nh, nb, blk, hd), lambda b, i: (b, 0, 0, 0, 0)),
            pl.BlockSpec((1, nh, nb, hd, blk), lambda b, i: (b, 0, 0, 0, 0)),
        ],
        out_specs=pl.BlockSpec((1, blk, nh * hd), lambda b, i: (b, i, 0)),
        scratch_shapes=[pltpu.VMEM((nh, nb, hd), F32), pltpu.VMEM((nh, hd, blk), BF16),
                        pltpu.VMEM((nh, nb, blk), F32), pltpu.VMEM((nh * hd, blk), F32)],
        compiler_params=_cparams("parallel", "arbitrary"),
        name="moba",
    )(qt, k, vt)


def _gdn_kernel(q_ref, k_ref, v_ref, z_ref, ab_ref, cw_ref, al_ref, dtb_ref, nw_ref, o_ref,
                g_ref, beta_ref, us_ref, ws_ref, qe_ref, kd_ref, at_ref, el_ref, s_ref, *, nh, seqlen):
    npair = nh // 2
    ch = GDN_CHUNK
    hd = HEAD_DIM
    gw = nh * hd
    pw = 2 * hd
    pairs = range(npair)

    rr = _iota((pw, pw), 0)
    cc = _iota((pw, pw), 1)
    same_head = (rr // hd) == (cc // hd)
    mask_incl = same_head & ((rr % ch) >= (cc % ch))
    mask_strict = same_head & ((rr % ch) > (cc % ch))
    eye = (rr == cc).astype(F32)
    tril_c = (_iota((ch, ch), 0) >= _iota((ch, ch), 1)).astype(F32)
    lane_c = _iota((ch, LANE), 1)
    first = lane_c < hd
    row_col = _iota((pw, 1), 0)
    lane_row = _iota((1, pw), 1)

    def stack(x):
        return jnp.concatenate([jnp.where(first, x, 0.0), jnp.where(first, 0.0, x)], axis=0)

    def fold(x):
        return x[:ch] + x[ch:]

    def head_sumsq(y):
        y2 = y * y
        sa = jnp.sum(jnp.where(first, y2, 0.0), axis=1, keepdims=True)
        sb = jnp.sum(jnp.where(first, 0.0, y2), axis=1, keepdims=True)
        return jnp.where(first, sa, sb)

    a = ab_ref[0]
    ea = a + dtb_ref[...]
    g_ref[...] = -jnp.exp(al_ref[...]) * (jnp.maximum(ea, 0.0) + jnp.log1p(jnp.exp(-jnp.abs(ea))))
    beta_ref[...] = jax.nn.sigmoid(a)

    def conv_silu(src, t, p, c, r0):
        lanes = slice(p * pw, (p + 1) * pw)
        h0 = pl.multiple_of(jnp.maximum(r0 - 8, 0), 8)
        hist = jnp.where(c > 0, src[0, pl.ds(h0, 8), lanes], 0.0)
        x = jnp.concatenate([hist, src[0, pl.ds(r0, ch), lanes]], axis=0)
        c0 = t * gw + p * pw
        acc = x * cw_ref[3:4, c0:c0 + pw]
        for tap in range(3):
            acc = acc + pltpu.roll(x, 3 - tap, 0) * cw_ref[tap:tap + 1, c0:c0 + pw]
        y = acc[8:]
        return y * jax.nn.sigmoid(y)

    def prep(cc, carry):
        chunks = [cc * GDN_PREP_CHUNKS + u for u in range(GDN_PREP_CHUNKS)]
        r0s = [pl.multiple_of(c * ch, ch) for c in chunks]
        gcs = [_dot(tril_c, g_ref[pl.ds(r0, ch), :], precision=HI) for r0 in r0s]
        bch = [beta_ref[pl.ds(r0, ch), :] for r0 in r0s]
        streams = [(u, p) for u in range(GDN_PREP_CHUNKS) for p in pairs]
        ns = range(len(streams))
        qs, ks, vs, gcol, bcol, decay = [], [], [], [], [], []
        for u, p in streams:
            q = conv_silu(q_ref, 0, p, chunks[u], r0s[u])
            k = conv_silu(k_ref, 1, p, chunks[u], r0s[u])
            v = conv_silu(v_ref, 2, p, chunks[u], r0s[u])
            q = q * (lax.rsqrt(head_sumsq(q) + RMS_EPS) * (hd ** -0.5))
            k = k * lax.rsqrt(head_sumsq(k) + RMS_EPS)
            qs.append(stack(q))
            ks.append(stack(k))
            vs.append(stack(v))
            ha, hb = 2 * p, 2 * p + 1
            gst = jnp.concatenate([jnp.where(lane_c == ha, gcs[u], 0.0), jnp.where(lane_c == hb, gcs[u], 0.0)], axis=0)
            bst = jnp.concatenate([jnp.where(lane_c == nh + ha, bch[u], 0.0),
                                   jnp.where(lane_c == nh + hb, bch[u], 0.0)], axis=0)
            gc = jnp.sum(gst, axis=1, keepdims=True)
            gcol.append(gc)
            bcol.append(jnp.sum(bst, axis=1, keepdims=True))
            gmat = jnp.broadcast_to(gc, (pw, pw))
            decay.append(jnp.where(mask_incl, jnp.exp(jnp.where(mask_incl, gmat - gmat.T, 0.0)), 0.0))
        kbs = [ks[n] * bcol[n] for n in ns]
        ksb = [ks[n].astype(BF16) for n in ns]
        pm = [jnp.where(mask_strict, _dot_nt(kbs[n].astype(BF16), ksb[n]) * decay[n], 0.0) for n in ns]
        tm = [eye - pm[n] for n in ns]
        pmb = [pm[n].astype(BF16) for n in ns]
        pm = [_dot(pmb[n], pmb[n]) for n in ns]
        for _ in range(4):
            pmb = [pm[n].astype(BF16) for n in ns]
            tm = [tm[n] + _dot(tm[n].astype(BF16), pmb[n]) for n in ns]
            pm = [_dot(pmb[n], pmb[n]) for n in ns]
        tm = [tm[n] + _dot(tm[n].astype(BF16), pm[n].astype(BF16)) for n in ns]
        out = []
        for n in ns:
            eg = jnp.exp(gcol[n])
            rhs = jnp.concatenate([vs[n] * bcol[n], kbs[n] * eg], axis=1).astype(BF16)
            sol = _dot(tm[n].astype(BF16), rhs)
            attn = jnp.where(mask_incl, _dot_nt(qs[n].astype(BF16), ksb[n]) * decay[n], 0.0)
            g_a = gcol[n][ch - 1:ch]
            g_b = gcol[n][2 * ch - 1:2 * ch]
            glast = jnp.where(row_col < ch, g_a, g_b)
            e_last = jnp.where(lane_row < hd, jnp.exp(g_a), jnp.exp(g_b))
            out.append((fold(sol[:, :pw]), fold(sol[:, pw:]).astype(BF16), fold(qs[n] * eg).astype(BF16),
                        fold(ks[n] * jnp.exp(glast - gcol[n])).astype(BF16), fold(attn).astype(BF16),
                        jnp.broadcast_to(e_last, (8, pw))))
        for n, (u, p) in enumerate(streams):
            rows = pl.ds(r0s[u], ch)
            us_ref[p, rows, :], ws_ref[p, rows, :], qe_ref[p, rows, :], kd_ref[p, rows, :], at_ref[p, rows, :] = out[n][:5]
            el_ref[p, pl.ds(pl.multiple_of(chunks[u] * 8, 8), 8), :] = out[n][5]
        return carry

    lax.fori_loop(0, seqlen // (ch * GDN_PREP_CHUNKS), prep, 0)

    s_ref[...] = jnp.zeros_like(s_ref)

    def scan(c, carry):
        r0 = pl.multiple_of(c * ch, ch)
        rows = pl.ds(r0, ch)
        lanes = [slice(p * pw, (p + 1) * pw) for p in pairs]
        states = [s_ref[p] for p in pairs]
        wq = [jnp.concatenate([ws_ref[p, rows, :], qe_ref[p, rows, :]], axis=0) for p in pairs]
        us = [us_ref[p, rows, :] for p in pairs]
        at = [at_ref[p, rows, :] for p in pairs]
        kd = [kd_ref[p, rows, :] for p in pairs]
        el = [el_ref[p, pl.ds(c * 8, 1), :] for p in pairs]
        zs = [z_ref[0, rows, lanes[p]] for p in pairs]
        wqs = [_dot(wq[p], states[p].astype(BF16)) for p in pairs]
        v_new = [us[p] - wqs[p][:ch] for p in pairs]
        upd = [_dot_tn(kd[p], v_new[p].astype(BF16)) for p in pairs]
        new_states = [states[p] * el[p] + jnp.where(same_head, upd[p], 0.0) for p in pairs]
        outs = []
        for p in pairs:
            o = wqs[p][ch:] + _dot(at[p], stack(v_new[p]).astype(BF16))
            ms = head_sumsq(o) * (1.0 / hd)
            outs.append(o * lax.rsqrt(ms + RMS_EPS) * nw_ref[...] * (zs[p] * jax.nn.sigmoid(zs[p])))
        for p in pairs:
            s_ref[p] = new_states[p]
            o_ref[0, rows, lanes[p]] = outs[p]
        return carry

    lax.fori_loop(0, seqlen // ch, scan, 0)


def _gdn_params(conv_w, a_log, dt_bias, norm_w):
    nh = a_log.shape[0]
    lane_pad = lambda v: jnp.pad(v.astype(F32), (0, LANE - nh)).reshape(1, LANE)
    return conv_w.astype(F32), lane_pad(a_log), lane_pad(dt_bias), jnp.tile(norm_w.astype(F32), 2).reshape(1, LANE)


def _gdn(q, k, v, z, ab, conv_w, al, dtb, nw):
    bsz, seqlen, gw = q.shape
    nh = gw // HEAD_DIM
    npair = nh // 2
    seq_spec = pl.BlockSpec((1, seqlen, gw), lambda b: (b, 0, 0))
    const = lambda b: (0, 0)
    return pl.pallas_call(
        functools.partial(_gdn_kernel, nh=nh, seqlen=seqlen),
        out_shape=jax.ShapeDtypeStruct((bsz, seqlen, gw), F32),
        grid=(bsz,),
        in_specs=[seq_spec, seq_spec, seq_spec, seq_spec,
                  pl.BlockSpec((1, seqlen, LANE), lambda b: (b, 0, 0)),
                  pl.BlockSpec(conv_w.shape, const),
                  pl.BlockSpec(al.shape, const),
                  pl.BlockSpec(dtb.shape, const),
                  pl.BlockSpec(nw.shape, const)],
        out_specs=seq_spec,
        scratch_shapes=[pltpu.VMEM((seqlen, LANE), F32)] * 2
        + [pltpu.VMEM((npair, seqlen, LANE), F32)]
        + [pltpu.VMEM((npair, seqlen, LANE), BF16)] * 4
        + [pltpu.VMEM((npair, seqlen // GDN_CHUNK * 8, LANE), F32)]
        + [pltpu.VMEM((npair, LANE, LANE), F32)],
        compiler_params=_cparams("parallel"),
        name="gdn",
    )(q, k, v, z, ab, conv_w, al, dtb, nw)


def _route(logits):
    n_exp = logits.shape[0]
    eidx = _iota(logits.shape, 0)
    m1 = jnp.max(logits, axis=0, keepdims=True)
    i1 = jnp.min(jnp.where(logits == m1, eidx, n_exp), axis=0, keepdims=True)
    lg2 = jnp.where(eidx == i1, -jnp.inf, logits)
    m2 = jnp.max(lg2, axis=0, keepdims=True)
    i2 = jnp.min(jnp.where(lg2 == m2, eidx, n_exp), axis=0, keepdims=True)
    e2 = jnp.exp(m2 - m1)
    g1 = 1.0 / (1.0 + e2)
    g2 = e2 / (1.0 + e2)
    gates = jnp.where(eidx == i1, g1, 0.0) + jnp.where(eidx == i2, g2, 0.0)
    mask = jnp.where((eidx == i1) | (eidx == i2), 1.0, 0.0)
    return jnp.concatenate([gates, mask], axis=0)


def _outproj_kernel(x_ref, ys_ref, ym_ref, yg_ref, w_ref, g_ref, b_ref, *rest, alpha, s5w, mw, n_exp):
    mix = _dot(ys_ref[...].astype(BF16), w_ref[0:s5w, :])
    mix = mix + _dot(ym_ref[0].astype(BF16), w_ref[s5w:s5w + mw, :])
    mix = mix + _dot(yg_ref[0].astype(BF16), w_ref[s5w + mw:, :])
    x1 = _layer_norm(alpha * x_ref[...] + mix, g_ref[...], b_ref[...])
    if n_exp:
        rw_ref, o_ref, r_ref, ot_ref = rest
        o_ref[...] = x1
        r_ref[...] = _route(_dot_nt(rw_ref[...], x1, precision=HI))
        for s in range(TOKEN_TILE_ROWS):
            ot_ref[:, s, :] = x1[:, s * LANE:(s + 1) * LANE]
    else:
        (o_ref,) = rest
        o_ref[...] = x1


def _outproj(x2, ys, ym, yg, w, g, b, router_w, *, bsz, seqlen, alpha, n_exp):
    tl = 512
    nt = seqlen // tl
    t, d = x2.shape
    s5w = ys.shape[1] // bsz
    mw = ym.shape[2]
    gw = yg.shape[2]
    const = lambda bi, j: (0, 0)
    row_spec = pl.BlockSpec((tl, d), lambda bi, j: (bi * nt + j, 0))
    in_specs = [row_spec,
                pl.BlockSpec((tl, s5w), lambda bi, j: (j, bi)),
                pl.BlockSpec((1, tl, mw), lambda bi, j: (bi, j, 0)),
                pl.BlockSpec((1, tl, gw), lambda bi, j: (bi, j, 0)),
                pl.BlockSpec(w.shape, const), pl.BlockSpec(g.shape, const), pl.BlockSpec(b.shape, const)]
    args = [x2, ys, ym, yg, w, g, b]
    out_shape = jax.ShapeDtypeStruct((t, d), F32)
    out_specs = row_spec
    if n_exp:
        in_specs.append(pl.BlockSpec(router_w.shape, const))
        args.append(router_w)
        assert d == TOKEN_TILE_ROWS * LANE
        out_shape = (out_shape, jax.ShapeDtypeStruct((2 * n_exp, t), F32),
                     jax.ShapeDtypeStruct((t, TOKEN_TILE_ROWS, LANE), F32))
        out_specs = (row_spec, pl.BlockSpec((2 * n_exp, tl), lambda bi, j: (0, bi * nt + j)),
                     pl.BlockSpec((tl, TOKEN_TILE_ROWS, LANE), lambda bi, j: (bi * nt + j, 0, 0)))
    return pl.pallas_call(
        functools.partial(_outproj_kernel, alpha=alpha, s5w=s5w, mw=mw, n_exp=n_exp),
        out_shape=out_shape,
        grid=(bsz, nt),
        in_specs=in_specs,
        out_specs=out_specs,
        compiler_params=_cparams("parallel", "arbitrary"),
        name="outproj_route" if n_exp else "outproj",
    )(*args)


def _swiglu_part(xb, wg, wu, wd):
    hg = _dot(xb, wg)
    hu = _dot(xb, wu)
    h = (hg * jax.nn.sigmoid(hg)) * hu
    return _dot(h.astype(BF16), wd)


def _ffn_dense_kernel(x_ref, wg_ref, wu_ref, wd_ref, g_ref, b_ref, o_ref, xb_ref, acc_ref, *, alpha):
    @pl.when(pl.program_id(1) == 0)
    def _():
        xb_ref[...] = x_ref[...].astype(BF16)
        acc_ref[...] = jnp.zeros_like(acc_ref)

    acc_ref[...] += _swiglu_part(xb_ref[...], wg_ref[0].astype(BF16), wu_ref[0].astype(BF16),
                                 wd_ref[0].astype(BF16))

    @pl.when(pl.program_id(1) == pl.num_programs(1) - 1)
    def _():
        o_ref[...] = _layer_norm(alpha * x_ref[...] + acc_ref[...], g_ref[...], b_ref[...])


def _ffn_dense(x2, wg, wu, wd, layer, g, b, *, alpha):
    t, d = x2.shape
    ff = wg.shape[2]
    tm, tf = FFN_ROWS, FFN_COLS
    const = lambda i, f: (0, 0)
    return pl.pallas_call(
        functools.partial(_ffn_dense_kernel, alpha=alpha),
        out_shape=jax.ShapeDtypeStruct((t, d), F32),
        grid=(t // tm, ff // tf),
        in_specs=[pl.BlockSpec((tm, d), lambda i, f: (i, 0)),
                  pl.BlockSpec((1, d, tf), lambda i, f: (layer, 0, f)),
                  pl.BlockSpec((1, d, tf), lambda i, f: (layer, 0, f)),
                  pl.BlockSpec((1, tf, d), lambda i, f: (layer, f, 0)),
                  pl.BlockSpec(g.shape, const), pl.BlockSpec(b.shape, const)],
        out_specs=pl.BlockSpec((tm, d), lambda i, f: (i, 0)),
        scratch_shapes=[pltpu.VMEM((tm, d), BF16), pltpu.VMEM((tm, d), F32)],
        compiler_params=_cparams("parallel", "arbitrary"),
        name="ffn_dense",
    )(x2, wg, wu, wd, g, b)


def _ffn_moe_kernel(tile_ref, exp_ref, lo_ref, hi_ref, first_ref, x_ref, wg_ref, wu_ref, wd_ref, o_ref,
                    xt_ref, xb_ref, acc_ref, wgb_ref, wub_ref, wdb_ref):
    k = pl.program_id(0)
    f = pl.program_id(1)
    lo, hi = lo_ref[k], hi_ref[k]
    sub = MOE_SUB_ROWS
    d = o_ref.shape[1]

    def each_live_sub_block(fn):
        for sb in range(o_ref.shape[0] // sub):
            pl.when((lo < (sb + 1) * sub) & (hi > sb * sub))(functools.partial(fn, sb, slice(sb * sub, (sb + 1) * sub)))

    @pl.when(hi > lo)
    def _():
        @pl.when((f == 0) & (first_ref[k] > 0))
        def _():
            o_ref[...] = jnp.zeros_like(o_ref)

        def load_rows(sb, rows):
            for s in range(TOKEN_TILE_ROWS):
                xt_ref[rows, s * LANE:(s + 1) * LANE] = x_ref[rows, s, :]
            xb_ref[rows, :] = xt_ref[rows, :].astype(BF16)
            acc_ref[rows, :] = jnp.zeros((sub, d), F32)

        pl.when(f == 0)(functools.partial(each_live_sub_block, load_rows))

        wgb_ref[...] = wg_ref[0, 0].astype(BF16)
        wub_ref[...] = wu_ref[0, 0].astype(BF16)
        wdb_ref[...] = wd_ref[0, 0].astype(BF16)

        def compute(sb, rows):
            acc_ref[rows, :] += _swiglu_part(xb_ref[rows, :], wgb_ref[...], wub_ref[...], wdb_ref[...])

        whole = (lo == 0) & (hi == o_ref.shape[0])

        @pl.when(whole)
        def _():
            acc_ref[...] += _swiglu_part(xb_ref[...], wgb_ref[...], wub_ref[...], wdb_ref[...])

        pl.when(jnp.logical_not(whole))(functools.partial(each_live_sub_block, compute))

        def emit(sb, rows):
            row = _iota((sub, d), 0) + sb * sub
            o_ref[rows, :] = jnp.where((row >= lo) & (row < hi), acc_ref[rows, :], o_ref[rows, :])

        pl.when(f == pl.num_programs(1) - 1)(functools.partial(each_live_sub_block, emit))


def _ffn_moe(items, xs, wg, wu, wd, layer):
    p_rows = xs.shape[0]
    d = wg.shape[2]
    ff = wg.shape[3]
    tm, tf = MOE_ROWS, FFN_COLS
    nf = ff // tf
    n_items = items[0].shape[0]

    def fsel(k, f, lo, hi):
        return jnp.where(hi[k] > lo[k], f, nf - 1)

    grid_spec = pltpu.PrefetchScalarGridSpec(
        num_scalar_prefetch=5,
        grid=(n_items, nf),
        in_specs=[pl.BlockSpec((tm, TOKEN_TILE_ROWS, LANE), lambda k, f, tile, exp, lo, hi, first: (tile[k], 0, 0)),
                  pl.BlockSpec((1, 1, d, tf),
                               lambda k, f, tile, exp, lo, hi, first: (layer, exp[k], 0, fsel(k, f, lo, hi))),
                  pl.BlockSpec((1, 1, d, tf),
                               lambda k, f, tile, exp, lo, hi, first: (layer, exp[k], 0, fsel(k, f, lo, hi))),
                  pl.BlockSpec((1, 1, tf, d),
                               lambda k, f, tile, exp, lo, hi, first: (layer, exp[k], fsel(k, f, lo, hi), 0))],
        out_specs=pl.BlockSpec((tm, d), lambda k, f, tile, exp, lo, hi, first: (tile[k], 0)),
        scratch_shapes=[pltpu.VMEM((tm, d), F32), pltpu.VMEM((tm, d), BF16), pltpu.VMEM((tm, d), F32),
                        pltpu.VMEM((d, tf), BF16), pltpu.VMEM((d, tf), BF16), pltpu.VMEM((tf, d), BF16)],
    )
    return pl.pallas_call(
        _ffn_moe_kernel,
        out_shape=jax.ShapeDtypeStruct((p_rows, d), F32),
        grid_spec=grid_spec,
        compiler_params=_cparams("arbitrary", "arbitrary"),
        name="ffn_moe",
    )(*items, xs, wg, wu, wd)


def _tile_copy(src_ref, src_row, dst_ref, dst_row, sem):
    return pltpu.make_async_copy(src_ref.at[pl.ds(src_row, 1)], dst_ref.at[pl.ds(dst_row, 1)], sem)


def _dispatch_kernel(p0_ref, p1_ref, x_ref, xs_hbm, sem, *, rows):
    def start(r, carry):
        _tile_copy(x_ref, r, xs_hbm, p0_ref[0, 0, r], sem.at[0]).start(priority=0)
        _tile_copy(x_ref, r, xs_hbm, p1_ref[0, 0, r], sem.at[1]).start(priority=1)
        return carry

    lax.fori_loop(0, rows, start, 0, unroll=8)

    def wait(r, carry):
        _tile_copy(x_ref, r, xs_hbm, 0, sem.at[0]).wait()
        _tile_copy(x_ref, r, xs_hbm, 0, sem.at[1]).wait()
        return carry

    lax.fori_loop(0, rows, wait, 0, unroll=8)


def _dispatch(p0, p1, xt):
    nt, _, rows = p0.shape
    t = xt.shape[0]
    idx_spec = pl.BlockSpec((1, 1, rows), lambda i: (i, 0, 0), memory_space=pltpu.SMEM)
    return pl.pallas_call(
        functools.partial(_dispatch_kernel, rows=rows),
        out_shape=jax.ShapeDtypeStruct((t * TOP_K, TOKEN_TILE_ROWS, LANE), xt.dtype),
        grid=(nt,),
        in_specs=[idx_spec, idx_spec, pl.BlockSpec((rows, TOKEN_TILE_ROWS, LANE), lambda i: (i, 0, 0))],
        out_specs=pl.BlockSpec(memory_space=pl.ANY),
        scratch_shapes=[pltpu.SemaphoreType.DMA((2,))],
        compiler_params=_cparams("arbitrary"),
        name="moe_dispatch",
    )(p0, p1, xt)


def _combine_kernel(p0_ref, p1_ref, ys_hbm, x_ref, gt_ref, g_ref, b_ref, o_ref, y0_ref, y1_ref, sem, *, rows, alpha):
    def start(r, carry):
        _tile_copy(ys_hbm, p0_ref[0, 0, r], y0_ref, r, sem.at[0]).start(priority=0)
        _tile_copy(ys_hbm, p1_ref[0, 0, r], y1_ref, r, sem.at[1]).start(priority=1)
        return carry

    lax.fori_loop(0, rows, start, 0, unroll=8)

    def wait(r, carry):
        _tile_copy(ys_hbm, 0, y0_ref, r, sem.at[0]).wait()
        _tile_copy(ys_hbm, 0, y1_ref, r, sem.at[1]).wait()
        return carry

    lax.fori_loop(0, rows, wait, 0, unroll=8)
    gt = gt_ref[...]
    f = gt[:, 0:1] * y0_ref[...] + gt[:, 1:2] * y1_ref[...]
    o_ref[...] = _layer_norm(alpha * x_ref[...] + f, g_ref[...], b_ref[...])


def _combine(p0, p1, ys, x2, gates, g, b, *, alpha):
    nt, _, rows = p0.shape
    t, d = x2.shape
    const = lambda i: (0, 0)
    idx_spec = pl.BlockSpec((1, 1, rows), lambda i: (i, 0, 0), memory_space=pltpu.SMEM)
    tile_buf = pltpu.VMEM((rows, d), F32)
    return pl.pallas_call(
        functools.partial(_combine_kernel, rows=rows, alpha=alpha),
        out_shape=jax.ShapeDtypeStruct((t, d), F32),
        grid=(nt,),
        in_specs=[idx_spec, idx_spec,
                  pl.BlockSpec(memory_space=pl.ANY),
                  pl.BlockSpec((rows, d), lambda i: (i, 0)),
                  pl.BlockSpec((rows, LANE), lambda i: (i, 0)),
                  pl.BlockSpec(g.shape, const), pl.BlockSpec(b.shape, const)],
        out_specs=pl.BlockSpec((rows, d), lambda i: (i, 0)),
        scratch_shapes=[tile_buf, tile_buf, pltpu.SemaphoreType.DMA((2,))],
        compiler_params=_cparams("arbitrary"),
        name="moe_combine",
    )(p0, p1, ys, x2, gates, g, b)


def _dispatch_plan(route, n_exp):
    t = route.shape[1]
    tm = MOE_ROWS
    gates = route[:n_exp].T
    sel = route[n_exp:].T > 0.5
    seli = sel.astype(I32)
    rank = jnp.cumsum(seli, axis=0) - seli
    counts = jnp.sum(seli, axis=0)
    ends = jnp.cumsum(counts)
    starts = ends - counts
    pos = starts[None, :] + rank
    order = jnp.cumsum(seli, axis=1) - seli
    first = sel & (order == 0)
    second = sel & (order == 1)
    pick = lambda m, v: jnp.sum(jnp.where(m, v, 0), axis=1)
    p0, p1 = pick(first, pos), pick(second, pos)
    g01 = jnp.stack([pick(first, gates), pick(second, gates)], axis=1)
    g01 = jnp.pad(g01, ((0, 0), (0, LANE - TOP_K)))
    n_tiles = t * TOP_K // tm
    n_items = n_tiles + n_exp - 1
    tile_lo = jnp.arange(n_tiles, dtype=I32)[:, None] * tm
    ov_lo = jnp.maximum(starts[None, :], tile_lo)
    ov_hi = jnp.minimum(ends[None, :], tile_lo + tm)
    live = (ov_hi > ov_lo).reshape(-1)
    flat = jnp.arange(n_tiles * n_exp, dtype=I32)
    take = jnp.argsort(jnp.where(live, flat, n_tiles * n_exp))[:n_items]
    n_live = jnp.sum(live.astype(I32))
    idle = jnp.arange(n_items, dtype=I32) >= n_live
    last = take[jnp.maximum(n_live - 1, 0)]
    take = jnp.where(idle, last, take)
    item_tile = take // n_exp
    item_exp = take % n_exp
    item_lo = jnp.where(idle, 0, (ov_lo - tile_lo).reshape(-1)[take])
    item_hi = jnp.where(idle, 0, (ov_hi - tile_lo).reshape(-1)[take])
    prev_tile = jnp.concatenate([jnp.full((1,), -1, I32), item_tile[:-1]])
    item_first = (item_tile != prev_tile).astype(I32)
    items = tuple(v.astype(I32) for v in (item_tile, item_exp, item_lo, item_hi, item_first))
    shape3 = lambda v: v.astype(I32).reshape(-1, 1, GATHER_ROWS)
    return shape3(p0), shape3(p1), g01, items


def kernel(x, w_in, w_out, s5_lam_re, s5_lam_im, s5_log_dt, s5_b_re, s5_b_im, s5_c_re, s5_c_im, s5_d, s5_glu_w,
           s5_glu_b, gdn_conv_w, gdn_a_log, gdn_dt_bias, gdn_norm_w, ln1_g, ln1_b, ln2_g, ln2_b, ffn_w_gate,
           ffn_w_up, ffn_w_down, moe_router, moe_w_gate, moe_w_up, moe_w_down):
    bsz, seqlen, d = x.shape
    depth = w_in.shape[0]
    s5w = s5_glu_w.shape[1]
    gw = gdn_a_log.shape[1] * HEAD_DIM
    mw = w_out.shape[1] - s5w - gw
    n_exp = moe_router.shape[2]
    alpha = (2 * depth) ** 0.25
    row = lambda v: v.astype(F32).reshape(1, -1)

    x2 = x.reshape(bsz * seqlen, d)
    for l in range(depth):
        w, wt = _pack_inproj(w_in[l], s5w, mw, gw)
        u, mqt, mk, mvt, gq, gk, gv, gz, ab = _inproj(x2, w, wt, bsz=bsz, seqlen=seqlen, s5w=s5w, mw=mw, gw=gw)
        bblk, cblk, lam, dskip = _s5_params(s5_lam_re[l], s5_lam_im[l], s5_log_dt[l], s5_b_re[l], s5_b_im[l],
                                            s5_c_re[l], s5_c_im[l], s5_d[l])
        y_s5 = _s5(u.reshape(seqlen * bsz, s5w), bblk, cblk, lam, dskip, s5_glu_w[l].astype(BF16),
                   row(s5_glu_b[l]), bsz=bsz, seqlen=seqlen).reshape(seqlen, bsz * s5w)
        y_moba = _moba(mqt, mk, mvt, seqlen=seqlen)
        y_gdn = _gdn(gq, gk, gv, gz, ab, *_gdn_params(gdn_conv_w[l], gdn_a_log[l], gdn_dt_bias[l], gdn_norm_w[l]))
        routed = l % 2 == 1
        router_w = moe_router[l // 2].astype(F32).T if routed else None
        res = _outproj(x2, y_s5, y_moba, y_gdn, w_out[l].astype(BF16), row(ln1_g[l]), row(ln1_b[l]), router_w,
                       bsz=bsz, seqlen=seqlen, alpha=alpha, n_exp=n_exp if routed else 0)
        if routed:
            x1, route, x1t = res
            p0, p1, g01, items = _dispatch_plan(route, n_exp)
            xs = _dispatch(p0, p1, x1t)
            ys = _ffn_moe(items, xs, moe_w_gate, moe_w_up, moe_w_down, l // 2)
            x2 = _combine(p0, p1, ys, x1, g01, row(ln2_g[l]), row(ln2_b[l]), alpha=alpha)
        else:
            x2 = _ffn_dense(res, ffn_w_gate, ffn_w_up, ffn_w_down, l // 2, row(ln2_g[l]), row(ln2_b[l]), alpha=alpha)
    return x2.reshape(bsz, seqlen, d)
```

```python
import functools

import jax
import jax.numpy as jnp
from jax import lax
from jax.experimental import pallas as pl
from jax.experimental.pallas import tpu as pltpu

F32 = jnp.float32
BF16 = jnp.bfloat16
I32 = jnp.int32
HI = lax.Precision.HIGHEST

HEAD_DIM = 64
MOBA_BLOCK = 256
MOBA_TOPK = 3
GDN_CHUNK = 64
S5_PARTS = 2
GDN_PREP_CHUNKS = 4
TOP_K = 2
LN_EPS = 1e-5
RMS_EPS = 1e-6
NEG_BIG = -1e30

V7X_VMEM_LIMIT_BYTES = 56 * 1024 * 1024
LANE = 128
TOKEN_TILE_ROWS = 8

FFN_ROWS = 1024
MOE_ROWS = 1024
MOE_SUB_ROWS = 256
FFN_COLS = 512
FFN_CHUNK = 256
GATHER_ROWS = 256


def _cparams(*sem):
    return pltpu.CompilerParams(dimension_semantics=sem, vmem_limit_bytes=V7X_VMEM_LIMIT_BYTES)


def _dot(a, b, precision=None):
    return jnp.dot(a, b, preferred_element_type=F32, precision=precision)


def _dot_nt(a, b, precision=None):
    return lax.dot_general(a, b, (((1,), (1,)), ((), ())), preferred_element_type=F32, precision=precision)


def _dot_tn(a, b, precision=None):
    return lax.dot_general(a, b, (((0,), (0,)), ((), ())), preferred_element_type=F32, precision=precision)


def _iota(shape, dim):
    return lax.broadcasted_iota(I32, shape, dim)


def _layer_norm(r, g, b):
    mu = jnp.mean(r, axis=-1, keepdims=True)
    c = r - mu
    var = jnp.mean(c * c, axis=-1, keepdims=True)
    return c * lax.rsqrt(var + LN_EPS) * g + b


def _inproj_kernel(x_ref, w_ref, wt_ref, u_ref, mqt_ref, mk_ref, mvt_ref, gq_ref, gk_ref, gv_ref, gz_ref,
                   ab_ref, *, s5w, mw, gw):
    xb = x_ref[...].astype(BF16)
    nh_m = mw // HEAD_DIM

    def cols(off, width):
        return _dot(xb, w_ref[:, off:off + width])

    u_ref[...] = cols(0, s5w)
    off = s5w + mw
    kk = cols(off, mw)
    for h in range(nh_m):
        mk_ref[0, h, 0] = kk[:, h * HEAD_DIM:(h + 1) * HEAD_DIM]
    off += 2 * mw
    for ref in (gq_ref, gk_ref, gv_ref, gz_ref):
        ref[0] = cols(off, gw)
        off += gw
    ab_ref[0] = cols(off, LANE)
    qt = _dot_nt(wt_ref[0:mw, :], xb)
    vt = _dot_nt(wt_ref[mw:2 * mw, :], xb)
    for h in range(nh_m):
        mqt_ref[0, h, 0] = qt[h * HEAD_DIM:(h + 1) * HEAD_DIM, :]
        mvt_ref[0, h, 0] = vt[h * HEAD_DIM:(h + 1) * HEAD_DIM, :]


def _pack_inproj(w_in, s5w, mw, gw):
    c = w_in.shape[1]
    main = s5w + 3 * mw + 4 * gw
    w = jnp.pad(w_in, ((0, 0), (0, main + LANE - c))).astype(BF16)
    q0, v0 = s5w, s5w + 2 * mw
    wt = jnp.concatenate([w_in[:, q0:q0 + mw], w_in[:, v0:v0 + mw]], axis=1).T.astype(BF16)
    return w, wt


def _inproj(x2, w, wt, *, bsz, seqlen, s5w, mw, gw):
    tl = MOBA_BLOCK
    nb = seqlen // tl
    d = x2.shape[1]
    nh_m = mw // HEAD_DIM
    hd = HEAD_DIM
    gdn_sds = jax.ShapeDtypeStruct((bsz, seqlen, gw), F32)
    out_shape = (
        jax.ShapeDtypeStruct((seqlen, bsz * s5w), F32),
        jax.ShapeDtypeStruct((bsz, nh_m, nb, hd, tl), F32),
        jax.ShapeDtypeStruct((bsz, nh_m, nb, tl, hd), F32),
        jax.ShapeDtypeStruct((bsz, nh_m, nb, hd, tl), F32),
        gdn_sds, gdn_sds, gdn_sds, gdn_sds,
        jax.ShapeDtypeStruct((bsz, seqlen, LANE), F32),
    )
    t5 = lambda b, j: (b, 0, j, 0, 0)
    t3 = lambda b, j: (b, j, 0)
    out_specs = (
        pl.BlockSpec((tl, s5w), lambda b, j: (j, b)),
        pl.BlockSpec((1, nh_m, 1, hd, tl), t5),
        pl.BlockSpec((1, nh_m, 1, tl, hd), t5),
        pl.BlockSpec((1, nh_m, 1, hd, tl), t5),
        pl.BlockSpec((1, tl, gw), t3),
        pl.BlockSpec((1, tl, gw), t3),
        pl.BlockSpec((1, tl, gw), t3),
        pl.BlockSpec((1, tl, gw), t3),
        pl.BlockSpec((1, tl, LANE), t3),
    )
    return pl.pallas_call(
        functools.partial(_inproj_kernel, s5w=s5w, mw=mw, gw=gw),
        out_shape=out_shape,
        grid=(bsz, nb),
        in_specs=[
            pl.BlockSpec((tl, d), lambda b, j: (b * nb + j, 0)),
            pl.BlockSpec(w.shape, lambda b, j: (0, 0)),
            pl.BlockSpec(wt.shape, lambda b, j: (0, 0)),
        ],
        out_specs=out_specs,
        compiler_params=_cparams("parallel", "arbitrary"),
        name="inproj",
    )(x2, w, wt)


def _s5_kernel(u_ref, bblk_ref, cblk_ref, lam_ref, d_ref, gw_ref, gb_ref, y_ref, st_ref, h_ref, *, bsz, tt, ns):
    @pl.when(pl.program_id(0) == 0)
    def _():
        h_ref[...] = jnp.zeros_like(h_ref)

    lam_re = jnp.broadcast_to(lam_ref[0:1, :], (bsz, ns))
    lam_im = jnp.broadcast_to(lam_ref[1:2, :], (bsz, ns))
    part = tt // S5_PARTS
    for q in range(S5_PARTS):
        rows = slice(q * part * bsz, (q + 1) * part * bsz)
        st_ref[rows, :] = _dot(u_ref[rows, :].astype(BF16), bblk_ref[...])
    h_re, h_im = h_ref[0], h_ref[1]
    for q in range(S5_PARTS):
        for t in range(q * part, (q + 1) * part):
            r = slice(t * bsz, (t + 1) * bsz)
            n_re = lam_re * h_re - lam_im * h_im + st_ref[r, 0:ns]
            n_im = lam_re * h_im + lam_im * h_re + st_ref[r, ns:2 * ns]
            st_ref[r, 0:ns] = n_re
            st_ref[r, ns:2 * ns] = n_im
            h_re, h_im = n_re, n_im
        rows = slice(q * part * bsz, (q + 1) * part * bsz)
        y = _dot(st_ref[rows, :].astype(BF16), cblk_ref[...]) + d_ref[...] * u_ref[rows, :]
        y = jax.nn.gelu(y)
        y_ref[rows, :] = y * jax.nn.sigmoid(_dot(y.astype(BF16), gw_ref[...]) + gb_ref[...])
    h_ref[0] = h_re
    h_ref[1] = h_im


def _s5(u2, bblk, cblk, lam, d, glu_w, glu_b, *, bsz, seqlen):
    width = u2.shape[1]
    ns = lam.shape[1]
    tt = 64
    rows = tt * bsz
    const = lambda i: (0, 0)
    return pl.pallas_call(
        functools.partial(_s5_kernel, bsz=bsz, tt=tt, ns=ns),
        out_shape=jax.ShapeDtypeStruct(u2.shape, F32),
        grid=(seqlen // tt,),
        in_specs=[
            pl.BlockSpec((rows, width), lambda i: (i, 0)),
            pl.BlockSpec(bblk.shape, const),
            pl.BlockSpec(cblk.shape, const),
            pl.BlockSpec(lam.shape, const),
            pl.BlockSpec(d.shape, const),
            pl.BlockSpec(glu_w.shape, const),
            pl.BlockSpec(glu_b.shape, const),
        ],
        out_specs=pl.BlockSpec((rows, width), lambda i: (i, 0)),
        scratch_shapes=[pltpu.VMEM((rows, 2 * ns), F32), pltpu.VMEM((2, bsz, ns), F32)],
        compiler_params=_cparams("arbitrary"),
        name="s5",
    )(u2, bblk, cblk, lam, d, glu_w, glu_b)


def _s5_params(lam_re, lam_im, log_dt, b_re, b_im, c_re, c_im, d_skip):
    g, n = lam_re.shape
    p = b_re.shape[-1]
    lam = lax.complex(lam_re.astype(F32), lam_im.astype(F32))
    step = jnp.exp(log_dt.astype(F32))[:, None]
    lam_bar = jnp.exp(lam * step)
    b_bar = ((lam_bar - 1.0) / lam)[..., None] * lax.complex(b_re.astype(F32), b_im.astype(F32))
    eye = jnp.eye(g, dtype=F32)
    b_re_blk = jnp.einsum('gnp,gh->gphn', b_bar.real, eye).reshape(g * p, g * n)
    b_im_blk = jnp.einsum('gnp,gh->gphn', b_bar.imag, eye).reshape(g * p, g * n)
    bblk = jnp.concatenate([b_re_blk, b_im_blk], axis=1)
    c_re_blk = jnp.einsum('gpn,gh->gnhp', c_re.astype(F32), eye).reshape(g * n, g * p)
    c_im_blk = jnp.einsum('gpn,gh->gnhp', c_im.astype(F32), eye).reshape(g * n, g * p)
    cblk = jnp.concatenate([c_re_blk, -c_im_blk], axis=0)
    lam2 = jnp.stack([lam_bar.real.reshape(g * n), lam_bar.imag.reshape(g * n)])
    return bblk.astype(BF16), cblk.astype(BF16), lam2, d_skip.astype(F32).reshape(1, g * p)


def _moba_kernel(qt_ref, k_ref, vt_ref, o_ref, km_ref, qs_ref, bias_ref, acc_ref, s_ref, *, nh, nb):
    i = pl.program_id(1)
    blk = MOBA_BLOCK
    hd = HEAD_DIM

    @pl.when(i == 0)
    def _():
        avg = jnp.full((1, blk), 1.0 / blk, F32)
        for h in range(nh):
            for n in range(nb):
                km_ref[h, n:n + 1, :] = _dot(avg, k_ref[0, h, n], precision=HI)

    n_iota = _iota((nb, blk), 0)
    kpos = _iota((blk, blk), 0)
    qpos = _iota((blk, blk), 1)
    causal = kpos <= qpos

    heads = range(nh)

    qts = [qt_ref[0, h, 0] * (hd ** -0.5) for h in heads]
    qtb = [qts[h].astype(BF16) for h in heads]
    s_own = [_dot(k_ref[0, h, i].astype(BF16), qtb[h]) for h in heads]
    gates = [_dot(km_ref[h], qts[h], precision=HI) for h in heads]
    vt_own = [vt_ref[0, h, i].astype(BF16) for h in heads]
    biases = []
    for h in heads:
        gate = jnp.where(n_iota < i, gates[h], -jnp.inf)
        rank = jnp.zeros((nb, blk), I32)
        for m in range(nb):
            gm = gate[m:m + 1, :]
            ahead = (gm > gate) | ((gm == gate) & (m < n_iota))
            rank = rank + ahead.astype(I32)
        biases.append(jnp.where((n_iota < i) & (rank < MOBA_TOPK), 0.0, NEG_BIG))
    m_own, l_own, acc_own = [], [], []
    for h in heads:
        s = jnp.where(causal, s_own[h], NEG_BIG)
        m0 = jnp.max(s, axis=0, keepdims=True)
        p = jnp.exp(s - m0)
        m_own.append(m0)
        l_own.append(jnp.sum(p, axis=0, keepdims=True))
        acc_own.append(_dot(vt_own[h], p.astype(BF16)))
    s_first = [_dot(k_ref[0, h, 0].astype(BF16), qtb[h]) for h in heads]
    for h in heads:
        bias_ref[h] = biases[h]
        qs_ref[h] = qtb[h]
        acc_ref[h * hd:(h + 1) * hd, :] = acc_own[h]
        s_ref[h] = s_first[h]

    def body(j, carry):
        ms, ls = carry
        ss = [s_ref[h] + bias_ref[h, pl.ds(j, 1), :] for h in heads]
        nxt = jnp.minimum(j + 1, i - 1)
        s_next = [_dot(k_ref[0, h, nxt].astype(BF16), qs_ref[h]) for h in heads]
        vts = [vt_ref[0, h, j].astype(BF16) for h in heads]
        accs = [acc_ref[h * hd:(h + 1) * hd, :] for h in heads]
        new_ms, new_ls, new_accs = [], [], []
        for h in heads:
            m_new = jnp.maximum(ms[h], jnp.max(ss[h], axis=0, keepdims=True))
            p = jnp.exp(ss[h] - m_new)
            corr = jnp.exp(ms[h] - m_new)
            new_ms.append(m_new)
            new_ls.append(ls[h] * corr + jnp.sum(p, axis=0, keepdims=True))
            new_accs.append(accs[h] * corr + _dot(vts[h], p.astype(BF16)))
        for h in heads:
            acc_ref[h * hd:(h + 1) * hd, :] = new_accs[h]
            s_ref[h] = s_next[h]
        return tuple(new_ms), tuple(new_ls)

    _, ls = lax.fori_loop(0, i, body, (tuple(m_own), tuple(l_own)))
    for h in heads:
        rows = slice(h * hd, (h + 1) * hd)
        acc_ref[rows, :] = acc_ref[rows, :] / ls[h]
    eye = (kpos == qpos).astype(F32)
    o_ref[0] = _dot_nt(eye, acc_ref[...], precision=HI)


def _moba(qt, k, vt, *, seqlen):
    bsz, nh, nb, hd, blk = qt.shape
    return pl.pallas_call(
        functools.partial(_moba_kernel, nh=nh, nb=nb),
        out_shape=jax.ShapeDtypeStruct((bsz, seqlen, nh * hd), F32),
        grid=(bsz, nb),
        in_specs=[
            pl.BlockSpec((1, nh, 1, hd, blk), lambda b, i: (b, 0, i, 0, 0)),
            pl.BlockSpec((1, nh, nb, blk, hd), lambda b, i: (b, 0, 0, 0, 0)),
            pl.BlockSpec((1, nh, nb, hd, blk), lambda b, i: (b, 0, 0, 0, 0)),
        ],
        out_specs=pl.BlockSpec((1, blk, nh * hd), lambda b, i: (b, i, 0)),
        scratch_shapes=[pltpu.VMEM((nh, nb, hd), F32), pltpu.VMEM((nh, hd, blk), BF16),
                        pltpu.VMEM((nh, nb, blk), F32), pltpu.VMEM((nh * hd, blk), F32),
                        pltpu.VMEM((nh, blk, blk), F32)],
        compiler_params=_cparams("parallel", "arbitrary"),
        name="moba",
    )(qt, k, vt)


def _gdn_kernel(q_ref, k_ref, v_ref, z_ref, ab_ref, cw_ref, al_ref, dtb_ref, nw_ref, o_ref,
                g_ref, beta_ref, us_ref, ws_ref, qe_ref, kd_ref, at_ref, el_ref, s_ref, *, nh, seqlen):
    npair = nh // 2
    ch = GDN_CHUNK
    hd = HEAD_DIM
    gw = nh * hd
    pw = 2 * hd
    pairs = range(npair)

    rr = _iota((pw, pw), 0)
    cc = _iota((pw, pw), 1)
    same_head = (rr // hd) == (cc // hd)
    mask_incl = same_head & ((rr % ch) >= (cc % ch))
    mask_strict = same_head & ((rr % ch) > (cc % ch))
    eye = (rr == cc).astype(F32)
    tril_c = (_iota((ch, ch), 0) >= _iota((ch, ch), 1)).astype(F32)
    lane_c = _iota((ch, LANE), 1)
    first = lane_c < hd
    row_col = _iota((pw, 1), 0)
    lane_row = _iota((1, pw), 1)

    def stack(x):
        return jnp.concatenate([jnp.where(first, x, 0.0), jnp.where(first, 0.0, x)], axis=0)

    def fold(x):
        return x[:ch] + x[ch:]

    def head_sumsq(y):
        y2 = y * y
        sa = jnp.sum(jnp.where(first, y2, 0.0), axis=1, keepdims=True)
        sb = jnp.sum(jnp.where(first, 0.0, y2), axis=1, keepdims=True)
        return jnp.where(first, sa, sb)

    a = ab_ref[0]
    ea = a + dtb_ref[...]
    g_ref[...] = -jnp.exp(al_ref[...]) * (jnp.maximum(ea, 0.0) + jnp.log1p(jnp.exp(-jnp.abs(ea))))
    beta_ref[...] = jax.nn.sigmoid(a)

    def conv_silu(src, t, p, c, r0):
        lanes = slice(p * pw, (p + 1) * pw)
        h0 = pl.multiple_of(jnp.maximum(r0 - 8, 0), 8)
        hist = jnp.where(c > 0, src[0, pl.ds(h0, 8), lanes], 0.0)
        x = jnp.concatenate([hist, src[0, pl.ds(r0, ch), lanes]], axis=0)
        c0 = t * gw + p * pw
        acc = x * cw_ref[3:4, c0:c0 + pw]
        for tap in range(3):
            acc = acc + pltpu.roll(x, 3 - tap, 0) * cw_ref[tap:tap + 1, c0:c0 + pw]
        y = acc[8:]
        return y * jax.nn.sigmoid(y)

    def prep(cc, carry):
        chunks = [cc * GDN_PREP_CHUNKS + u for u in range(GDN_PREP_CHUNKS)]
        r0s = [pl.multiple_of(c * ch, ch) for c in chunks]
        gcs = [_dot(tril_c, g_ref[pl.ds(r0, ch), :], precision=HI) for r0 in r0s]
        bch = [beta_ref[pl.ds(r0, ch), :] for r0 in r0s]
        streams = [(u, p) for u in range(GDN_PREP_CHUNKS) for p in pairs]
        ns = range(len(streams))
        qs, ks, vs, gcol, bcol, decay = [], [], [], [], [], []
        for u, p in streams:
            q = conv_silu(q_ref, 0, p, chunks[u], r0s[u])
            k = conv_silu(k_ref, 1, p, chunks[u], r0s[u])
            v = conv_silu(v_ref, 2, p, chunks[u], r0s[u])
            q = q * (lax.rsqrt(head_sumsq(q) + RMS_EPS) * (hd ** -0.5))
            k = k * lax.rsqrt(head_sumsq(k) + RMS_EPS)
            qs.append(stack(q))
            ks.append(stack(k))
            vs.append(stack(v))
            ha, hb = 2 * p, 2 * p + 1
            gst = jnp.concatenate([jnp.where(lane_c == ha, gcs[u], 0.0), jnp.where(lane_c == hb, gcs[u], 0.0)], axis=0)
            bst = jnp.concatenate([jnp.where(lane_c == nh + ha, bch[u], 0.0),
                                   jnp.where(lane_c == nh + hb, bch[u], 0.0)], axis=0)
            gc = jnp.sum(gst, axis=1, keepdims=True)
            gcol.append(gc)
            bcol.append(jnp.sum(bst, axis=1, keepdims=True))
            gmat = jnp.broadcast_to(gc, (pw, pw))
            decay.append(jnp.where(mask_incl, jnp.exp(jnp.where(mask_incl, gmat - gmat.T, 0.0)), 0.0))
        kbs = [ks[n] * bcol[n] for n in ns]
        ksb = [ks[n].astype(BF16) for n in ns]
        pm = [jnp.where(mask_strict, _dot_nt(kbs[n].astype(BF16), ksb[n]) * decay[n], 0.0) for n in ns]
        tm = [eye - pm[n] for n in ns]
        pmb = [pm[n].astype(BF16) for n in ns]
        pm = [_dot(pmb[n], pmb[n]) for n in ns]
        for _ in range(4):
            pmb = [pm[n].astype(BF16) for n in ns]
            tm = [tm[n] + _dot(tm[n].astype(BF16), pmb[n]) for n in ns]
            pm = [_dot(pmb[n], pmb[n]) for n in ns]
        tm = [tm[n] + _dot(tm[n].astype(BF16), pm[n].astype(BF16)) for n in ns]
        out = []
        for n in ns:
            eg = jnp.exp(gcol[n])
            rhs = jnp.concatenate([vs[n] * bcol[n], kbs[n] * eg], axis=1).astype(BF16)
            sol = _dot(tm[n].astype(BF16), rhs)
            attn = jnp.where(mask_incl, _dot_nt(qs[n].astype(BF16), ksb[n]) * decay[n], 0.0)
            g_a = gcol[n][ch - 1:ch]
            g_b = gcol[n][2 * ch - 1:2 * ch]
            glast = jnp.where(row_col < ch, g_a, g_b)
            e_last = jnp.where(lane_row < hd, jnp.exp(g_a), jnp.exp(g_b))
            out.append((fold(sol[:, :pw]), fold(sol[:, pw:]).astype(BF16), fold(qs[n] * eg).astype(BF16),
                        fold(ks[n] * jnp.exp(glast - gcol[n])).astype(BF16), fold(attn).astype(BF16),
                        jnp.broadcast_to(e_last, (8, pw))))
        for n, (u, p) in enumerate(streams):
            rows = pl.ds(r0s[u], ch)
            us_ref[p, rows, :], ws_ref[p, rows, :], qe_ref[p, rows, :], kd_ref[p, rows, :], at_ref[p, rows, :] = out[n][:5]
            el_ref[p, pl.ds(pl.multiple_of(chunks[u] * 8, 8), 8), :] = out[n][5]
        return carry

    lax.fori_loop(0, seqlen // (ch * GDN_PREP_CHUNKS), prep, 0)

    s_ref[...] = jnp.zeros_like(s_ref)

    def scan(c, carry):
        r0 = pl.multiple_of(c * ch, ch)
        rows = pl.ds(r0, ch)
        lanes = [slice(p * pw, (p + 1) * pw) for p in pairs]
        states = [s_ref[p] for p in pairs]
        wq = [jnp.concatenate([ws_ref[p, rows, :], qe_ref[p, rows, :]], axis=0) for p in pairs]
        us = [us_ref[p, rows, :] for p in pairs]
        at = [at_ref[p, rows, :] for p in pairs]
        kd = [kd_ref[p, rows, :] for p in pairs]
        el = [el_ref[p, pl.ds(c * 8, 1), :] for p in pairs]
        zs = [z_ref[0, rows, lanes[p]] for p in pairs]
        wqs = [_dot(wq[p], states[p].astype(BF16)) for p in pairs]
        v_new = [us[p] - wqs[p][:ch] for p in pairs]
        upd = [_dot_tn(kd[p], v_new[p].astype(BF16)) for p in pairs]
        new_states = [states[p] * el[p] + jnp.where(same_head, upd[p], 0.0) for p in pairs]
        outs = []
        for p in pairs:
            o = wqs[p][ch:] + _dot(at[p], stack(v_new[p]).astype(BF16))
            ms = head_sumsq(o) * (1.0 / hd)
            outs.append(o * lax.rsqrt(ms + RMS_EPS) * nw_ref[...] * (zs[p] * jax.nn.sigmoid(zs[p])))
        for p in pairs:
            s_ref[p] = new_states[p]
            o_ref[0, rows, lanes[p]] = outs[p]
        return carry

    lax.fori_loop(0, seqlen // ch, scan, 0)


def _gdn_params(conv_w, a_log, dt_bias, norm_w):
    nh = a_log.shape[0]
    lane_pad = lambda v: jnp.pad(v.astype(F32), (0, LANE - nh)).reshape(1, LANE)
    return conv_w.astype(F32), lane_pad(a_log), lane_pad(dt_bias), jnp.tile(norm_w.astype(F32), 2).reshape(1, LANE)


def _gdn(q, k, v, z, ab, conv_w, al, dtb, nw):
    bsz, seqlen, gw = q.shape
    nh = gw // HEAD_DIM
    npair = nh // 2
    seq_spec = pl.BlockSpec((1, seqlen, gw), lambda b: (b, 0, 0))
    const = lambda b: (0, 0)
    return pl.pallas_call(
        functools.partial(_gdn_kernel, nh=nh, seqlen=seqlen),
        out_shape=jax.ShapeDtypeStruct((bsz, seqlen, gw), F32),
        grid=(bsz,),
        in_specs=[seq_spec, seq_spec, seq_spec, seq_spec,
                  pl.BlockSpec((1, seqlen, LANE), lambda b: (b, 0, 0)),
                  pl.BlockSpec(conv_w.shape, const),
                  pl.BlockSpec(al.shape, const),
                  pl.BlockSpec(dtb.shape, const),
                  pl.BlockSpec(nw.shape, const)],
        out_specs=seq_spec,
        scratch_shapes=[pltpu.VMEM((seqlen, LANE), F32)] * 2
        + [pltpu.VMEM((npair, seqlen, LANE), F32)]
        + [pltpu.VMEM((npair, seqlen, LANE), BF16)] * 4
        + [pltpu.VMEM((npair, seqlen // GDN_CHUNK * 8, LANE), F32)]
        + [pltpu.VMEM((npair, LANE, LANE), F32)],
        compiler_params=_cparams("parallel"),
        name="gdn",
    )(q, k, v, z, ab, conv_w, al, dtb, nw)


def _route(logits):
    n_exp = logits.shape[0]
    eidx = _iota(logits.shape, 0)
    m1 = jnp.max(logits, axis=0, keepdims=True)
    i1 = jnp.min(jnp.where(logits == m1, eidx, n_exp), axis=0, keepdims=True)
    lg2 = jnp.where(eidx == i1, -jnp.inf, logits)
    m2 = jnp.max(lg2, axis=0, keepdims=True)
    i2 = jnp.min(jnp.where(lg2 == m2, eidx, n_exp), axis=0, keepdims=True)
    e2 = jnp.exp(m2 - m1)
    g1 = 1.0 / (1.0 + e2)
    g2 = e2 / (1.0 + e2)
    gates = jnp.where(eidx == i1, g1, 0.0) + jnp.where(eidx == i2, g2, 0.0)
    mask = jnp.where((eidx == i1) | (eidx == i2), 1.0, 0.0)
    return jnp.concatenate([gates, mask], axis=0)


def _outproj_kernel(x_ref, ys_ref, ym_ref, yg_ref, w_ref, g_ref, b_ref, *rest, alpha, s5w, mw, n_exp):
    mix = _dot(ys_ref[...].astype(BF16), w_ref[0:s5w, :])
    mix = mix + _dot(ym_ref[0].astype(BF16), w_ref[s5w:s5w + mw, :])
    mix = mix + _dot(yg_ref[0].astype(BF16), w_ref[s5w + mw:, :])
    x1 = _layer_norm(alpha * x_ref[...] + mix, g_ref[...], b_ref[...])
    if n_exp:
        rw_ref, o_ref, r_ref, ot_ref = rest
        o_ref[...] = x1
        r_ref[...] = _route(_dot_nt(rw_ref[...], x1, precision=HI))
        for s in range(TOKEN_TILE_ROWS):
            ot_ref[:, s, :] = x1[:, s * LANE:(s + 1) * LANE]
    else:
        (o_ref,) = rest
        o_ref[...] = x1


def _outproj(x2, ys, ym, yg, w, g, b, router_w, *, bsz, seqlen, alpha, n_exp):
    tl = 512
    nt = seqlen // tl
    t, d = x2.shape
    s5w = ys.shape[1] // bsz
    mw = ym.shape[2]
    gw = yg.shape[2]
    const = lambda bi, j: (0, 0)
    row_spec = pl.BlockSpec((tl, d), lambda bi, j: (bi * nt + j, 0))
    in_specs = [row_spec,
                pl.BlockSpec((tl, s5w), lambda bi, j: (j, bi)),
                pl.BlockSpec((1, tl, mw), lambda bi, j: (bi, j, 0)),
                pl.BlockSpec((1, tl, gw), lambda bi, j: (bi, j, 0)),
                pl.BlockSpec(w.shape, const), pl.BlockSpec(g.shape, const), pl.BlockSpec(b.shape, const)]
    args = [x2, ys, ym, yg, w, g, b]
    out_shape = jax.ShapeDtypeStruct((t, d), F32)
    out_specs = row_spec
    if n_exp:
        in_specs.append(pl.BlockSpec(router_w.shape, const))
        args.append(router_w)
        assert d == TOKEN_TILE_ROWS * LANE
        out_shape = (out_shape, jax.ShapeDtypeStruct((2 * n_exp, t), F32),
                     jax.ShapeDtypeStruct((t, TOKEN_TILE_ROWS, LANE), F32))
        out_specs = (row_spec, pl.BlockSpec((2 * n_exp, tl), lambda bi, j: (0, bi * nt + j)),
                     pl.BlockSpec((tl, TOKEN_TILE_ROWS, LANE), lambda bi, j: (bi * nt + j, 0, 0)))
    return pl.pallas_call(
        functools.partial(_outproj_kernel, alpha=alpha, s5w=s5w, mw=mw, n_exp=n_exp),
        out_shape=out_shape,
        grid=(bsz, nt),
        in_specs=in_specs,
        out_specs=out_specs,
        compiler_params=_cparams("parallel", "arbitrary"),
        name="outproj_route" if n_exp else "outproj",
    )(*args)


def _swiglu_part(xb, wg_ref, wu_ref, wd_ref):
    out = None
    for c0 in range(0, wg_ref.shape[1], FFN_CHUNK):
        cols = slice(c0, c0 + FFN_CHUNK)
        hg = _dot(xb, wg_ref[:, cols].astype(BF16))
        hu = _dot(xb, wu_ref[:, cols].astype(BF16))
        h = (hg * jax.nn.sigmoid(hg)) * hu
        part = _dot(h.astype(BF16), wd_ref[cols, :].astype(BF16))
        out = part if out is None else out + part
    return out


def _ffn_dense_kernel(x_ref, wg_ref, wu_ref, wd_ref, g_ref, b_ref, o_ref, xb_ref, acc_ref, *, alpha):
    @pl.when(pl.program_id(1) == 0)
    def _():
        xb_ref[...] = x_ref[...].astype(BF16)
        acc_ref[...] = jnp.zeros_like(acc_ref)

    acc_ref[...] += _swiglu_part(xb_ref[...], wg_ref.at[0], wu_ref.at[0], wd_ref.at[0])

    @pl.when(pl.program_id(1) == pl.num_programs(1) - 1)
    def _():
        o_ref[...] = _layer_norm(alpha * x_ref[...] + acc_ref[...], g_ref[...], b_ref[...])


def _ffn_dense(x2, wg, wu, wd, layer, g, b, *, alpha):
    t, d = x2.shape
    ff = wg.shape[2]
    tm, tf = FFN_ROWS, FFN_COLS
    const = lambda i, f: (0, 0)
    return pl.pallas_call(
        functools.partial(_ffn_dense_kernel, alpha=alpha),
        out_shape=jax.ShapeDtypeStruct((t, d), F32),
        grid=(t // tm, ff // tf),
        in_specs=[pl.BlockSpec((tm, d), lambda i, f: (i, 0)),
                  pl.BlockSpec((1, d, tf), lambda i, f: (layer, 0, f)),
                  pl.BlockSpec((1, d, tf), lambda i, f: (layer, 0, f)),
                  pl.BlockSpec((1, tf, d), lambda i, f: (layer, f, 0)),
                  pl.BlockSpec(g.shape, const), pl.BlockSpec(b.shape, const)],
        out_specs=pl.BlockSpec((tm, d), lambda i, f: (i, 0)),
        scratch_shapes=[pltpu.VMEM((tm, d), BF16), pltpu.VMEM((tm, d), F32)],
        compiler_params=_cparams("parallel", "arbitrary"),
        name="ffn_dense",
    )(x2, wg, wu, wd, g, b)


def _ffn_moe_kernel(tile_ref, exp_ref, lo_ref, hi_ref, first_ref, x_ref, wg_ref, wu_ref, wd_ref, o_ref,
                    xt_ref, xb_ref, acc_ref, wgb_ref, wub_ref, wdb_ref):
    k = pl.program_id(0)
    f = pl.program_id(1)
    lo, hi = lo_ref[k], hi_ref[k]
    sub = MOE_SUB_ROWS
    d = o_ref.shape[1]

    nsub = o_ref.shape[0] // sub
    touched = sum(((lo < (sb + 1) * sub) & (hi > sb * sub)).astype(I32) for sb in range(nsub))
    whole = touched > nsub // 2

    def each_live_sub_block(fn):
        for sb in range(nsub):
            live = whole | ((lo < (sb + 1) * sub) & (hi > sb * sub))
            pl.when(live)(functools.partial(fn, sb, slice(sb * sub, (sb + 1) * sub)))

    @pl.when(hi > lo)
    def _():
        @pl.when((f == 0) & (first_ref[k] > 0))
        def _():
            o_ref[...] = jnp.zeros_like(o_ref)

        def load_rows(sb, rows):
            for s in range(TOKEN_TILE_ROWS):
                xt_ref[rows, s * LANE:(s + 1) * LANE] = x_ref[rows, s, :]
            xb_ref[rows, :] = xt_ref[rows, :].astype(BF16)
            acc_ref[rows, :] = jnp.zeros((sub, d), F32)

        pl.when(f == 0)(functools.partial(each_live_sub_block, load_rows))

        @pl.when(whole)
        def _():
            acc_ref[...] += _swiglu_part(xb_ref[...], wg_ref.at[0, 0], wu_ref.at[0, 0], wd_ref.at[0, 0])

        @pl.when(jnp.logical_not(whole))
        def _():
            wgb_ref[...] = wg_ref[0, 0].astype(BF16)
            wub_ref[...] = wu_ref[0, 0].astype(BF16)
            wdb_ref[...] = wd_ref[0, 0].astype(BF16)

            def compute(sb, rows):
                acc_ref[rows, :] += _swiglu_part(xb_ref[rows, :], wgb_ref, wub_ref, wdb_ref)

            each_live_sub_block(compute)

        def emit(sb, rows):
            row = _iota((sub, d), 0) + sb * sub
            o_ref[rows, :] = jnp.where((row >= lo) & (row < hi), acc_ref[rows, :], o_ref[rows, :])

        pl.when(f == pl.num_programs(1) - 1)(functools.partial(each_live_sub_block, emit))


def _ffn_moe(items, xs, wg, wu, wd, layer):
    p_rows = xs.shape[0]
    d = wg.shape[2]
    ff = wg.shape[3]
    tm, tf = MOE_ROWS, FFN_COLS
    nf = ff // tf
    n_items = items[0].shape[0]

    def fsel(k, f, lo, hi):
        return jnp.where(hi[k] > lo[k], f, nf - 1)

    grid_spec = pltpu.PrefetchScalarGridSpec(
        num_scalar_prefetch=5,
        grid=(n_items, nf),
        in_specs=[pl.BlockSpec((tm, TOKEN_TILE_ROWS, LANE), lambda k, f, tile, exp, lo, hi, first: (tile[k], 0, 0)),
                  pl.BlockSpec((1, 1, d, tf),
                               lambda k, f, tile, exp, lo, hi, first: (layer, exp[k], 0, fsel(k, f, lo, hi))),
                  pl.BlockSpec((1, 1, d, tf),
                               lambda k, f, tile, exp, lo, hi, first: (layer, exp[k], 0, fsel(k, f, lo, hi))),
                  pl.BlockSpec((1, 1, tf, d),
                               lambda k, f, tile, exp, lo, hi, first: (layer, exp[k], fsel(k, f, lo, hi), 0))],
        out_specs=pl.BlockSpec((tm, d), lambda k, f, tile, exp, lo, hi, first: (tile[k], 0)),
        scratch_shapes=[pltpu.VMEM((tm, d), F32), pltpu.VMEM((tm, d), BF16), pltpu.VMEM((tm, d), F32),
                        pltpu.VMEM((d, tf), BF16), pltpu.VMEM((d, tf), BF16), pltpu.VMEM((tf, d), BF16)],
    )
    return pl.pallas_call(
        _ffn_moe_kernel,
        out_shape=jax.ShapeDtypeStruct((p_rows, d), F32),
        grid_spec=grid_spec,
        compiler_params=_cparams("arbitrary", "arbitrary"),
        name="ffn_moe",
    )(*items, xs, wg, wu, wd)


def _tile_copy(src_ref, src_row, dst_ref, dst_row, sem):
    return pltpu.make_async_copy(src_ref.at[pl.ds(src_row, 1)], dst_ref.at[pl.ds(dst_row, 1)], sem)


def _dispatch_kernel(p0_ref, p1_ref, x_ref, xs_hbm, sem, *, rows):
    def start(r, carry):
        _tile_copy(x_ref, r, xs_hbm, p0_ref[0, 0, r], sem.at[0]).start(priority=0)
        _tile_copy(x_ref, r, xs_hbm, p1_ref[0, 0, r], sem.at[1]).start(priority=1)
        return carry

    lax.fori_loop(0, rows, start, 0, unroll=8)

    def wait(r, carry):
        _tile_copy(x_ref, r, xs_hbm, 0, sem.at[0]).wait()
        _tile_copy(x_ref, r, xs_hbm, 0, sem.at[1]).wait()
        return carry

    lax.fori_loop(0, rows, wait, 0, unroll=8)


def _dispatch(p0, p1, xt):
    nt, _, rows = p0.shape
    t = xt.shape[0]
    idx_spec = pl.BlockSpec((1, 1, rows), lambda i: (i, 0, 0), memory_space=pltpu.SMEM)
    return pl.pallas_call(
        functools.partial(_dispatch_kernel, rows=rows),
        out_shape=jax.ShapeDtypeStruct((t * TOP_K, TOKEN_TILE_ROWS, LANE), xt.dtype),
        grid=(nt,),
        in_specs=[idx_spec, idx_spec, pl.BlockSpec((rows, TOKEN_TILE_ROWS, LANE), lambda i: (i, 0, 0))],
        out_specs=pl.BlockSpec(memory_space=pl.ANY),
        scratch_shapes=[pltpu.SemaphoreType.DMA((2,))],
        compiler_params=_cparams("arbitrary"),
        name="moe_dispatch",
    )(p0, p1, xt)


def _combine_kernel(p0_ref, p1_ref, ys_hbm, x_ref, gt_ref, g_ref, b_ref, o_ref, y0_ref, y1_ref, sem, *, rows, alpha):
    def start(r, carry):
        _tile_copy(ys_hbm, p0_ref[0, 0, r], y0_ref, r, sem.at[0]).start(priority=0)
        _tile_copy(ys_hbm, p1_ref[0, 0, r], y1_ref, r, sem.at[1]).start(priority=1)
        return carry

    lax.fori_loop(0, rows, start, 0, unroll=8)

    def wait(r, carry):
        _tile_copy(ys_hbm, 0, y0_ref, r, sem.at[0]).wait()
        _tile_copy(ys_hbm, 0, y1_ref, r, sem.at[1]).wait()
        return carry

    lax.fori_loop(0, rows, wait, 0, unroll=8)
    gt = gt_ref[...]
    f = gt[:, 0:1] * y0_ref[...] + gt[:, 1:2] * y1_ref[...]
    o_ref[...] = _layer_norm(alpha * x_ref[...] + f, g_ref[...], b_ref[...])


def _combine(p0, p1, ys, x2, gates, g, b, *, alpha):
    nt, _, rows = p0.shape
    t, d = x2.shape
    const = lambda i: (0, 0)
    idx_spec = pl.BlockSpec((1, 1, rows), lambda i: (i, 0, 0), memory_space=pltpu.SMEM)
    tile_buf = pltpu.VMEM((rows, d), F32)
    return pl.pallas_call(
        functools.partial(_combine_kernel, rows=rows, alpha=alpha),
        out_shape=jax.ShapeDtypeStruct((t, d), F32),
        grid=(nt,),
        in_specs=[idx_spec, idx_spec,
                  pl.BlockSpec(memory_space=pl.ANY),
                  pl.BlockSpec((rows, d), lambda i: (i, 0)),
                  pl.BlockSpec((rows, LANE), lambda i: (i, 0)),
                  pl.BlockSpec(g.shape, const), pl.BlockSpec(b.shape, const)],
        out_specs=pl.BlockSpec((rows, d), lambda i: (i, 0)),
        scratch_shapes=[tile_buf, tile_buf, pltpu.SemaphoreType.DMA((2,))],
        compiler_params=_cparams("arbitrary"),
        name="moe_combine",
    )(p0, p1, ys, x2, gates, g, b)


def _dispatch_plan(route, n_exp):
    t = route.shape[1]
    tm = MOE_ROWS
    gates = route[:n_exp].T
    sel = route[n_exp:].T > 0.5
    seli = sel.astype(I32)
    rank = jnp.cumsum(seli, axis=0) - seli
    counts = jnp.sum(seli, axis=0)
    ends = jnp.cumsum(counts)
    starts = ends - counts
    pos = starts[None, :] + rank
    order = jnp.cumsum(seli, axis=1) - seli
    first = sel & (order == 0)
    second = sel & (order == 1)
    pick = lambda m, v: jnp.sum(jnp.where(m, v, 0), axis=1)
    p0, p1 = pick(first, pos), pick(second, pos)
    g01 = jnp.stack([pick(first, gates), pick(second, gates)], axis=1)
    g01 = jnp.pad(g01, ((0, 0), (0, LANE - TOP_K)))
    n_tiles = t * TOP_K // tm
    n_items = n_tiles + n_exp - 1
    tile_lo = jnp.arange(n_tiles, dtype=I32)[:, None] * tm
    ov_lo = jnp.maximum(starts[None, :], tile_lo)
    ov_hi = jnp.minimum(ends[None, :], tile_lo + tm)
    live = (ov_hi > ov_lo).reshape(-1)
    flat = jnp.arange(n_tiles * n_exp, dtype=I32)
    take = jnp.argsort(jnp.where(live, flat, n_tiles * n_exp))[:n_items]
    n_live = jnp.sum(live.astype(I32))
    idle = jnp.arange(n_items, dtype=I32) >= n_live
    last = take[jnp.maximum(n_live - 1, 0)]
    take = jnp.where(idle, last, take)
    item_tile = take // n_exp
    item_exp = take % n_exp
    item_lo = jnp.where(idle, 0, (ov_lo - tile_lo).reshape(-1)[take])
    item_hi = jnp.where(idle, 0, (ov_hi - tile_lo).reshape(-1)[take])
    prev_tile = jnp.concatenate([jnp.full((1,), -1, I32), item_tile[:-1]])
    item_first = (item_tile != prev_tile).astype(I32)
    items = tuple(v.astype(I32) for v in (item_tile, item_exp, item_lo, item_hi, item_first))
    shape3 = lambda v: v.astype(I32).reshape(-1, 1, GATHER_ROWS)
    return shape3(p0), shape3(p1), g01, items


def kernel(x, w_in, w_out, s5_lam_re, s5_lam_im, s5_log_dt, s5_b_re, s5_b_im, s5_c_re, s5_c_im, s5_d, s5_glu_w,
           s5_glu_b, gdn_conv_w, gdn_a_log, gdn_dt_bias, gdn_norm_w, ln1_g, ln1_b, ln2_g, ln2_b, ffn_w_gate,
           ffn_w_up, ffn_w_down, moe_router, moe_w_gate, moe_w_up, moe_w_down):
    bsz, seqlen, d = x.shape
    depth = w_in.shape[0]
    s5w = s5_glu_w.shape[1]
    gw = gdn_a_log.shape[1] * HEAD_DIM
    mw = w_out.shape[1] - s5w - gw
    n_exp = moe_router.shape[2]
    alpha = (2 * depth) ** 0.25
    row = lambda v: v.astype(F32).reshape(1, -1)

    x2 = x.reshape(bsz * seqlen, d)
    for l in range(depth):
        w, wt = _pack_inproj(w_in[l], s5w, mw, gw)
        u, mqt, mk, mvt, gq, gk, gv, gz, ab = _inproj(x2, w, wt, bsz=bsz, seqlen=seqlen, s5w=s5w, mw=mw, gw=gw)
        bblk, cblk, lam, dskip = _s5_params(s5_lam_re[l], s5_lam_im[l], s5_log_dt[l], s5_b_re[l], s5_b_im[l],
                                            s5_c_re[l], s5_c_im[l], s5_d[l])
        y_s5 = _s5(u.reshape(seqlen * bsz, s5w), bblk, cblk, lam, dskip, s5_glu_w[l].astype(BF16),
                   row(s5_glu_b[l]), bsz=bsz, seqlen=seqlen).reshape(seqlen, bsz * s5w)
        y_moba = _moba(mqt, mk, mvt, seqlen=seqlen)
        y_gdn = _gdn(gq, gk, gv, gz, ab, *_gdn_params(gdn_conv_w[l], gdn_a_log[l], gdn_dt_bias[l], gdn_norm_w[l]))
        routed = l % 2 == 1
        router_w = moe_router[l // 2].astype(F32).T if routed else None
        res = _outproj(x2, y_s5, y_moba, y_gdn, w_out[l].astype(BF16), row(ln1_g[l]), row(ln1_b[l]), router_w,
                       bsz=bsz, seqlen=seqlen, alpha=alpha, n_exp=n_exp if routed else 0)
        if routed:
            x1, route, x1t = res
            p0, p1, g01, items = _dispatch_plan(route, n_exp)
            xs = _dispatch(p0, p1, x1t)
            ys = _ffn_moe(items, xs, moe_w_gate, moe_w_up, moe_w_down, l // 2)
            x2 = _combine(p0, p1, ys, x1, g01, row(ln2_g[l]), row(ln2_b[l]), alpha=alpha)
        else:
            x2 = _ffn_dense(res, ffn_w_gate, ffn_w_up, ffn_w_down, l // 2, row(ln2_g[l]), row(ln2_b[l]), alpha=alpha)
    return x2.reshape(bsz, seqlen, d)
```

```python
import functools

import jax
import jax.numpy as jnp
from jax import lax
from jax.experimental import pallas as pl
from jax.experimental.pallas import tpu as pltpu

F32 = jnp.float32
BF16 = jnp.bfloat16
I32 = jnp.int32
HI = lax.Precision.HIGHEST

HEAD_DIM = 64
MOBA_BLOCK = 256
MOBA_TOPK = 3
GDN_CHUNK = 64
S5_PARTS = 2
GDN_PREP_CHUNKS = 4
TOP_K = 2
LN_EPS = 1e-5
RMS_EPS = 1e-6
NEG_BIG = -1e30

V7X_VMEM_LIMIT_BYTES = 56 * 1024 * 1024
LANE = 128
TOKEN_TILE_ROWS = 8

INPROJ_ROWS = 1024
OUTPROJ_ROWS = 1024
FFN_ROWS = 1024
MOE_ROWS = 1024
MOE_SUB_ROWS = 256
FFN_COLS = 512
FFN_CHUNK = 256
GATHER_ROWS = 256


def _cparams(*sem):
    return pltpu.CompilerParams(dimension_semantics=sem, vmem_limit_bytes=V7X_VMEM_LIMIT_BYTES)


def _dot(a, b, precision=None):
    return jnp.dot(a, b, preferred_element_type=F32, precision=precision)


def _dot_nt(a, b, precision=None):
    return lax.dot_general(a, b, (((1,), (1,)), ((), ())), preferred_element_type=F32, precision=precision)


def _dot_tn(a, b, precision=None):
    return lax.dot_general(a, b, (((0,), (0,)), ((), ())), preferred_element_type=F32, precision=precision)


def _iota(shape, dim):
    return lax.broadcasted_iota(I32, shape, dim)


def _layer_norm(r, g, b):
    mu = jnp.mean(r, axis=-1, keepdims=True)
    c = r - mu
    var = jnp.mean(c * c, axis=-1, keepdims=True)
    return c * lax.rsqrt(var + LN_EPS) * g + b


def _inproj_kernel(x_ref, w_ref, wt_ref, u_ref, mqt_ref, mk_ref, mvt_ref, gq_ref, gk_ref, gv_ref, gz_ref,
                   ab_ref, *, s5w, mw, gw):
    xb = x_ref[...].astype(BF16)
    nh_m = mw // HEAD_DIM
    blk = MOBA_BLOCK
    blocks = range(x_ref.shape[0] // blk)

    def cols(off, width):
        return _dot(xb, w_ref[:, off:off + width])

    u_ref[...] = cols(0, s5w)
    off = s5w + mw
    kk = cols(off, mw)
    for h in range(nh_m):
        for n in blocks:
            mk_ref[0, h, n] = kk[n * blk:(n + 1) * blk, h * HEAD_DIM:(h + 1) * HEAD_DIM]
    off += 2 * mw
    for ref in (gq_ref, gk_ref, gv_ref, gz_ref):
        ref[0] = cols(off, gw)
        off += gw
    ab_ref[0] = cols(off, LANE)
    qt = _dot_nt(wt_ref[0:mw, :], xb)
    vt = _dot_nt(wt_ref[mw:2 * mw, :], xb)
    for h in range(nh_m):
        for n in blocks:
            mqt_ref[0, h, n] = qt[h * HEAD_DIM:(h + 1) * HEAD_DIM, n * blk:(n + 1) * blk]
            mvt_ref[0, h, n] = vt[h * HEAD_DIM:(h + 1) * HEAD_DIM, n * blk:(n + 1) * blk]


def _pack_inproj(w_in, s5w, mw, gw):
    c = w_in.shape[1]
    main = s5w + 3 * mw + 4 * gw
    w = jnp.pad(w_in, ((0, 0), (0, main + LANE - c))).astype(BF16)
    q0, v0 = s5w, s5w + 2 * mw
    wt = jnp.concatenate([w_in[:, q0:q0 + mw], w_in[:, v0:v0 + mw]], axis=1).T.astype(BF16)
    return w, wt


def _inproj(x2, w, wt, *, bsz, seqlen, s5w, mw, gw):
    tl = INPROJ_ROWS
    blk = MOBA_BLOCK
    nt = seqlen // tl
    nb = seqlen // blk
    per = tl // blk
    d = x2.shape[1]
    nh_m = mw // HEAD_DIM
    hd = HEAD_DIM
    gdn_sds = jax.ShapeDtypeStruct((bsz, seqlen, gw), F32)
    out_shape = (
        jax.ShapeDtypeStruct((seqlen, bsz * s5w), F32),
        jax.ShapeDtypeStruct((bsz, nh_m, nb, hd, blk), F32),
        jax.ShapeDtypeStruct((bsz, nh_m, nb, blk, hd), F32),
        jax.ShapeDtypeStruct((bsz, nh_m, nb, hd, blk), F32),
        gdn_sds, gdn_sds, gdn_sds, gdn_sds,
        jax.ShapeDtypeStruct((bsz, seqlen, LANE), F32),
    )
    t5 = lambda b, j: (b, 0, j, 0, 0)
    t3 = lambda b, j: (b, j, 0)
    out_specs = (
        pl.BlockSpec((tl, s5w), lambda b, j: (j, b)),
        pl.BlockSpec((1, nh_m, per, hd, blk), t5),
        pl.BlockSpec((1, nh_m, per, blk, hd), t5),
        pl.BlockSpec((1, nh_m, per, hd, blk), t5),
        pl.BlockSpec((1, tl, gw), t3),
        pl.BlockSpec((1, tl, gw), t3),
        pl.BlockSpec((1, tl, gw), t3),
        pl.BlockSpec((1, tl, gw), t3),
        pl.BlockSpec((1, tl, LANE), t3),
    )
    return pl.pallas_call(
        functools.partial(_inproj_kernel, s5w=s5w, mw=mw, gw=gw),
        out_shape=out_shape,
        grid=(bsz, nt),
        in_specs=[
            pl.BlockSpec((tl, d), lambda b, j: (b * nt + j, 0)),
            pl.BlockSpec(w.shape, lambda b, j: (0, 0)),
            pl.BlockSpec(wt.shape, lambda b, j: (0, 0)),
        ],
        out_specs=out_specs,
        compiler_params=_cparams("parallel", "arbitrary"),
        name="inproj",
    )(x2, w, wt)


def _s5_kernel(u_ref, bblk_ref, cblk_ref, lam_ref, d_ref, gw_ref, gb_ref, y_ref, st_ref, h_ref, *, bsz, tt, ns):
    @pl.when(pl.program_id(0) == 0)
    def _():
        h_ref[...] = jnp.zeros_like(h_ref)

    lam_re = jnp.broadcast_to(lam_ref[0:1, :], (bsz, ns))
    lam_im = jnp.broadcast_to(lam_ref[1:2, :], (bsz, ns))
    part = tt // S5_PARTS
    for q in range(S5_PARTS):
        rows = slice(q * part * bsz, (q + 1) * part * bsz)
        st_ref[rows, :] = _dot(u_ref[rows, :].astype(BF16), bblk_ref[...])
    h_re, h_im = h_ref[0], h_ref[1]
    for q in range(S5_PARTS):
        for t in range(q * part, (q + 1) * part):
            r = slice(t * bsz, (t + 1) * bsz)
            n_re = lam_re * h_re - lam_im * h_im + st_ref[r, 0:ns]
            n_im = lam_re * h_im + lam_im * h_re + st_ref[r, ns:2 * ns]
            st_ref[r, 0:ns] = n_re
            st_ref[r, ns:2 * ns] = n_im
            h_re, h_im = n_re, n_im
        rows = slice(q * part * bsz, (q + 1) * part * bsz)
        y = _dot(st_ref[rows, :].astype(BF16), cblk_ref[...]) + d_ref[...] * u_ref[rows, :]
        y = jax.nn.gelu(y)
        y_ref[rows, :] = y * jax.nn.sigmoid(_dot(y.astype(BF16), gw_ref[...]) + gb_ref[...])
    h_ref[0] = h_re
    h_ref[1] = h_im


def _s5(u2, bblk, cblk, lam, d, glu_w, glu_b, *, bsz, seqlen):
    width = u2.shape[1]
    ns = lam.shape[1]
    tt = 64
    rows = tt * bsz
    const = lambda i: (0, 0)
    return pl.pallas_call(
        functools.partial(_s5_kernel, bsz=bsz, tt=tt, ns=ns),
        out_shape=jax.ShapeDtypeStruct(u2.shape, F32),
        grid=(seqlen // tt,),
        in_specs=[
            pl.BlockSpec((rows, width), lambda i: (i, 0)),
            pl.BlockSpec(bblk.shape, const),
            pl.BlockSpec(cblk.shape, const),
            pl.BlockSpec(lam.shape, const),
            pl.BlockSpec(d.shape, const),
            pl.BlockSpec(glu_w.shape, const),
            pl.BlockSpec(glu_b.shape, const),
        ],
        out_specs=pl.BlockSpec((rows, width), lambda i: (i, 0)),
        scratch_shapes=[pltpu.VMEM((rows, 2 * ns), F32), pltpu.VMEM((2, bsz, ns), F32)],
        compiler_params=_cparams("arbitrary"),
        name="s5",
    )(u2, bblk, cblk, lam, d, glu_w, glu_b)


def _s5_params(lam_re, lam_im, log_dt, b_re, b_im, c_re, c_im, d_skip):
    g, n = lam_re.shape
    p = b_re.shape[-1]
    lam = lax.complex(lam_re.astype(F32), lam_im.astype(F32))
    step = jnp.exp(log_dt.astype(F32))[:, None]
    lam_bar = jnp.exp(lam * step)
    b_bar = ((lam_bar - 1.0) / lam)[..., None] * lax.complex(b_re.astype(F32), b_im.astype(F32))
    eye = jnp.eye(g, dtype=F32)
    b_re_blk = jnp.einsum('gnp,gh->gphn', b_bar.real, eye).reshape(g * p, g * n)
    b_im_blk = jnp.einsum('gnp,gh->gphn', b_bar.imag, eye).reshape(g * p, g * n)
    bblk = jnp.concatenate([b_re_blk, b_im_blk], axis=1)
    c_re_blk = jnp.einsum('gpn,gh->gnhp', c_re.astype(F32), eye).reshape(g * n, g * p)
    c_im_blk = jnp.einsum('gpn,gh->gnhp', c_im.astype(F32), eye).reshape(g * n, g * p)
    cblk = jnp.concatenate([c_re_blk, -c_im_blk], axis=0)
    lam2 = jnp.stack([lam_bar.real.reshape(g * n), lam_bar.imag.reshape(g * n)])
    return bblk.astype(BF16), cblk.astype(BF16), lam2, d_skip.astype(F32).reshape(1, g * p)


def _moba_kernel(qt_ref, k_ref, vt_ref, o_ref, km_ref, qs_ref, bias_ref, acc_ref, s_ref, *, nh, nb):
    i = pl.program_id(1)
    blk = MOBA_BLOCK
    hd = HEAD_DIM

    @pl.when(i == 0)
    def _():
        avg = jnp.full((1, blk), 1.0 / blk, F32)
        for h in range(nh):
            for n in range(nb):
                km_ref[h, n:n + 1, :] = _dot(avg, k_ref[0, h, n], precision=HI)

    n_iota = _iota((nb, blk), 0)
    kpos = _iota((blk, blk), 0)
    qpos = _iota((blk, blk), 1)
    causal = kpos <= qpos

    heads = range(nh)

    qts = [qt_ref[0, h, 0] * (hd ** -0.5) for h in heads]
    qtb = [qts[h].astype(BF16) for h in heads]
    s_own = [_dot(k_ref[0, h, i].astype(BF16), qtb[h]) for h in heads]
    gates = [_dot(km_ref[h], qts[h], precision=HI) for h in heads]
    vt_own = [vt_ref[0, h, i].astype(BF16) for h in heads]
    biases = []
    for h in heads:
        gate = jnp.where(n_iota < i, gates[h], -jnp.inf)
        rank = jnp.zeros((nb, blk), I32)
        for m in range(nb):
            gm = gate[m:m + 1, :]
            ahead = (gm > gate) | ((gm == gate) & (m < n_iota))
            rank = rank + ahead.astype(I32)
        biases.append(jnp.where((n_iota < i) & (rank < MOBA_TOPK), 0.0, NEG_BIG))
    m_own, l_own, acc_own = [], [], []
    for h in heads:
        s = jnp.where(causal, s_own[h], NEG_BIG)
        m0 = jnp.max(s, axis=0, keepdims=True)
        p = jnp.exp(s - m0)
        m_own.append(m0)
        l_own.append(jnp.sum(p, axis=0, keepdims=True))
        acc_own.append(_dot(vt_own[h], p.astype(BF16)))
    s_first = [_dot(k_ref[0, h, 0].astype(BF16), qtb[h]) for h in heads]
    for h in heads:
        bias_ref[h] = biases[h]
        qs_ref[h] = qtb[h]
        acc_ref[h * hd:(h + 1) * hd, :] = acc_own[h]
        s_ref[h] = s_first[h]

    def body(j, carry):
        ms, ls = carry
        ss = [s_ref[h] + bias_ref[h, pl.ds(j, 1), :] for h in heads]
        nxt = jnp.minimum(j + 1, i - 1)
        s_next = [_dot(k_ref[0, h, nxt].astype(BF16), qs_ref[h]) for h in heads]
        vts = [vt_ref[0, h, j].astype(BF16) for h in heads]
        accs = [acc_ref[h * hd:(h + 1) * hd, :] for h in heads]
        new_ms, new_ls, new_accs = [], [], []
        for h in heads:
            m_new = jnp.maximum(ms[h], jnp.max(ss[h], axis=0, keepdims=True))
            p = jnp.exp(ss[h] - m_new)
            corr = jnp.exp(ms[h] - m_new)
            new_ms.append(m_new)
            new_ls.append(ls[h] * corr + jnp.sum(p, axis=0, keepdims=True))
            new_accs.append(accs[h] * corr + _dot(vts[h], p.astype(BF16)))
        for h in heads:
            acc_ref[h * hd:(h + 1) * hd, :] = new_accs[h]
            s_ref[h] = s_next[h]
        return tuple(new_ms), tuple(new_ls)

    _, ls = lax.fori_loop(0, i, body, (tuple(m_own), tuple(l_own)))
    for h in heads:
        rows = slice(h * hd, (h + 1) * hd)
        acc_ref[rows, :] = acc_ref[rows, :] / ls[h]
    eye = (kpos == qpos).astype(F32)
    o_ref[0] = _dot_nt(eye, acc_ref[...], precision=HI)


def _moba(qt, k, vt, *, seqlen):
    bsz, nh, nb, hd, blk = qt.shape
    return pl.pallas_call(
        functools.partial(_moba_kernel, nh=nh, nb=nb),
        out_shape=jax.ShapeDtypeStruct((bsz, seqlen, nh * hd), F32),
        grid=(bsz, nb),
        in_specs=[
            pl.BlockSpec((1, nh, 1, hd, blk), lambda b, i: (b, 0, i, 0, 0)),
            pl.BlockSpec((1, nh, nb, blk, hd), lambda b, i: (b, 0, 0, 0, 0)),
            pl.BlockSpec((1, nh, nb, hd, blk), lambda b, i: (b, 0, 0, 0, 0)),
        ],
        out_specs=pl.BlockSpec((1, blk, nh * hd), lambda b, i: (b, i, 0)),
        scratch_shapes=[pltpu.VMEM((nh, nb, hd), F32), pltpu.VMEM((nh, hd, blk), BF16),
                        pltpu.VMEM((nh, nb, blk), F32), pltpu.VMEM((nh * hd, blk), F32),
                        pltpu.VMEM((nh, blk, blk), F32)],
        compiler_params=_cparams("parallel", "arbitrary"),
        name="moba",
    )(qt, k, vt)


def _gdn_kernel(q_ref, k_ref, v_ref, z_ref, ab_ref, cw_ref, al_ref, dtb_ref, nw_ref, o_ref,
                g_ref, beta_ref, us_ref, ws_ref, qe_ref, kd_ref, at_ref, el_ref, s_ref, *, nh, seqlen):
    npair = nh // 2
    ch = GDN_CHUNK
    hd = HEAD_DIM
    gw = nh * hd
    pw = 2 * hd
    pairs = range(npair)

    rr = _iota((pw, pw), 0)
    cc = _iota((pw, pw), 1)
    same_head = (rr // hd) == (cc // hd)
    mask_incl = same_head & ((rr % ch) >= (cc % ch))
    mask_strict = same_head & ((rr % ch) > (cc % ch))
    eye = (rr == cc).astype(F32)
    tril_c = (_iota((ch, ch), 0) >= _iota((ch, ch), 1)).astype(F32)
    lane_c = _iota((ch, LANE), 1)
    first = lane_c < hd
    row_col = _iota((pw, 1), 0)
    lane_row = _iota((1, pw), 1)

    def stack(x):
        return jnp.concatenate([jnp.where(first, x, 0.0), jnp.where(first, 0.0, x)], axis=0)

    def fold(x):
        return x[:ch] + x[ch:]

    def head_sumsq(y):
        y2 = y * y
        sa = jnp.sum(jnp.where(first, y2, 0.0), axis=1, keepdims=True)
        sb = jnp.sum(jnp.where(first, 0.0, y2), axis=1, keepdims=True)
        return jnp.where(first, sa, sb)

    a = ab_ref[0]
    ea = a + dtb_ref[...]
    g_ref[...] = -jnp.exp(al_ref[...]) * (jnp.maximum(ea, 0.0) + jnp.log1p(jnp.exp(-jnp.abs(ea))))
    beta_ref[...] = jax.nn.sigmoid(a)

    def conv_silu(src, t, p, c, r0):
        lanes = slice(p * pw, (p + 1) * pw)
        h0 = pl.multiple_of(jnp.maximum(r0 - 8, 0), 8)
        hist = jnp.where(c > 0, src[0, pl.ds(h0, 8), lanes], 0.0)
        x = jnp.concatenate([hist, src[0, pl.ds(r0, ch), lanes]], axis=0)
        c0 = t * gw + p * pw
        acc = x * cw_ref[3:4, c0:c0 + pw]
        for tap in range(3):
            acc = acc + pltpu.roll(x, 3 - tap, 0) * cw_ref[tap:tap + 1, c0:c0 + pw]
        y = acc[8:]
        return y * jax.nn.sigmoid(y)

    def prep(cc, carry):
        chunks = [cc * GDN_PREP_CHUNKS + u for u in range(GDN_PREP_CHUNKS)]
        r0s = [pl.multiple_of(c * ch, ch) for c in chunks]
        gcs = [_dot(tril_c, g_ref[pl.ds(r0, ch), :], precision=HI) for r0 in r0s]
        bch = [beta_ref[pl.ds(r0, ch), :] for r0 in r0s]
        streams = [(u, p) for u in range(GDN_PREP_CHUNKS) for p in pairs]
        ns = range(len(streams))
        qs, ks, vs, gcol, bcol, decay = [], [], [], [], [], []
        for u, p in streams:
            q = conv_silu(q_ref, 0, p, chunks[u], r0s[u])
            k = conv_silu(k_ref, 1, p, chunks[u], r0s[u])
            v = conv_silu(v_ref, 2, p, chunks[u], r0s[u])
            q = q * (lax.rsqrt(head_sumsq(q) + RMS_EPS) * (hd ** -0.5))
            k = k * lax.rsqrt(head_sumsq(k) + RMS_EPS)
            qs.append(stack(q))
            ks.append(stack(k))
            vs.append(stack(v))
            ha, hb = 2 * p, 2 * p + 1
            gst = jnp.concatenate([jnp.where(lane_c == ha, gcs[u], 0.0), jnp.where(lane_c == hb, gcs[u], 0.0)], axis=0)
            bst = jnp.concatenate([jnp.where(lane_c == nh + ha, bch[u], 0.0),
                                   jnp.where(lane_c == nh + hb, bch[u], 0.0)], axis=0)
            gc = jnp.sum(gst, axis=1, keepdims=True)
            gcol.append(gc)
            bcol.append(jnp.sum(bst, axis=1, keepdims=True))
            gmat = jnp.broadcast_to(gc, (pw, pw))
            decay.append(jnp.where(mask_incl, jnp.exp(jnp.where(mask_incl, gmat - gmat.T, 0.0)), 0.0))
        kbs = [ks[n] * bcol[n] for n in ns]
        ksb = [ks[n].astype(BF16) for n in ns]
        pm = [jnp.where(mask_strict, _dot_nt(kbs[n].astype(BF16), ksb[n]) * decay[n], 0.0) for n in ns]
        tm = [eye - pm[n] for n in ns]
        pmb = [pm[n].astype(BF16) for n in ns]
        pm = [_dot(pmb[n], pmb[n]) for n in ns]
        for _ in range(4):
            pmb = [pm[n].astype(BF16) for n in ns]
            tm = [tm[n] + _dot(tm[n].astype(BF16), pmb[n]) for n in ns]
            pm = [_dot(pmb[n], pmb[n]) for n in ns]
        tm = [tm[n] + _dot(tm[n].astype(BF16), pm[n].astype(BF16)) for n in ns]
        out = []
        for n in ns:
            eg = jnp.exp(gcol[n])
            rhs = jnp.concatenate([vs[n] * bcol[n], kbs[n] * eg], axis=1).astype(BF16)
            sol = _dot(tm[n].astype(BF16), rhs)
            attn = jnp.where(mask_incl, _dot_nt(qs[n].astype(BF16), ksb[n]) * decay[n], 0.0)
            g_a = gcol[n][ch - 1:ch]
            g_b = gcol[n][2 * ch - 1:2 * ch]
            glast = jnp.where(row_col < ch, g_a, g_b)
            e_last = jnp.where(lane_row < hd, jnp.exp(g_a), jnp.exp(g_b))
            out.append((fold(sol[:, :pw]), fold(sol[:, pw:]).astype(BF16), fold(qs[n] * eg).astype(BF16),
                        fold(ks[n] * jnp.exp(glast - gcol[n])).astype(BF16), fold(attn).astype(BF16),
                        jnp.broadcast_to(e_last, (8, pw))))
        for n, (u, p) in enumerate(streams):
            rows = pl.ds(r0s[u], ch)
            us_ref[p, rows, :], ws_ref[p, rows, :], qe_ref[p, rows, :], kd_ref[p, rows, :], at_ref[p, rows, :] = out[n][:5]
            el_ref[p, pl.ds(pl.multiple_of(chunks[u] * 8, 8), 8), :] = out[n][5]
        return carry

    lax.fori_loop(0, seqlen // (ch * GDN_PREP_CHUNKS), prep, 0)

    s_ref[...] = jnp.zeros_like(s_ref)

    def scan(c, carry):
        r0 = pl.multiple_of(c * ch, ch)
        rows = pl.ds(r0, ch)
        lanes = [slice(p * pw, (p + 1) * pw) for p in pairs]
        states = [s_ref[p] for p in pairs]
        wq = [jnp.concatenate([ws_ref[p, rows, :], qe_ref[p, rows, :]], axis=0) for p in pairs]
        us = [us_ref[p, rows, :] for p in pairs]
        at = [at_ref[p, rows, :] for p in pairs]
        kd = [kd_ref[p, rows, :] for p in pairs]
        el = [el_ref[p, pl.ds(c * 8, 1), :] for p in pairs]
        zs = [z_ref[0, rows, lanes[p]] for p in pairs]
        wqs = [_dot(wq[p], states[p].astype(BF16)) for p in pairs]
        v_new = [us[p] - wqs[p][:ch] for p in pairs]
        upd = [_dot_tn(kd[p], v_new[p].astype(BF16)) for p in pairs]
        new_states = [states[p] * el[p] + jnp.where(same_head, upd[p], 0.0) for p in pairs]
        outs = []
        for p in pairs:
            o = wqs[p][ch:] + _dot(at[p], stack(v_new[p]).astype(BF16))
            ms = head_sumsq(o) * (1.0 / hd)
            outs.append(o * lax.rsqrt(ms + RMS_EPS) * nw_ref[...] * (zs[p] * jax.nn.sigmoid(zs[p])))
        for p in pairs:
            s_ref[p] = new_states[p]
            o_ref[0, rows, lanes[p]] = outs[p]
        return carry

    lax.fori_loop(0, seqlen // ch, scan, 0)


def _gdn_params(conv_w, a_log, dt_bias, norm_w):
    nh = a_log.shape[0]
    lane_pad = lambda v: jnp.pad(v.astype(F32), (0, LANE - nh)).reshape(1, LANE)
    return conv_w.astype(F32), lane_pad(a_log), lane_pad(dt_bias), jnp.tile(norm_w.astype(F32), 2).reshape(1, LANE)


def _gdn(q, k, v, z, ab, conv_w, al, dtb, nw):
    bsz, seqlen, gw = q.shape
    nh = gw // HEAD_DIM
    npair = nh // 2
    seq_spec = pl.BlockSpec((1, seqlen, gw), lambda b: (b, 0, 0))
    const = lambda b: (0, 0)
    return pl.pallas_call(
        functools.partial(_gdn_kernel, nh=nh, seqlen=seqlen),
        out_shape=jax.ShapeDtypeStruct((bsz, seqlen, gw), F32),
        grid=(bsz,),
        in_specs=[seq_spec, seq_spec, seq_spec, seq_spec,
                  pl.BlockSpec((1, seqlen, LANE), lambda b: (b, 0, 0)),
                  pl.BlockSpec(conv_w.shape, const),
                  pl.BlockSpec(al.shape, const),
                  pl.BlockSpec(dtb.shape, const),
                  pl.BlockSpec(nw.shape, const)],
        out_specs=seq_spec,
        scratch_shapes=[pltpu.VMEM((seqlen, LANE), F32)] * 2
        + [pltpu.VMEM((npair, seqlen, LANE), F32)]
        + [pltpu.VMEM((npair, seqlen, LANE), BF16)] * 4
        + [pltpu.VMEM((npair, seqlen // GDN_CHUNK * 8, LANE), F32)]
        + [pltpu.VMEM((npair, LANE, LANE), F32)],
        compiler_params=_cparams("parallel"),
        name="gdn",
    )(q, k, v, z, ab, conv_w, al, dtb, nw)


def _route(logits):
    n_exp = logits.shape[0]
    eidx = _iota(logits.shape, 0)
    m1 = jnp.max(logits, axis=0, keepdims=True)
    i1 = jnp.min(jnp.where(logits == m1, eidx, n_exp), axis=0, keepdims=True)
    lg2 = jnp.where(eidx == i1, -jnp.inf, logits)
    m2 = jnp.max(lg2, axis=0, keepdims=True)
    i2 = jnp.min(jnp.where(lg2 == m2, eidx, n_exp), axis=0, keepdims=True)
    e2 = jnp.exp(m2 - m1)
    g1 = 1.0 / (1.0 + e2)
    g2 = e2 / (1.0 + e2)
    gates = jnp.where(eidx == i1, g1, 0.0) + jnp.where(eidx == i2, g2, 0.0)
    mask = jnp.where((eidx == i1) | (eidx == i2), 1.0, 0.0)
    return jnp.concatenate([gates, mask], axis=0)


def _outproj_kernel(x_ref, ys_ref, ym_ref, yg_ref, w_ref, g_ref, b_ref, *rest, alpha, s5w, mw, n_exp):
    y = jnp.concatenate([ys_ref[...], ym_ref[0], yg_ref[0]], axis=-1).astype(BF16)
    x1 = _layer_norm(alpha * x_ref[...] + _dot(y, w_ref[...]), g_ref[...], b_ref[...])
    if n_exp:
        rw_ref, o_ref, r_ref, ot_ref = rest
        o_ref[...] = x1
        r_ref[...] = _route(_dot_nt(rw_ref[...], x1, precision=HI))
        for s in range(TOKEN_TILE_ROWS):
            ot_ref[:, s, :] = x1[:, s * LANE:(s + 1) * LANE]
    else:
        (o_ref,) = rest
        o_ref[...] = x1


def _outproj(x2, ys, ym, yg, w, g, b, router_w, *, bsz, seqlen, alpha, n_exp):
    tl = OUTPROJ_ROWS
    nt = seqlen // tl
    t, d = x2.shape
    s5w = ys.shape[1] // bsz
    mw = ym.shape[2]
    gw = yg.shape[2]
    const = lambda bi, j: (0, 0)
    row_spec = pl.BlockSpec((tl, d), lambda bi, j: (bi * nt + j, 0))
    in_specs = [row_spec,
                pl.BlockSpec((tl, s5w), lambda bi, j: (j, bi)),
                pl.BlockSpec((1, tl, mw), lambda bi, j: (bi, j, 0)),
                pl.BlockSpec((1, tl, gw), lambda bi, j: (bi, j, 0)),
                pl.BlockSpec(w.shape, const), pl.BlockSpec(g.shape, const), pl.BlockSpec(b.shape, const)]
    args = [x2, ys, ym, yg, w, g, b]
    out_shape = jax.ShapeDtypeStruct((t, d), F32)
    out_specs = row_spec
    if n_exp:
        in_specs.append(pl.BlockSpec(router_w.shape, const))
        args.append(router_w)
        assert d == TOKEN_TILE_ROWS * LANE
        out_shape = (out_shape, jax.ShapeDtypeStruct((2 * n_exp, t), F32),
                     jax.ShapeDtypeStruct((t, TOKEN_TILE_ROWS, LANE), F32))
        out_specs = (row_spec, pl.BlockSpec((2 * n_exp, tl), lambda bi, j: (0, bi * nt + j)),
                     pl.BlockSpec((tl, TOKEN_TILE_ROWS, LANE), lambda bi, j: (bi * nt + j, 0, 0)))
    return pl.pallas_call(
        functools.partial(_outproj_kernel, alpha=alpha, s5w=s5w, mw=mw, n_exp=n_exp),
        out_shape=out_shape,
        grid=(bsz, nt),
        in_specs=in_specs,
        out_specs=out_specs,
        compiler_params=_cparams("parallel", "arbitrary"),
        name="outproj_route" if n_exp else "outproj",
    )(*args)


def _swiglu_part(xb, wg_ref, wu_ref, wd_ref):
    out = None
    for c0 in range(0, wg_ref.shape[1], FFN_CHUNK):
        cols = slice(c0, c0 + FFN_CHUNK)
        hg = _dot(xb, wg_ref[:, cols].astype(BF16))
        hu = _dot(xb, wu_ref[:, cols].astype(BF16))
        h = (hg * jax.nn.sigmoid(hg)) * hu
        part = _dot(h.astype(BF16), wd_ref[cols, :].astype(BF16))
        out = part if out is None else out + part
    return out


def _ffn_dense_kernel(x_ref, wg_ref, wu_ref, wd_ref, g_ref, b_ref, o_ref, xb_ref, acc_ref, *, alpha):
    @pl.when(pl.program_id(1) == 0)
    def _():
        xb_ref[...] = x_ref[...].astype(BF16)
        acc_ref[...] = jnp.zeros_like(acc_ref)

    acc_ref[...] += _swiglu_part(xb_ref[...], wg_ref.at[0], wu_ref.at[0], wd_ref.at[0])

    @pl.when(pl.program_id(1) == pl.num_programs(1) - 1)
    def _():
        o_ref[...] = _layer_norm(alpha * x_ref[...] + acc_ref[...], g_ref[...], b_ref[...])


def _ffn_dense(x2, wg, wu, wd, layer, g, b, *, alpha):
    t, d = x2.shape
    ff = wg.shape[2]
    tm, tf = FFN_ROWS, FFN_COLS
    const = lambda i, f: (0, 0)
    return pl.pallas_call(
        functools.partial(_ffn_dense_kernel, alpha=alpha),
        out_shape=jax.ShapeDtypeStruct((t, d), F32),
        grid=(t // tm, ff // tf),
        in_specs=[pl.BlockSpec((tm, d), lambda i, f: (i, 0)),
                  pl.BlockSpec((1, d, tf), lambda i, f: (layer, 0, f)),
                  pl.BlockSpec((1, d, tf), lambda i, f: (layer, 0, f)),
                  pl.BlockSpec((1, tf, d), lambda i, f: (layer, f, 0)),
                  pl.BlockSpec(g.shape, const), pl.BlockSpec(b.shape, const)],
        out_specs=pl.BlockSpec((tm, d), lambda i, f: (i, 0)),
        scratch_shapes=[pltpu.VMEM((tm, d), BF16), pltpu.VMEM((tm, d), F32)],
        compiler_params=_cparams("parallel", "arbitrary"),
        name="ffn_dense",
    )(x2, wg, wu, wd, g, b)


def _ffn_moe_kernel(tile_ref, exp_ref, lo_ref, hi_ref, first_ref, x_ref, wg_ref, wu_ref, wd_ref, o_ref,
                    xt_ref, xb_ref, acc_ref, wgb_ref, wub_ref, wdb_ref):
    k = pl.program_id(0)
    f = pl.program_id(1)
    lo, hi = lo_ref[k], hi_ref[k]
    sub = MOE_SUB_ROWS
    d = o_ref.shape[1]

    nsub = o_ref.shape[0] // sub
    touched = sum(((lo < (sb + 1) * sub) & (hi > sb * sub)).astype(I32) for sb in range(nsub))
    whole = touched > nsub // 2

    def each_live_sub_block(fn):
        for sb in range(nsub):
            live = whole | ((lo < (sb + 1) * sub) & (hi > sb * sub))
            pl.when(live)(functools.partial(fn, sb, slice(sb * sub, (sb + 1) * sub)))

    @pl.when(hi > lo)
    def _():
        @pl.when((f == 0) & (first_ref[k] > 0))
        def _():
            o_ref[...] = jnp.zeros_like(o_ref)

        def load_rows(sb, rows):
            for s in range(TOKEN_TILE_ROWS):
                xt_ref[rows, s * LANE:(s + 1) * LANE] = x_ref[rows, s, :]
            xb_ref[rows, :] = xt_ref[rows, :].astype(BF16)
            acc_ref[rows, :] = jnp.zeros((sub, d), F32)

        pl.when(f == 0)(functools.partial(each_live_sub_block, load_rows))

        @pl.when(whole)
        def _():
            acc_ref[...] += _swiglu_part(xb_ref[...], wg_ref.at[0, 0], wu_ref.at[0, 0], wd_ref.at[0, 0])

        @pl.when(jnp.logical_not(whole))
        def _():
            wgb_ref[...] = wg_ref[0, 0].astype(BF16)
            wub_ref[...] = wu_ref[0, 0].astype(BF16)
            wdb_ref[...] = wd_ref[0, 0].astype(BF16)

            def compute(sb, rows):
                acc_ref[rows, :] += _swiglu_part(xb_ref[rows, :], wgb_ref, wub_ref, wdb_ref)

            each_live_sub_block(compute)

        def emit(sb, rows):
            row = _iota((sub, d), 0) + sb * sub
            o_ref[rows, :] = jnp.where((row >= lo) & (row < hi), acc_ref[rows, :], o_ref[rows, :])

        pl.when(f == pl.num_programs(1) - 1)(functools.partial(each_live_sub_block, emit))


def _ffn_moe(items, xs, wg, wu, wd, layer):
    p_rows = xs.shape[0]
    d = wg.shape[2]
    ff = wg.shape[3]
    tm, tf = MOE_ROWS, FFN_COLS
    nf = ff // tf
    n_items = items[0].shape[0]

    def fsel(k, f, lo, hi):
        return jnp.where(hi[k] > lo[k], f, nf - 1)

    grid_spec = pltpu.PrefetchScalarGridSpec(
        num_scalar_prefetch=5,
        grid=(n_items, nf),
        in_specs=[pl.BlockSpec((tm, TOKEN_TILE_ROWS, LANE), lambda k, f, tile, exp, lo, hi, first: (tile[k], 0, 0)),
                  pl.BlockSpec((1, 1, d, tf),
                               lambda k, f, tile, exp, lo, hi, first: (layer, exp[k], 0, fsel(k, f, lo, hi))),
                  pl.BlockSpec((1, 1, d, tf),
                               lambda k, f, tile, exp, lo, hi, first: (layer, exp[k], 0, fsel(k, f, lo, hi))),
                  pl.BlockSpec((1, 1, tf, d),
                               lambda k, f, tile, exp, lo, hi, first: (layer, exp[k], fsel(k, f, lo, hi), 0))],
        out_specs=pl.BlockSpec((tm, d), lambda k, f, tile, exp, lo, hi, first: (tile[k], 0)),
        scratch_shapes=[pltpu.VMEM((tm, d), F32), pltpu.VMEM((tm, d), BF16), pltpu.VMEM((tm, d), F32),
                        pltpu.VMEM((d, tf), BF16), pltpu.VMEM((d, tf), BF16), pltpu.VMEM((tf, d), BF16)],
    )
    return pl.pallas_call(
        _ffn_moe_kernel,
        out_shape=jax.ShapeDtypeStruct((p_rows, d), F32),
        grid_spec=grid_spec,
        compiler_params=_cparams("arbitrary", "arbitrary"),
        name="ffn_moe",
    )(*items, xs, wg, wu, wd)


def _tile_copy(src_ref, src_row, dst_ref, dst_row, sem):
    return pltpu.make_async_copy(src_ref.at[pl.ds(src_row, 1)], dst_ref.at[pl.ds(dst_row, 1)], sem)


def _dispatch_kernel(p0_ref, p1_ref, x_ref, xs_hbm, sem, *, rows):
    def start(r, carry):
        _tile_copy(x_ref, r, xs_hbm, p0_ref[0, 0, r], sem.at[0]).start(priority=0)
        _tile_copy(x_ref, r, xs_hbm, p1_ref[0, 0, r], sem.at[1]).start(priority=1)
        return carry

    lax.fori_loop(0, rows, start, 0, unroll=8)

    def wait(r, carry):
        _tile_copy(x_ref, r, xs_hbm, 0, sem.at[0]).wait()
        _tile_copy(x_ref, r, xs_hbm, 0, sem.at[1]).wait()
        return carry

    lax.fori_loop(0, rows, wait, 0, unroll=8)


def _dispatch(p0, p1, xt):
    nt, _, rows = p0.shape
    t = xt.shape[0]
    idx_spec = pl.BlockSpec((1, 1, rows), lambda i: (i, 0, 0), memory_space=pltpu.SMEM)
    return pl.pallas_call(
        functools.partial(_dispatch_kernel, rows=rows),
        out_shape=jax.ShapeDtypeStruct((t * TOP_K, TOKEN_TILE_ROWS, LANE), xt.dtype),
        grid=(nt,),
        in_specs=[idx_spec, idx_spec, pl.BlockSpec((rows, TOKEN_TILE_ROWS, LANE), lambda i: (i, 0, 0))],
        out_specs=pl.BlockSpec(memory_space=pl.ANY),
        scratch_shapes=[pltpu.SemaphoreType.DMA((2,))],
        compiler_params=_cparams("arbitrary"),
        name="moe_dispatch",
    )(p0, p1, xt)


def _combine_kernel(p0_ref, p1_ref, ys_hbm, x_ref, gt_ref, g_ref, b_ref, o_ref, y0_ref, y1_ref, sem, *, rows, alpha):
    def start(r, carry):
        _tile_copy(ys_hbm, p0_ref[0, 0, r], y0_ref, r, sem.at[0]).start(priority=0)
        _tile_copy(ys_hbm, p1_ref[0, 0, r], y1_ref, r, sem.at[1]).start(priority=1)
        return carry

    lax.fori_loop(0, rows, start, 0, unroll=8)

    def wait(r, carry):
        _tile_copy(ys_hbm, 0, y0_ref, r, sem.at[0]).wait()
        _tile_copy(ys_hbm, 0, y1_ref, r, sem.at[1]).wait()
        return carry

    lax.fori_loop(0, rows, wait, 0, unroll=8)
    gt = gt_ref[...]
    f = gt[:, 0:1] * y0_ref[...] + gt[:, 1:2] * y1_ref[...]
    o_ref[...] = _layer_norm(alpha * x_ref[...] + f, g_ref[...], b_ref[...])


def _combine(p0, p1, ys, x2, gates, g, b, *, alpha):
    nt, _, rows = p0.shape
    t, d = x2.shape
    const = lambda i: (0, 0)
    idx_spec = pl.BlockSpec((1, 1, rows), lambda i: (i, 0, 0), memory_space=pltpu.SMEM)
    tile_buf = pltpu.VMEM((rows, d), F32)
    return pl.pallas_call(
        functools.partial(_combine_kernel, rows=rows, alpha=alpha),
        out_shape=jax.ShapeDtypeStruct((t, d), F32),
        grid=(nt,),
        in_specs=[idx_spec, idx_spec,
                  pl.BlockSpec(memory_space=pl.ANY),
                  pl.BlockSpec((rows, d), lambda i: (i, 0)),
                  pl.BlockSpec((rows, LANE), lambda i: (i, 0)),
                  pl.BlockSpec(g.shape, const), pl.BlockSpec(b.shape, const)],
        out_specs=pl.BlockSpec((rows, d), lambda i: (i, 0)),
        scratch_shapes=[tile_buf, tile_buf, pltpu.SemaphoreType.DMA((2,))],
        compiler_params=_cparams("arbitrary"),
        name="moe_combine",
    )(p0, p1, ys, x2, gates, g, b)


def _dispatch_plan(route, n_exp):
    t = route.shape[1]
    tm = MOE_ROWS
    gates = route[:n_exp].T
    sel = route[n_exp:].T > 0.5
    seli = sel.astype(I32)
    rank = jnp.cumsum(seli, axis=0) - seli
    counts = jnp.sum(seli, axis=0)
    ends = jnp.cumsum(counts)
    starts = ends - counts
    pos = starts[None, :] + rank
    order = jnp.cumsum(seli, axis=1) - seli
    first = sel & (order == 0)
    second = sel & (order == 1)
    pick = lambda m, v: jnp.sum(jnp.where(m, v, 0), axis=1)
    p0, p1 = pick(first, pos), pick(second, pos)
    g01 = jnp.stack([pick(first, gates), pick(second, gates)], axis=1)
    g01 = jnp.pad(g01, ((0, 0), (0, LANE - TOP_K)))
    n_tiles = t * TOP_K // tm
    n_items = n_tiles + n_exp - 1
    tile_lo = jnp.arange(n_tiles, dtype=I32)[:, None] * tm
    ov_lo = jnp.maximum(starts[None, :], tile_lo)
    ov_hi = jnp.minimum(ends[None, :], tile_lo + tm)
    live = (ov_hi > ov_lo).reshape(-1)
    flat = jnp.arange(n_tiles * n_exp, dtype=I32)
    take = jnp.argsort(jnp.where(live, flat, n_tiles * n_exp))[:n_items]
    n_live = jnp.sum(live.astype(I32))
    idle = jnp.arange(n_items, dtype=I32) >= n_live
    last = take[jnp.maximum(n_live - 1, 0)]
    take = jnp.where(idle, last, take)
    item_tile = take // n_exp
    item_exp = take % n_exp
    item_lo = jnp.where(idle, 0, (ov_lo - tile_lo).reshape(-1)[take])
    item_hi = jnp.where(idle, 0, (ov_hi - tile_lo).reshape(-1)[take])
    prev_tile = jnp.concatenate([jnp.full((1,), -1, I32), item_tile[:-1]])
    item_first = (item_tile != prev_tile).astype(I32)
    items = tuple(v.astype(I32) for v in (item_tile, item_exp, item_lo, item_hi, item_first))
    shape3 = lambda v: v.astype(I32).reshape(-1, 1, GATHER_ROWS)
    return shape3(p0), shape3(p1), g01, items


def kernel(x, w_in, w_out, s5_lam_re, s5_lam_im, s5_log_dt, s5_b_re, s5_b_im, s5_c_re, s5_c_im, s5_d, s5_glu_w,
           s5_glu_b, gdn_conv_w, gdn_a_log, gdn_dt_bias, gdn_norm_w, ln1_g, ln1_b, ln2_g, ln2_b, ffn_w_gate,
           ffn_w_up, ffn_w_down, moe_router, moe_w_gate, moe_w_up, moe_w_down):
    bsz, seqlen, d = x.shape
    depth = w_in.shape[0]
    s5w = s5_glu_w.shape[1]
    gw = gdn_a_log.shape[1] * HEAD_DIM
    mw = w_out.shape[1] - s5w - gw
    n_exp = moe_router.shape[2]
    alpha = (2 * depth) ** 0.25
    row = lambda v: v.astype(F32).reshape(1, -1)

    x2 = x.reshape(bsz * seqlen, d)
    for l in range(depth):
        w, wt = _pack_inproj(w_in[l], s5w, mw, gw)
        u, mqt, mk, mvt, gq, gk, gv, gz, ab = _inproj(x2, w, wt, bsz=bsz, seqlen=seqlen, s5w=s5w, mw=mw, gw=gw)
        bblk, cblk, lam, dskip = _s5_params(s5_lam_re[l], s5_lam_im[l], s5_log_dt[l], s5_b_re[l], s5_b_im[l],
                                            s5_c_re[l], s5_c_im[l], s5_d[l])
        y_s5 = _s5(u.reshape(seqlen * bsz, s5w), bblk, cblk, lam, dskip, s5_glu_w[l].astype(BF16),
                   row(s5_glu_b[l]), bsz=bsz, seqlen=seqlen).reshape(seqlen, bsz * s5w)
        y_moba = _moba(mqt, mk, mvt, seqlen=seqlen)
        y_gdn = _gdn(gq, gk, gv, gz, ab, *_gdn_params(gdn_conv_w[l], gdn_a_log[l], gdn_dt_bias[l], gdn_norm_w[l]))
        routed = l % 2 == 1
        router_w = moe_router[l // 2].astype(F32).T if routed else None
        res = _outproj(x2, y_s5, y_moba, y_gdn, w_out[l].astype(BF16), row(ln1_g[l]), row(ln1_b[l]), router_w,
                       bsz=bsz, seqlen=seqlen, alpha=alpha, n_exp=n_exp if routed else 0)
        if routed:
            x1, route, x1t = res
            p0, p1, g01, items = _dispatch_plan(route, n_exp)
            xs = _dispatch(p0, p1, x1t)
            ys = _ffn_moe(items, xs, moe_w_gate, moe_w_up, moe_w_down, l // 2)
            x2 = _combine(p0, p1, ys, x1, g01, row(ln2_g[l]), row(ln2_b[l]), alpha=alpha)
        else:
            x2 = _ffn_dense(res, ffn_w_gate, ffn_w_up, ffn_w_down, l // 2, row(ln2_g[l]), row(ln2_b[l]), alpha=alpha)
    return x2.reshape(bsz, seqlen, d)
```

```python
import functools

import jax
import jax.numpy as jnp
from jax import lax
from jax.experimental import pallas as pl
from jax.experimental.pallas import tpu as pltpu

F32 = jnp.float32
BF16 = jnp.bfloat16
I32 = jnp.int32
HI = lax.Precision.HIGHEST

HEAD_DIM = 64
MOBA_BLOCK = 256
MOBA_TOPK = 3
GDN_CHUNK = 64
S5_PARTS = 2
GDN_PREP_CHUNKS = 4
TOP_K = 2
LN_EPS = 1e-5
RMS_EPS = 1e-6
NEG_BIG = -1e30

V7X_VMEM_LIMIT_BYTES = 56 * 1024 * 1024
LANE = 128
TOKEN_TILE_ROWS = 8

INPROJ_ROWS = 1024
OUTPROJ_ROWS = 1024
FFN_ROWS = 1024
MOE_ROWS = 1024
MOE_SUB_ROWS = 256
FFN_COLS = 512
FFN_CHUNK = 256
GATHER_ROWS = 256


def _cparams(*sem):
    return pltpu.CompilerParams(dimension_semantics=sem, vmem_limit_bytes=V7X_VMEM_LIMIT_BYTES)


def _dot(a, b, precision=None):
    return jnp.dot(a, b, preferred_element_type=F32, precision=precision)


def _dot_nt(a, b, precision=None):
    return lax.dot_general(a, b, (((1,), (1,)), ((), ())), preferred_element_type=F32, precision=precision)


def _dot_tn(a, b, precision=None):
    return lax.dot_general(a, b, (((0,), (0,)), ((), ())), preferred_element_type=F32, precision=precision)


def _iota(shape, dim):
    return lax.broadcasted_iota(I32, shape, dim)


def _layer_norm(r, g, b):
    mu = jnp.mean(r, axis=-1, keepdims=True)
    c = r - mu
    var = jnp.mean(c * c, axis=-1, keepdims=True)
    return c * lax.rsqrt(var + LN_EPS) * g + b


def _inproj_kernel(x_ref, w_ref, wt_ref, u_ref, mqt_ref, mk_ref, mvt_ref, gq_ref, gk_ref, gv_ref, gz_ref,
                   ab_ref, *, s5w, mw, gw):
    xb = x_ref[...].astype(BF16)
    nh_m = mw // HEAD_DIM
    blk = MOBA_BLOCK
    blocks = range(x_ref.shape[0] // blk)

    def cols(off, width):
        return _dot(xb, w_ref[:, off:off + width])

    u_ref[...] = cols(0, s5w)
    off = s5w + mw
    kk = cols(off, mw)
    for h in range(nh_m):
        for n in blocks:
            mk_ref[0, h, n] = kk[n * blk:(n + 1) * blk, h * HEAD_DIM:(h + 1) * HEAD_DIM]
    off += 2 * mw
    for ref in (gq_ref, gk_ref, gv_ref, gz_ref):
        ref[0] = cols(off, gw)
        off += gw
    ab_ref[0] = cols(off, LANE)
    qt = _dot_nt(wt_ref[0:mw, :], xb)
    vt = _dot_nt(wt_ref[mw:2 * mw, :], xb)
    for h in range(nh_m):
        for n in blocks:
            mqt_ref[0, h, n] = qt[h * HEAD_DIM:(h + 1) * HEAD_DIM, n * blk:(n + 1) * blk]
            mvt_ref[0, h, n] = vt[h * HEAD_DIM:(h + 1) * HEAD_DIM, n * blk:(n + 1) * blk]


def _pack_inproj(w_in, s5w, mw, gw):
    c = w_in.shape[1]
    main = s5w + 3 * mw + 4 * gw
    w = jnp.pad(w_in, ((0, 0), (0, main + LANE - c))).astype(BF16)
    q0, v0 = s5w, s5w + 2 * mw
    wt = jnp.concatenate([w_in[:, q0:q0 + mw], w_in[:, v0:v0 + mw]], axis=1).T.astype(BF16)
    return w, wt


def _inproj(x2, w, wt, *, bsz, seqlen, s5w, mw, gw):
    tl = INPROJ_ROWS
    blk = MOBA_BLOCK
    nt = seqlen // tl
    nb = seqlen // blk
    per = tl // blk
    d = x2.shape[1]
    nh_m = mw // HEAD_DIM
    hd = HEAD_DIM
    gdn_sds = jax.ShapeDtypeStruct((bsz, seqlen, gw), F32)
    out_shape = (
        jax.ShapeDtypeStruct((seqlen, bsz * s5w), F32),
        jax.ShapeDtypeStruct((bsz, nh_m, nb, hd, blk), F32),
        jax.ShapeDtypeStruct((bsz, nh_m, nb, blk, hd), F32),
        jax.ShapeDtypeStruct((bsz, nh_m, nb, hd, blk), F32),
        gdn_sds, gdn_sds, gdn_sds, gdn_sds,
        jax.ShapeDtypeStruct((bsz, seqlen, LANE), F32),
    )
    t5 = lambda b, j: (b, 0, j, 0, 0)
    t3 = lambda b, j: (b, j, 0)
    out_specs = (
        pl.BlockSpec((tl, s5w), lambda b, j: (j, b)),
        pl.BlockSpec((1, nh_m, per, hd, blk), t5),
        pl.BlockSpec((1, nh_m, per, blk, hd), t5),
        pl.BlockSpec((1, nh_m, per, hd, blk), t5),
        pl.BlockSpec((1, tl, gw), t3),
        pl.BlockSpec((1, tl, gw), t3),
        pl.BlockSpec((1, tl, gw), t3),
        pl.BlockSpec((1, tl, gw), t3),
        pl.BlockSpec((1, tl, LANE), t3),
    )
    return pl.pallas_call(
        functools.partial(_inproj_kernel, s5w=s5w, mw=mw, gw=gw),
        out_shape=out_shape,
        grid=(bsz, nt),
        in_specs=[
            pl.BlockSpec((tl, d), lambda b, j: (b * nt + j, 0)),
            pl.BlockSpec(w.shape, lambda b, j: (0, 0)),
            pl.BlockSpec(wt.shape, lambda b, j: (0, 0)),
        ],
        out_specs=out_specs,
        compiler_params=_cparams("parallel", "arbitrary"),
        name="inproj",
    )(x2, w, wt)


def _s5_kernel(u_ref, bblk_ref, cblk_ref, lam_ref, d_ref, gw_ref, gb_ref, y_ref, st_ref, h_ref, *, bsz, tt, ns):
    @pl.when(pl.program_id(0) == 0)
    def _():
        h_ref[...] = jnp.zeros_like(h_ref)

    lam_re = jnp.broadcast_to(lam_ref[0:1, :], (bsz, ns))
    lam_im = jnp.broadcast_to(lam_ref[1:2, :], (bsz, ns))
    part = tt // S5_PARTS
    for q in range(S5_PARTS):
        rows = slice(q * part * bsz, (q + 1) * part * bsz)
        st_ref[rows, :] = _dot(u_ref[rows, :].astype(BF16), bblk_ref[...])
    h_re, h_im = h_ref[0], h_ref[1]
    for q in range(S5_PARTS):
        for t in range(q * part, (q + 1) * part):
            r = slice(t * bsz, (t + 1) * bsz)
            n_re = lam_re * h_re - lam_im * h_im + st_ref[r, 0:ns]
            n_im = lam_re * h_im + lam_im * h_re + st_ref[r, ns:2 * ns]
            st_ref[r, 0:ns] = n_re
            st_ref[r, ns:2 * ns] = n_im
            h_re, h_im = n_re, n_im
        rows = slice(q * part * bsz, (q + 1) * part * bsz)
        y = _dot(st_ref[rows, :].astype(BF16), cblk_ref[...]) + d_ref[...] * u_ref[rows, :]
        y = jax.nn.gelu(y)
        y_ref[rows, :] = y * jax.nn.sigmoid(_dot(y.astype(BF16), gw_ref[...]) + gb_ref[...])
    h_ref[0] = h_re
    h_ref[1] = h_im


def _s5(u2, bblk, cblk, lam, d, glu_w, glu_b, *, bsz, seqlen):
    width = u2.shape[1]
    ns = lam.shape[1]
    tt = 64
    rows = tt * bsz
    const = lambda i: (0, 0)
    return pl.pallas_call(
        functools.partial(_s5_kernel, bsz=bsz, tt=tt, ns=ns),
        out_shape=jax.ShapeDtypeStruct(u2.shape, F32),
        grid=(seqlen // tt,),
        in_specs=[
            pl.BlockSpec((rows, width), lambda i: (i, 0)),
            pl.BlockSpec(bblk.shape, const),
            pl.BlockSpec(cblk.shape, const),
            pl.BlockSpec(lam.shape, const),
            pl.BlockSpec(d.shape, const),
            pl.BlockSpec(glu_w.shape, const),
            pl.BlockSpec(glu_b.shape, const),
        ],
        out_specs=pl.BlockSpec((rows, width), lambda i: (i, 0)),
        scratch_shapes=[pltpu.VMEM((rows, 2 * ns), F32), pltpu.VMEM((2, bsz, ns), F32)],
        compiler_params=_cparams("arbitrary"),
        name="s5",
    )(u2, bblk, cblk, lam, d, glu_w, glu_b)


def _s5_params(lam_re, lam_im, log_dt, b_re, b_im, c_re, c_im, d_skip):
    g, n = lam_re.shape
    p = b_re.shape[-1]
    lam = lax.complex(lam_re.astype(F32), lam_im.astype(F32))
    step = jnp.exp(log_dt.astype(F32))[:, None]
    lam_bar = jnp.exp(lam * step)
    b_bar = ((lam_bar - 1.0) / lam)[..., None] * lax.complex(b_re.astype(F32), b_im.astype(F32))
    eye = jnp.eye(g, dtype=F32)
    b_re_blk = jnp.einsum('gnp,gh->gphn', b_bar.real, eye).reshape(g * p, g * n)
    b_im_blk = jnp.einsum('gnp,gh->gphn', b_bar.imag, eye).reshape(g * p, g * n)
    bblk = jnp.concatenate([b_re_blk, b_im_blk], axis=1)
    c_re_blk = jnp.einsum('gpn,gh->gnhp', c_re.astype(F32), eye).reshape(g * n, g * p)
    c_im_blk = jnp.einsum('gpn,gh->gnhp', c_im.astype(F32), eye).reshape(g * n, g * p)
    cblk = jnp.concatenate([c_re_blk, -c_im_blk], axis=0)
    lam2 = jnp.stack([lam_bar.real.reshape(g * n), lam_bar.imag.reshape(g * n)])
    return bblk.astype(BF16), cblk.astype(BF16), lam2, d_skip.astype(F32).reshape(1, g * p)


def _moba_kernel(qt_ref, k_ref, vt_ref, o_ref, km_ref, qs_ref, bias_ref, acc_ref, s_ref, *, nh, nb):
    i = pl.program_id(1)
    blk = MOBA_BLOCK
    hd = HEAD_DIM

    @pl.when(i == 0)
    def _():
        avg = jnp.full((1, blk), 1.0 / blk, F32)
        for h in range(nh):
            for n in range(nb):
                km_ref[h, n:n + 1, :] = _dot(avg, k_ref[0, h, n], precision=HI)

    n_iota = _iota((nb, blk), 0)
    kpos = _iota((blk, blk), 0)
    qpos = _iota((blk, blk), 1)
    causal = kpos <= qpos

    heads = range(nh)

    qts = [qt_ref[0, h, 0] * (hd ** -0.5) for h in heads]
    qtb = [qts[h].astype(BF16) for h in heads]
    s_own = [_dot(k_ref[0, h, i].astype(BF16), qtb[h]) for h in heads]
    gates = [_dot(km_ref[h], qts[h], precision=HI) for h in heads]
    vt_own = [vt_ref[0, h, i].astype(BF16) for h in heads]
    biases = []
    for h in heads:
        gate = jnp.where(n_iota < i, gates[h], -jnp.inf)
        rank = jnp.zeros((nb, blk), I32)
        for m in range(nb):
            gm = gate[m:m + 1, :]
            ahead = (gm > gate) | ((gm == gate) & (m < n_iota))
            rank = rank + ahead.astype(I32)
        biases.append(jnp.where((n_iota < i) & (rank < MOBA_TOPK), 0.0, NEG_BIG))
    m_own, l_own, acc_own = [], [], []
    for h in heads:
        s = jnp.where(causal, s_own[h], NEG_BIG)
        m0 = jnp.max(s, axis=0, keepdims=True)
        p = jnp.exp(s - m0)
        m_own.append(m0)
        l_own.append(jnp.sum(p, axis=0, keepdims=True))
        acc_own.append(_dot(vt_own[h], p.astype(BF16)))
    s_first = [_dot(k_ref[0, h, 0].astype(BF16), qtb[h]) for h in heads]
    for h in heads:
        bias_ref[h] = biases[h]
        qs_ref[h] = qtb[h]
        acc_ref[h * hd:(h + 1) * hd, :] = acc_own[h]
        s_ref[h] = s_first[h]

    def body(j, carry):
        ms, ls = carry
        ss = [s_ref[h] + bias_ref[h, pl.ds(j, 1), :] for h in heads]
        nxt = jnp.minimum(j + 1, i - 1)
        s_next = [_dot(k_ref[0, h, nxt].astype(BF16), qs_ref[h]) for h in heads]
        vts = [vt_ref[0, h, j].astype(BF16) for h in heads]
        accs = [acc_ref[h * hd:(h + 1) * hd, :] for h in heads]
        new_ms, new_ls, new_accs = [], [], []
        for h in heads:
            m_new = jnp.maximum(ms[h], jnp.max(ss[h], axis=0, keepdims=True))
            p = jnp.exp(ss[h] - m_new)
            corr = jnp.exp(ms[h] - m_new)
            new_ms.append(m_new)
            new_ls.append(ls[h] * corr + jnp.sum(p, axis=0, keepdims=True))
            new_accs.append(accs[h] * corr + _dot(vts[h], p.astype(BF16)))
        for h in heads:
            acc_ref[h * hd:(h + 1) * hd, :] = new_accs[h]
            s_ref[h] = s_next[h]
        return tuple(new_ms), tuple(new_ls)

    _, ls = lax.fori_loop(0, i, body, (tuple(m_own), tuple(l_own)))
    for h in heads:
        rows = slice(h * hd, (h + 1) * hd)
        acc_ref[rows, :] = acc_ref[rows, :] / ls[h]
    eye = (kpos == qpos).astype(F32)
    o_ref[0] = _dot_nt(eye, acc_ref[...], precision=HI)


def _moba(qt, k, vt, *, seqlen):
    bsz, nh, nb, hd, blk = qt.shape
    return pl.pallas_call(
        functools.partial(_moba_kernel, nh=nh, nb=nb),
        out_shape=jax.ShapeDtypeStruct((bsz, seqlen, nh * hd), F32),
        grid=(bsz, nb),
        in_specs=[
            pl.BlockSpec((1, nh, 1, hd, blk), lambda b, i: (b, 0, i, 0, 0)),
            pl.BlockSpec((1, nh, nb, blk, hd), lambda b, i: (b, 0, 0, 0, 0)),
            pl.BlockSpec((1, nh, nb, hd, blk), lambda b, i: (b, 0, 0, 0, 0)),
        ],
        out_specs=pl.BlockSpec((1, blk, nh * hd), lambda b, i: (b, i, 0)),
        scratch_shapes=[pltpu.VMEM((nh, nb, hd), F32), pltpu.VMEM((nh, hd, blk), BF16),
                        pltpu.VMEM((nh, nb, blk), F32), pltpu.VMEM((nh * hd, blk), F32),
                        pltpu.VMEM((nh, blk, blk), F32)],
        compiler_params=_cparams("parallel", "arbitrary"),
        name="moba",
    )(qt, k, vt)


def _gdn_kernel(q_ref, k_ref, v_ref, z_ref, ab_ref, cw_ref, al_ref, dtb_ref, nw_ref, o_ref,
                g_ref, beta_ref, us_ref, ws_ref, qe_ref, kd_ref, at_ref, el_ref, s_ref, *, nh, seqlen):
    npair = nh // 2
    ch = GDN_CHUNK
    hd = HEAD_DIM
    gw = nh * hd
    pw = 2 * hd
    pairs = range(npair)

    rr = _iota((pw, pw), 0)
    cc = _iota((pw, pw), 1)
    same_head = (rr // hd) == (cc // hd)
    mask_incl = same_head & ((rr % ch) >= (cc % ch))
    mask_strict = same_head & ((rr % ch) > (cc % ch))
    eye = (rr == cc).astype(F32)
    tril_c = (_iota((ch, ch), 0) >= _iota((ch, ch), 1)).astype(F32)
    lane_c = _iota((ch, LANE), 1)
    first = lane_c < hd
    row_col = _iota((pw, 1), 0)
    lane_row = _iota((1, pw), 1)

    def stack(x):
        return jnp.concatenate([jnp.where(first, x, 0.0), jnp.where(first, 0.0, x)], axis=0)

    def fold(x):
        return x[:ch] + x[ch:]

    def head_sumsq(y):
        y2 = y * y
        sa = jnp.sum(jnp.where(first, y2, 0.0), axis=1, keepdims=True)
        sb = jnp.sum(jnp.where(first, 0.0, y2), axis=1, keepdims=True)
        return jnp.where(first, sa, sb)

    a = ab_ref[0]
    ea = a + dtb_ref[...]
    g_ref[...] = -jnp.exp(al_ref[...]) * (jnp.maximum(ea, 0.0) + jnp.log1p(jnp.exp(-jnp.abs(ea))))
    beta_ref[...] = jax.nn.sigmoid(a)

    def conv_silu(src, t, p, c, r0):
        lanes = slice(p * pw, (p + 1) * pw)
        h0 = pl.multiple_of(jnp.maximum(r0 - 8, 0), 8)
        hist = jnp.where(c > 0, src[0, pl.ds(h0, 8), lanes], 0.0)
        x = jnp.concatenate([hist, src[0, pl.ds(r0, ch), lanes]], axis=0)
        c0 = t * gw + p * pw
        acc = x * cw_ref[3:4, c0:c0 + pw]
        for tap in range(3):
            acc = acc + pltpu.roll(x, 3 - tap, 0) * cw_ref[tap:tap + 1, c0:c0 + pw]
        y = acc[8:]
        return y * jax.nn.sigmoid(y)

    def prep(cc, carry):
        chunks = [cc * GDN_PREP_CHUNKS + u for u in range(GDN_PREP_CHUNKS)]
        r0s = [pl.multiple_of(c * ch, ch) for c in chunks]
        gcs = [_dot(tril_c, g_ref[pl.ds(r0, ch), :], precision=HI) for r0 in r0s]
        bch = [beta_ref[pl.ds(r0, ch), :] for r0 in r0s]
        streams = [(u, p) for u in range(GDN_PREP_CHUNKS) for p in pairs]
        ns = range(len(streams))
        qs, ks, vs, gcol, bcol, decay = [], [], [], [], [], []
        for u, p in streams:
            q = conv_silu(q_ref, 0, p, chunks[u], r0s[u])
            k = conv_silu(k_ref, 1, p, chunks[u], r0s[u])
            v = conv_silu(v_ref, 2, p, chunks[u], r0s[u])
            q = q * (lax.rsqrt(head_sumsq(q) + RMS_EPS) * (hd ** -0.5))
            k = k * lax.rsqrt(head_sumsq(k) + RMS_EPS)
            qs.append(stack(q))
            ks.append(stack(k))
            vs.append(stack(v))
            ha, hb = 2 * p, 2 * p + 1
            gst = jnp.concatenate([jnp.where(lane_c == ha, gcs[u], 0.0), jnp.where(lane_c == hb, gcs[u], 0.0)], axis=0)
            bst = jnp.concatenate([jnp.where(lane_c == nh + ha, bch[u], 0.0),
                                   jnp.where(lane_c == nh + hb, bch[u], 0.0)], axis=0)
            gc = jnp.sum(gst, axis=1, keepdims=True)
            gcol.append(gc)
            bcol.append(jnp.sum(bst, axis=1, keepdims=True))
            gmat = jnp.broadcast_to(gc, (pw, pw))
            decay.append(jnp.where(mask_incl, jnp.exp(jnp.where(mask_incl, gmat - gmat.T, 0.0)), 0.0))
        kbs = [ks[n] * bcol[n] for n in ns]
        ksb = [ks[n].astype(BF16) for n in ns]
        pm = [jnp.where(mask_strict, _dot_nt(kbs[n].astype(BF16), ksb[n]) * decay[n], 0.0) for n in ns]
        tm = [eye - pm[n] for n in ns]
        pmb = [pm[n].astype(BF16) for n in ns]
        pm = [_dot(pmb[n], pmb[n]) for n in ns]
        for _ in range(4):
            pmb = [pm[n].astype(BF16) for n in ns]
            tm = [tm[n] + _dot(tm[n].astype(BF16), pmb[n]) for n in ns]
            pm = [_dot(pmb[n], pmb[n]) for n in ns]
        tm = [tm[n] + _dot(tm[n].astype(BF16), pm[n].astype(BF16)) for n in ns]
        out = []
        for n in ns:
            eg = jnp.exp(gcol[n])
            rhs = jnp.concatenate([vs[n] * bcol[n], kbs[n] * eg], axis=1).astype(BF16)
            sol = _dot(tm[n].astype(BF16), rhs)
            attn = jnp.where(mask_incl, _dot_nt(qs[n].astype(BF16), ksb[n]) * decay[n], 0.0)
            g_a = gcol[n][ch - 1:ch]
            g_b = gcol[n][2 * ch - 1:2 * ch]
            glast = jnp.where(row_col < ch, g_a, g_b)
            e_last = jnp.where(lane_row < hd, jnp.exp(g_a), jnp.exp(g_b))
            out.append((fold(sol[:, :pw]), fold(sol[:, pw:]).astype(BF16), fold(qs[n] * eg).astype(BF16),
                        fold(ks[n] * jnp.exp(glast - gcol[n])).astype(BF16), fold(attn).astype(BF16),
                        jnp.broadcast_to(e_last, (8, pw))))
        for n, (u, p) in enumerate(streams):
            rows = pl.ds(r0s[u], ch)
            us_ref[p, rows, :], ws_ref[p, rows, :], qe_ref[p, rows, :], kd_ref[p, rows, :], at_ref[p, rows, :] = out[n][:5]
            el_ref[p, pl.ds(pl.multiple_of(chunks[u] * 8, 8), 8), :] = out[n][5]
        return carry

    lax.fori_loop(0, seqlen // (ch * GDN_PREP_CHUNKS), prep, 0)

    s_ref[...] = jnp.zeros_like(s_ref)

    def scan(c, carry):
        r0 = pl.multiple_of(c * ch, ch)
        rows = pl.ds(r0, ch)
        lanes = [slice(p * pw, (p + 1) * pw) for p in pairs]
        states = [s_ref[p] for p in pairs]
        wq = [jnp.concatenate([ws_ref[p, rows, :], qe_ref[p, rows, :]], axis=0) for p in pairs]
        us = [us_ref[p, rows, :] for p in pairs]
        at = [at_ref[p, rows, :] for p in pairs]
        kd = [kd_ref[p, rows, :] for p in pairs]
        el = [el_ref[p, pl.ds(c * 8, 1), :] for p in pairs]
        zs = [z_ref[0, rows, lanes[p]] for p in pairs]
        wqs = [_dot(wq[p], states[p].astype(BF16)) for p in pairs]
        v_new = [us[p] - wqs[p][:ch] for p in pairs]
        upd = [_dot_tn(kd[p], v_new[p].astype(BF16)) for p in pairs]
        new_states = [states[p] * el[p] + jnp.where(same_head, upd[p], 0.0) for p in pairs]
        outs = []
        for p in pairs:
            o = wqs[p][ch:] + _dot(at[p], stack(v_new[p]).astype(BF16))
            ms = head_sumsq(o) * (1.0 / hd)
            outs.append(o * lax.rsqrt(ms + RMS_EPS) * nw_ref[...] * (zs[p] * jax.nn.sigmoid(zs[p])))
        for p in pairs:
            s_ref[p] = new_states[p]
            o_ref[0, rows, lanes[p]] = outs[p]
        return carry

    lax.fori_loop(0, seqlen // ch, scan, 0)


def _gdn_params(conv_w, a_log, dt_bias, norm_w):
    nh = a_log.shape[0]
    lane_pad = lambda v: jnp.pad(v.astype(F32), (0, LANE - nh)).reshape(1, LANE)
    return conv_w.astype(F32), lane_pad(a_log), lane_pad(dt_bias), jnp.tile(norm_w.astype(F32), 2).reshape(1, LANE)


def _gdn(q, k, v, z, ab, conv_w, al, dtb, nw):
    bsz, seqlen, gw = q.shape
    nh = gw // HEAD_DIM
    npair = nh // 2
    seq_spec = pl.BlockSpec((1, seqlen, gw), lambda b: (b, 0, 0))
    const = lambda b: (0, 0)
    return pl.pallas_call(
        functools.partial(_gdn_kernel, nh=nh, seqlen=seqlen),
        out_shape=jax.ShapeDtypeStruct((bsz, seqlen, gw), F32),
        grid=(bsz,),
        in_specs=[seq_spec, seq_spec, seq_spec, seq_spec,
                  pl.BlockSpec((1, seqlen, LANE), lambda b: (b, 0, 0)),
                  pl.BlockSpec(conv_w.shape, const),
                  pl.BlockSpec(al.shape, const),
                  pl.BlockSpec(dtb.shape, const),
                  pl.BlockSpec(nw.shape, const)],
        out_specs=seq_spec,
        scratch_shapes=[pltpu.VMEM((seqlen, LANE), F32)] * 2
        + [pltpu.VMEM((npair, seqlen, LANE), F32)]
        + [pltpu.VMEM((npair, seqlen, LANE), BF16)] * 4
        + [pltpu.VMEM((npair, seqlen // GDN_CHUNK * 8, LANE), F32)]
        + [pltpu.VMEM((npair, LANE, LANE), F32)],
        compiler_params=_cparams("parallel"),
        name="gdn",
    )(q, k, v, z, ab, conv_w, al, dtb, nw)


def _route(logits):
    n_exp = logits.shape[0]
    eidx = _iota(logits.shape, 0)
    m1 = jnp.max(logits, axis=0, keepdims=True)
    i1 = jnp.min(jnp.where(logits == m1, eidx, n_exp), axis=0, keepdims=True)
    lg2 = jnp.where(eidx == i1, -jnp.inf, logits)
    m2 = jnp.max(lg2, axis=0, keepdims=True)
    i2 = jnp.min(jnp.where(lg2 == m2, eidx, n_exp), axis=0, keepdims=True)
    e2 = jnp.exp(m2 - m1)
    g1 = 1.0 / (1.0 + e2)
    g2 = e2 / (1.0 + e2)
    gates = jnp.where(eidx == i1, g1, 0.0) + jnp.where(eidx == i2, g2, 0.0)
    mask = jnp.where((eidx == i1) | (eidx == i2), 1.0, 0.0)
    return jnp.concatenate([gates, mask], axis=0)


def _outproj_kernel(x_ref, ys_ref, ym_ref, yg_ref, w_ref, g_ref, b_ref, *rest, alpha, s5w, mw, n_exp):
    y = jnp.concatenate([ys_ref[...], ym_ref[0], yg_ref[0]], axis=-1).astype(BF16)
    x1 = _layer_norm(alpha * x_ref[...] + _dot(y, w_ref[...]), g_ref[...], b_ref[...])
    if n_exp:
        rw_ref, o_ref, r_ref, ot_ref = rest
        o_ref[...] = x1
        r_ref[...] = _route(_dot_nt(rw_ref[...], x1, precision=HI))
        for s in range(TOKEN_TILE_ROWS):
            ot_ref[:, s, :] = x1[:, s * LANE:(s + 1) * LANE]
    else:
        (o_ref,) = rest
        o_ref[...] = x1


def _outproj(x2, ys, ym, yg, w, g, b, router_w, *, bsz, seqlen, alpha, n_exp):
    tl = OUTPROJ_ROWS
    nt = seqlen // tl
    t, d = x2.shape
    s5w = ys.shape[1] // bsz
    mw = ym.shape[2]
    gw = yg.shape[2]
    const = lambda bi, j: (0, 0)
    row_spec = pl.BlockSpec((tl, d), lambda bi, j: (bi * nt + j, 0))
    in_specs = [row_spec,
                pl.BlockSpec((tl, s5w), lambda bi, j: (j, bi)),
                pl.BlockSpec((1, tl, mw), lambda bi, j: (bi, j, 0)),
                pl.BlockSpec((1, tl, gw), lambda bi, j: (bi, j, 0)),
                pl.BlockSpec(w.shape, const), pl.BlockSpec(g.shape, const), pl.BlockSpec(b.shape, const)]
    args = [x2, ys, ym, yg, w, g, b]
    out_shape = jax.ShapeDtypeStruct((t, d), F32)
    out_specs = row_spec
    if n_exp:
        in_specs.append(pl.BlockSpec(router_w.shape, const))
        args.append(router_w)
        assert d == TOKEN_TILE_ROWS * LANE
        out_shape = (out_shape, jax.ShapeDtypeStruct((2 * n_exp, t), F32),
                     jax.ShapeDtypeStruct((t, TOKEN_TILE_ROWS, LANE), F32))
        out_specs = (row_spec, pl.BlockSpec((2 * n_exp, tl), lambda bi, j: (0, bi * nt + j)),
                     pl.BlockSpec((tl, TOKEN_TILE_ROWS, LANE), lambda bi, j: (bi * nt + j, 0, 0)))
    return pl.pallas_call(
        functools.partial(_outproj_kernel, alpha=alpha, s5w=s5w, mw=mw, n_exp=n_exp),
        out_shape=out_shape,
        grid=(bsz, nt),
        in_specs=in_specs,
        out_specs=out_specs,
        compiler_params=_cparams("parallel", "arbitrary"),
        name="outproj_route" if n_exp else "outproj",
    )(*args)


def _swiglu_part(xb, wg_ref, wu_ref, wd_ref):
    out = None
    for c0 in range(0, wg_ref.shape[1], FFN_CHUNK):
        cols = slice(c0, c0 + FFN_CHUNK)
        hg = _dot(xb, wg_ref[:, cols].astype(BF16))
        hu = _dot(xb, wu_ref[:, cols].astype(BF16))
        h = (hg * jax.nn.sigmoid(hg)) * hu
        part = _dot(h.astype(BF16), wd_ref[cols, :].astype(BF16))
        out = part if out is None else out + part
    return out


def _ffn_dense_kernel(x_ref, wg_ref, wu_ref, wd_ref, g_ref, b_ref, o_ref, xb_ref, acc_ref, *, alpha):
    @pl.when(pl.program_id(1) == 0)
    def _():
        xb_ref[...] = x_ref[...].astype(BF16)
        acc_ref[...] = jnp.zeros_like(acc_ref)

    acc_ref[...] += _swiglu_part(xb_ref[...], wg_ref.at[0], wu_ref.at[0], wd_ref.at[0])

    @pl.when(pl.program_id(1) == pl.num_programs(1) - 1)
    def _():
        o_ref[...] = _layer_norm(alpha * x_ref[...] + acc_ref[...], g_ref[...], b_ref[...])


def _ffn_dense(x2, wg, wu, wd, layer, g, b, *, alpha):
    t, d = x2.shape
    ff = wg.shape[2]
    tm, tf = FFN_ROWS, FFN_COLS
    const = lambda i, f: (0, 0)
    return pl.pallas_call(
        functools.partial(_ffn_dense_kernel, alpha=alpha),
        out_shape=jax.ShapeDtypeStruct((t, d), F32),
        grid=(t // tm, ff // tf),
        in_specs=[pl.BlockSpec((tm, d), lambda i, f: (i, 0)),
                  pl.BlockSpec((1, d, tf), lambda i, f: (layer, 0, f)),
                  pl.BlockSpec((1, d, tf), lambda i, f: (layer, 0, f)),
                  pl.BlockSpec((1, tf, d), lambda i, f: (layer, f, 0)),
                  pl.BlockSpec(g.shape, const), pl.BlockSpec(b.shape, const)],
        out_specs=pl.BlockSpec((tm, d), lambda i, f: (i, 0)),
        scratch_shapes=[pltpu.VMEM((tm, d), BF16), pltpu.VMEM((tm, d), F32)],
        compiler_params=_cparams("parallel", "arbitrary"),
        name="ffn_dense",
    )(x2, wg, wu, wd, g, b)


def _ffn_moe_kernel(tile_ref, exp_ref, lo_ref, hi_ref, first_ref, x_ref, wg_ref, wu_ref, wd_ref, o_ref,
                    xt_ref, xb_ref, acc_ref, wgb_ref, wub_ref, wdb_ref):
    k = pl.program_id(0)
    f = pl.program_id(1)
    lo, hi = lo_ref[k], hi_ref[k]
    sub = MOE_SUB_ROWS
    d = o_ref.shape[1]

    nsub = o_ref.shape[0] // sub
    touched = sum(((lo < (sb + 1) * sub) & (hi > sb * sub)).astype(I32) for sb in range(nsub))
    whole = touched > nsub // 2

    def each_live_sub_block(fn):
        for sb in range(nsub):
            live = whole | ((lo < (sb + 1) * sub) & (hi > sb * sub))
            pl.when(live)(functools.partial(fn, sb, slice(sb * sub, (sb + 1) * sub)))

    @pl.when(hi > lo)
    def _():
        @pl.when((f == 0) & (first_ref[k] > 0))
        def _():
            o_ref[...] = jnp.zeros_like(o_ref)

        def load_rows(sb, rows):
            for s in range(TOKEN_TILE_ROWS):
                xt_ref[rows, s * LANE:(s + 1) * LANE] = x_ref[rows, s, :]
            xb_ref[rows, :] = xt_ref[rows, :].astype(BF16)
            acc_ref[rows, :] = jnp.zeros((sub, d), F32)

        pl.when(f == 0)(functools.partial(each_live_sub_block, load_rows))

        @pl.when(whole)
        def _():
            acc_ref[...] += _swiglu_part(xb_ref[...], wg_ref.at[0, 0], wu_ref.at[0, 0], wd_ref.at[0, 0])

        @pl.when(jnp.logical_not(whole))
        def _():
            wgb_ref[...] = wg_ref[0, 0].astype(BF16)
            wub_ref[...] = wu_ref[0, 0].astype(BF16)
            wdb_ref[...] = wd_ref[0, 0].astype(BF16)

            def compute(sb, rows):
                acc_ref[rows, :] += _swiglu_part(xb_ref[rows, :], wgb_ref, wub_ref, wdb_ref)

            each_live_sub_block(compute)

        def emit(sb, rows):
            row = _iota((sub, d), 0) + sb * sub
            o_ref[rows, :] = jnp.where((row >= lo) & (row < hi), acc_ref[rows, :], o_ref[rows, :])

        pl.when(f == pl.num_programs(1) - 1)(functools.partial(each_live_sub_block, emit))


def _ffn_moe(items, xs, wg, wu, wd, layer):
    p_rows = xs.shape[0]
    d = wg.shape[2]
    ff = wg.shape[3]
    tm, tf = MOE_ROWS, FFN_COLS
    nf = ff // tf
    n_items = items[0].shape[0]

    def fsel(k, f, lo, hi):
        return jnp.where(hi[k] > lo[k], f, nf - 1)

    grid_spec = pltpu.PrefetchScalarGridSpec(
        num_scalar_prefetch=5,
        grid=(n_items, nf),
        in_specs=[pl.BlockSpec((tm, TOKEN_TILE_ROWS, LANE), lambda k, f, tile, exp, lo, hi, first: (tile[k], 0, 0)),
                  pl.BlockSpec((1, 1, d, tf),
                               lambda k, f, tile, exp, lo, hi, first: (layer, exp[k], 0, fsel(k, f, lo, hi))),
                  pl.BlockSpec((1, 1, d, tf),
                               lambda k, f, tile, exp, lo, hi, first: (layer, exp[k], 0, fsel(k, f, lo, hi))),
                  pl.BlockSpec((1, 1, tf, d),
                               lambda k, f, tile, exp, lo, hi, first: (layer, exp[k], fsel(k, f, lo, hi), 0))],
        out_specs=pl.BlockSpec((tm, d), lambda k, f, tile, exp, lo, hi, first: (tile[k], 0)),
        scratch_shapes=[pltpu.VMEM((tm, d), F32), pltpu.VMEM((tm, d), BF16), pltpu.VMEM((tm, d), F32),
                        pltpu.VMEM((d, tf), BF16), pltpu.VMEM((d, tf), BF16), pltpu.VMEM((tf, d), BF16)],
    )
    return pl.pallas_call(
        _ffn_moe_kernel,
        out_shape=jax.ShapeDtypeStruct((p_rows, d), F32),
        grid_spec=grid_spec,
        compiler_params=_cparams("arbitrary", "arbitrary"),
        name="ffn_moe",
    )(*items, xs, wg, wu, wd)


def _tile_copy(src_ref, src_row, dst_ref, dst_row, sem):
    return pltpu.make_async_copy(src_ref.at[pl.ds(src_row, 1)], dst_ref.at[pl.ds(dst_row, 1)], sem)


def _dispatch_kernel(p0_ref, p1_ref, x_ref, xs_hbm, sem, *, rows):
    def start(r, carry):
        _tile_copy(x_ref, r, xs_hbm, p0_ref[0, 0, r], sem.at[0]).start(priority=0)
        _tile_copy(x_ref, r, xs_hbm, p1_ref[0, 0, r], sem.at[1]).start(priority=1)
        return carry

    lax.fori_loop(0, rows, start, 0, unroll=8)

    def wait(r, carry):
        _tile_copy(x_ref, r, xs_hbm, 0, sem.at[0]).wait()
        _tile_copy(x_ref, r, xs_hbm, 0, sem.at[1]).wait()
        return carry

    lax.fori_loop(0, rows, wait, 0, unroll=8)


def _dispatch(p0, p1, xt):
    nt, _, rows = p0.shape
    t = xt.shape[0]
    idx_spec = pl.BlockSpec((1, 1, rows), lambda i: (i, 0, 0), memory_space=pltpu.SMEM)
    return pl.pallas_call(
        functools.partial(_dispatch_kernel, rows=rows),
        out_shape=jax.ShapeDtypeStruct((t * TOP_K, TOKEN_TILE_ROWS, LANE), xt.dtype),
        grid=(nt,),
        in_specs=[idx_spec, idx_spec, pl.BlockSpec((rows, TOKEN_TILE_ROWS, LANE), lambda i: (i, 0, 0))],
        out_specs=pl.BlockSpec(memory_space=pl.ANY),
        scratch_shapes=[pltpu.SemaphoreType.DMA((2,))],
        compiler_params=_cparams("arbitrary"),
        name="moe_dispatch",
    )(p0, p1, xt)


def _combine_kernel(p0_ref, p1_ref, n0_ref, n1_ref, ys_hbm, x_ref, gt_ref, g_ref, b_ref, o_ref, y_ref, sem, *,
                    rows, alpha):
    i = pl.program_id(0)
    slot = lax.rem(i, 2)

    def fetch(i0_ref, i1_ref, s):
        def start(r, carry):
            _tile_copy(ys_hbm, i0_ref[0, 0, r], y_ref.at[s, 0], r, sem.at[s, 0]).start(priority=0)
            _tile_copy(ys_hbm, i1_ref[0, 0, r], y_ref.at[s, 1], r, sem.at[s, 1]).start(priority=1)
            return carry

        lax.fori_loop(0, rows, start, 0, unroll=8)

    pl.when(i == 0)(functools.partial(fetch, p0_ref, p1_ref, slot))
    pl.when(i + 1 < pl.num_programs(0))(functools.partial(fetch, n0_ref, n1_ref, 1 - slot))

    def wait(r, carry):
        _tile_copy(ys_hbm, 0, y_ref.at[slot, 0], r, sem.at[slot, 0]).wait()
        _tile_copy(ys_hbm, 0, y_ref.at[slot, 1], r, sem.at[slot, 1]).wait()
        return carry

    lax.fori_loop(0, rows, wait, 0, unroll=8)
    gt = gt_ref[...]
    f = gt[:, 0:1] * y_ref[slot, 0] + gt[:, 1:2] * y_ref[slot, 1]
    o_ref[...] = _layer_norm(alpha * x_ref[...] + f, g_ref[...], b_ref[...])


def _combine(p0, p1, ys, x2, gates, g, b, *, alpha):
    nt, _, rows = p0.shape
    t, d = x2.shape
    const = lambda i: (0, 0)
    idx_spec = pl.BlockSpec((1, 1, rows), lambda i: (i, 0, 0), memory_space=pltpu.SMEM)
    next_spec = pl.BlockSpec((1, 1, rows), lambda i: (jnp.minimum(i + 1, nt - 1), 0, 0), memory_space=pltpu.SMEM)
    return pl.pallas_call(
        functools.partial(_combine_kernel, rows=rows, alpha=alpha),
        out_shape=jax.ShapeDtypeStruct((t, d), F32),
        grid=(nt,),
        in_specs=[idx_spec, idx_spec, next_spec, next_spec,
                  pl.BlockSpec(memory_space=pl.ANY),
                  pl.BlockSpec((rows, d), lambda i: (i, 0)),
                  pl.BlockSpec((rows, LANE), lambda i: (i, 0)),
                  pl.BlockSpec(g.shape, const), pl.BlockSpec(b.shape, const)],
        out_specs=pl.BlockSpec((rows, d), lambda i: (i, 0)),
        scratch_shapes=[pltpu.VMEM((2, 2, rows, d), F32), pltpu.SemaphoreType.DMA((2, 2))],
        compiler_params=_cparams("arbitrary"),
        name="moe_combine",
    )(p0, p1, p0, p1, ys, x2, gates, g, b)


def _dispatch_plan(route, n_exp):
    t = route.shape[1]
    tm = MOE_ROWS
    gates = route[:n_exp].T
    sel = route[n_exp:].T > 0.5
    seli = sel.astype(I32)
    rank = jnp.cumsum(seli, axis=0) - seli
    counts = jnp.sum(seli, axis=0)
    ends = jnp.cumsum(counts)
    starts = ends - counts
    pos = starts[None, :] + rank
    order = jnp.cumsum(seli, axis=1) - seli
    first = sel & (order == 0)
    second = sel & (order == 1)
    pick = lambda m, v: jnp.sum(jnp.where(m, v, 0), axis=1)
    p0, p1 = pick(first, pos), pick(second, pos)
    g01 = jnp.stack([pick(first, gates), pick(second, gates)], axis=1)
    g01 = jnp.pad(g01, ((0, 0), (0, LANE - TOP_K)))
    n_tiles = t * TOP_K // tm
    n_items = n_tiles + n_exp - 1
    tile_lo = jnp.arange(n_tiles, dtype=I32)[:, None] * tm
    ov_lo = jnp.maximum(starts[None, :], tile_lo)
    ov_hi = jnp.minimum(ends[None, :], tile_lo + tm)
    live = (ov_hi > ov_lo).reshape(-1)
    flat = jnp.arange(n_tiles * n_exp, dtype=I32)
    take = jnp.argsort(jnp.where(live, flat, n_tiles * n_exp))[:n_items]
    n_live = jnp.sum(live.astype(I32))
    idle = jnp.arange(n_items, dtype=I32) >= n_live
    last = take[jnp.maximum(n_live - 1, 0)]
    take = jnp.where(idle, last, take)
    item_tile = take // n_exp
    item_exp = take % n_exp
    item_lo = jnp.where(idle, 0, (ov_lo - tile_lo).reshape(-1)[take])
    item_hi = jnp.where(idle, 0, (ov_hi - tile_lo).reshape(-1)[take])
    prev_tile = jnp.concatenate([jnp.full((1,), -1, I32), item_tile[:-1]])
    item_first = (item_tile != prev_tile).astype(I32)
    items = tuple(v.astype(I32) for v in (item_tile, item_exp, item_lo, item_hi, item_first))
    shape3 = lambda v: v.astype(I32).reshape(-1, 1, GATHER_ROWS)
    return shape3(p0), shape3(p1), g01, items


def kernel(x, w_in, w_out, s5_lam_re, s5_lam_im, s5_log_dt, s5_b_re, s5_b_im, s5_c_re, s5_c_im, s5_d, s5_glu_w,
           s5_glu_b, gdn_conv_w, gdn_a_log, gdn_dt_bias, gdn_norm_w, ln1_g, ln1_b, ln2_g, ln2_b, ffn_w_gate,
           ffn_w_up, ffn_w_down, moe_router, moe_w_gate, moe_w_up, moe_w_down):
    bsz, seqlen, d = x.shape
    depth = w_in.shape[0]
    s5w = s5_glu_w.shape[1]
    gw = gdn_a_log.shape[1] * HEAD_DIM
    mw = w_out.shape[1] - s5w - gw
    n_exp = moe_router.shape[2]
    alpha = (2 * depth) ** 0.25
    row = lambda v: v.astype(F32).reshape(1, -1)

    x2 = x.reshape(bsz * seqlen, d)
    for l in range(depth):
        w, wt = _pack_inproj(w_in[l], s5w, mw, gw)
        u, mqt, mk, mvt, gq, gk, gv, gz, ab = _inproj(x2, w, wt, bsz=bsz, seqlen=seqlen, s5w=s5w, mw=mw, gw=gw)
        bblk, cblk, lam, dskip = _s5_params(s5_lam_re[l], s5_lam_im[l], s5_log_dt[l], s5_b_re[l], s5_b_im[l],
                                            s5_c_re[l], s5_c_im[l], s5_d[l])
        y_s5 = _s5(u.reshape(seqlen * bsz, s5w), bblk, cblk, lam, dskip, s5_glu_w[l].astype(BF16),
                   row(s5_glu_b[l]), bsz=bsz, seqlen=seqlen).reshape(seqlen, bsz * s5w)
        y_moba = _moba(mqt, mk, mvt, seqlen=seqlen)
        y_gdn = _gdn(gq, gk, gv, gz, ab, *_gdn_params(gdn_conv_w[l], gdn_a_log[l], gdn_dt_bias[l], gdn_norm_w[l]))
        routed = l % 2 == 1
        router_w = moe_router[l // 2].astype(F32).T if routed else None
        res = _outproj(x2, y_s5, y_moba, y_gdn, w_out[l].astype(BF16), row(ln1_g[l]), row(ln1_b[l]), router_w,
                       bsz=bsz, seqlen=seqlen, alpha=alpha, n_exp=n_exp if routed else 0)
        if routed:
            x1, route, x1t = res
            p0, p1, g01, items = _dispatch_plan(route, n_exp)
            xs = _dispatch(p0, p1, x1t)
            ys = _ffn_moe(items, xs, moe_w_gate, moe_w_up, moe_w_down, l // 2)
            x2 = _combine(p0, p1, ys, x1, g01, row(ln2_g[l]), row(ln2_b[l]), alpha=alpha)
        else:
            x2 = _ffn_dense(res, ffn_w_gate, ffn_w_up, ffn_w_down, l // 2, row(ln2_g[l]), row(ln2_b[l]), alpha=alpha)
    return x2.reshape(bsz, seqlen, d)
```

```python
import functools

import jax
import jax.numpy as jnp
from jax import lax
from jax.experimental import pallas as pl
from jax.experimental.pallas import tpu as pltpu

F32 = jnp.float32
BF16 = jnp.bfloat16
I32 = jnp.int32
HI = lax.Precision.HIGHEST

HEAD_DIM = 64
MOBA_BLOCK = 256
MOBA_TOPK = 3
GDN_CHUNK = 64
S5_STEPS = 128
S5_PARTS = 2
GDN_PREP_CHUNKS = 4
TOP_K = 2
LN_EPS = 1e-5
RMS_EPS = 1e-6
NEG_BIG = -1e30

V7X_VMEM_LIMIT_BYTES = 56 * 1024 * 1024
LANE = 128
TOKEN_TILE_ROWS = 8

INPROJ_ROWS = 1024
OUTPROJ_ROWS = 1024
FFN_ROWS = 1024
MOE_ROWS = 1024
MOE_SUB_ROWS = 256
FFN_COLS = 512
FFN_CHUNK = 256
GATHER_ROWS = 512


def _cparams(*sem):
    return pltpu.CompilerParams(dimension_semantics=sem, vmem_limit_bytes=V7X_VMEM_LIMIT_BYTES)


def _dot(a, b, precision=None):
    return jnp.dot(a, b, preferred_element_type=F32, precision=precision)


def _dot_nt(a, b, precision=None):
    return lax.dot_general(a, b, (((1,), (1,)), ((), ())), preferred_element_type=F32, precision=precision)


def _dot_tn(a, b, precision=None):
    return lax.dot_general(a, b, (((0,), (0,)), ((), ())), preferred_element_type=F32, precision=precision)


def _iota(shape, dim):
    return lax.broadcasted_iota(I32, shape, dim)


def _layer_norm(r, g, b):
    mu = jnp.mean(r, axis=-1, keepdims=True)
    c = r - mu
    var = jnp.mean(c * c, axis=-1, keepdims=True)
    return c * lax.rsqrt(var + LN_EPS) * g + b


def _inproj_kernel(x_ref, w_ref, wt_ref, u_ref, mqt_ref, mk_ref, mvt_ref, gq_ref, gk_ref, gv_ref, gz_ref,
                   ab_ref, *, s5w, mw, gw):
    xb = x_ref[...].astype(BF16)
    nh_m = mw // HEAD_DIM
    blk = MOBA_BLOCK
    blocks = range(x_ref.shape[0] // blk)

    def cols(off, width):
        return _dot(xb, w_ref[:, off:off + width])

    u_ref[...] = cols(0, s5w)
    off = s5w + mw
    kk = cols(off, mw)
    for h in range(nh_m):
        for n in blocks:
            mk_ref[0, h, n] = kk[n * blk:(n + 1) * blk, h * HEAD_DIM:(h + 1) * HEAD_DIM]
    off += 2 * mw
    for ref in (gq_ref, gk_ref, gv_ref, gz_ref):
        ref[0] = cols(off, gw)
        off += gw
    ab_ref[0] = cols(off, LANE)
    qt = _dot_nt(wt_ref[0:mw, :], xb)
    vt = _dot_nt(wt_ref[mw:2 * mw, :], xb)
    for h in range(nh_m):
        for n in blocks:
            mqt_ref[0, h, n] = qt[h * HEAD_DIM:(h + 1) * HEAD_DIM, n * blk:(n + 1) * blk]
            mvt_ref[0, h, n] = vt[h * HEAD_DIM:(h + 1) * HEAD_DIM, n * blk:(n + 1) * blk]


def _pack_inproj(w_in, s5w, mw, gw):
    c = w_in.shape[1]
    main = s5w + 3 * mw + 4 * gw
    w = jnp.pad(w_in, ((0, 0), (0, main + LANE - c))).astype(BF16)
    q0, v0 = s5w, s5w + 2 * mw
    wt = jnp.concatenate([w_in[:, q0:q0 + mw], w_in[:, v0:v0 + mw]], axis=1).T.astype(BF16)
    return w, wt


def _inproj(x2, w, wt, *, bsz, seqlen, s5w, mw, gw):
    tl = INPROJ_ROWS
    blk = MOBA_BLOCK
    nt = seqlen // tl
    nb = seqlen // blk
    per = tl // blk
    d = x2.shape[1]
    nh_m = mw // HEAD_DIM
    hd = HEAD_DIM
    gdn_sds = jax.ShapeDtypeStruct((bsz, seqlen, gw), F32)
    out_shape = (
        jax.ShapeDtypeStruct((seqlen, bsz * s5w), F32),
        jax.ShapeDtypeStruct((bsz, nh_m, nb, hd, blk), F32),
        jax.ShapeDtypeStruct((bsz, nh_m, nb, blk, hd), F32),
        jax.ShapeDtypeStruct((bsz, nh_m, nb, hd, blk), F32),
        gdn_sds, gdn_sds, gdn_sds, gdn_sds,
        jax.ShapeDtypeStruct((bsz, seqlen, LANE), F32),
    )
    t5 = lambda b, j: (b, 0, j, 0, 0)
    t3 = lambda b, j: (b, j, 0)
    out_specs = (
        pl.BlockSpec((tl, s5w), lambda b, j: (j, b)),
        pl.BlockSpec((1, nh_m, per, hd, blk), t5),
        pl.BlockSpec((1, nh_m, per, blk, hd), t5),
        pl.BlockSpec((1, nh_m, per, hd, blk), t5),
        pl.BlockSpec((1, tl, gw), t3),
        pl.BlockSpec((1, tl, gw), t3),
        pl.BlockSpec((1, tl, gw), t3),
        pl.BlockSpec((1, tl, gw), t3),
        pl.BlockSpec((1, tl, LANE), t3),
    )
    return pl.pallas_call(
        functools.partial(_inproj_kernel, s5w=s5w, mw=mw, gw=gw),
        out_shape=out_shape,
        grid=(bsz, nt),
        in_specs=[
            pl.BlockSpec((tl, d), lambda b, j: (b * nt + j, 0)),
            pl.BlockSpec(w.shape, lambda b, j: (0, 0)),
            pl.BlockSpec(wt.shape, lambda b, j: (0, 0)),
        ],
        out_specs=out_specs,
        compiler_params=_cparams("parallel", "arbitrary"),
        name="inproj",
    )(x2, w, wt)


def _s5_kernel(u_ref, bblk_ref, cblk_ref, lam_ref, d_ref, gw_ref, gb_ref, y_ref, st_ref, h_ref, *, bsz, tt, ns):
    @pl.when(pl.program_id(0) == 0)
    def _():
        h_ref[...] = jnp.zeros_like(h_ref)

    lam_re = jnp.broadcast_to(lam_ref[0:1, :], (bsz, ns))
    lam_im = jnp.broadcast_to(lam_ref[1:2, :], (bsz, ns))
    part = tt // S5_PARTS
    for q in range(S5_PARTS):
        rows = slice(q * part * bsz, (q + 1) * part * bsz)
        st_ref[rows, :] = _dot(u_ref[rows, :].astype(BF16), bblk_ref[...])
    h_re, h_im = h_ref[0], h_ref[1]
    for q in range(S5_PARTS):
        for t in range(q * part, (q + 1) * part):
            r = slice(t * bsz, (t + 1) * bsz)
            n_re = lam_re * h_re - lam_im * h_im + st_ref[r, 0:ns]
            n_im = lam_re * h_im + lam_im * h_re + st_ref[r, ns:2 * ns]
            st_ref[r, 0:ns] = n_re
            st_ref[r, ns:2 * ns] = n_im
            h_re, h_im = n_re, n_im
        rows = slice(q * part * bsz, (q + 1) * part * bsz)
        y = _dot(st_ref[rows, :].astype(BF16), cblk_ref[...]) + d_ref[...] * u_ref[rows, :]
        y = jax.nn.gelu(y)
        y_ref[rows, :] = y * jax.nn.sigmoid(_dot(y.astype(BF16), gw_ref[...]) + gb_ref[...])
    h_ref[0] = h_re
    h_ref[1] = h_im


def _s5(u2, bblk, cblk, lam, d, glu_w, glu_b, *, bsz, seqlen):
    width = u2.shape[1]
    ns = lam.shape[1]
    tt = S5_STEPS
    rows = tt * bsz
    const = lambda i: (0, 0)
    return pl.pallas_call(
        functools.partial(_s5_kernel, bsz=bsz, tt=tt, ns=ns),
        out_shape=jax.ShapeDtypeStruct(u2.shape, F32),
        grid=(seqlen // tt,),
        in_specs=[
            pl.BlockSpec((rows, width), lambda i: (i, 0)),
            pl.BlockSpec(bblk.shape, const),
            pl.BlockSpec(cblk.shape, const),
            pl.BlockSpec(lam.shape, const),
            pl.BlockSpec(d.shape, const),
            pl.BlockSpec(glu_w.shape, const),
            pl.BlockSpec(glu_b.shape, const),
        ],
        out_specs=pl.BlockSpec((rows, width), lambda i: (i, 0)),
        scratch_shapes=[pltpu.VMEM((rows, 2 * ns), F32), pltpu.VMEM((2, bsz, ns), F32)],
        compiler_params=_cparams("arbitrary"),
        name="s5",
    )(u2, bblk, cblk, lam, d, glu_w, glu_b)


def _s5_params(lam_re, lam_im, log_dt, b_re, b_im, c_re, c_im, d_skip):
    g, n = lam_re.shape
    p = b_re.shape[-1]
    lam = lax.complex(lam_re.astype(F32), lam_im.astype(F32))
    step = jnp.exp(log_dt.astype(F32))[:, None]
    lam_bar = jnp.exp(lam * step)
    b_bar = ((lam_bar - 1.0) / lam)[..., None] * lax.complex(b_re.astype(F32), b_im.astype(F32))
    eye = jnp.eye(g, dtype=F32)
    b_re_blk = jnp.einsum('gnp,gh->gphn', b_bar.real, eye).reshape(g * p, g * n)
    b_im_blk = jnp.einsum('gnp,gh->gphn', b_bar.imag, eye).reshape(g * p, g * n)
    bblk = jnp.concatenate([b_re_blk, b_im_blk], axis=1)
    c_re_blk = jnp.einsum('gpn,gh->gnhp', c_re.astype(F32), eye).reshape(g * n, g * p)
    c_im_blk = jnp.einsum('gpn,gh->gnhp', c_im.astype(F32), eye).reshape(g * n, g * p)
    cblk = jnp.concatenate([c_re_blk, -c_im_blk], axis=0)
    lam2 = jnp.stack([lam_bar.real.reshape(g * n), lam_bar.imag.reshape(g * n)])
    return bblk.astype(BF16), cblk.astype(BF16), lam2, d_skip.astype(F32).reshape(1, g * p)


def _moba_kernel(qt_ref, k_ref, vt_ref, o_ref, km_ref, qs_ref, bias_ref, acc_ref, s_ref, *, nh, nb):
    i = pl.program_id(1)
    blk = MOBA_BLOCK
    hd = HEAD_DIM

    @pl.when(i == 0)
    def _():
        avg = jnp.full((1, blk), 1.0 / blk, F32)
        for h in range(nh):
            for n in range(nb):
                km_ref[h, n:n + 1, :] = _dot(avg, k_ref[0, h, n], precision=HI)

    n_iota = _iota((nb, blk), 0)
    kpos = _iota((blk, blk), 0)
    qpos = _iota((blk, blk), 1)
    causal = kpos <= qpos

    heads = range(nh)

    qts = [qt_ref[0, h, 0] * (hd ** -0.5) for h in heads]
    qtb = [qts[h].astype(BF16) for h in heads]
    s_own = [_dot(k_ref[0, h, i].astype(BF16), qtb[h]) for h in heads]
    gates = [_dot(km_ref[h], qts[h], precision=HI) for h in heads]
    vt_own = [vt_ref[0, h, i].astype(BF16) for h in heads]
    biases = []
    for h in heads:
        gate = jnp.where(n_iota < i, gates[h], -jnp.inf)
        rank = jnp.zeros((nb, blk), I32)
        for m in range(nb):
            gm = gate[m:m + 1, :]
            ahead = (gm > gate) | ((gm == gate) & (m < n_iota))
            rank = rank + ahead.astype(I32)
        biases.append(jnp.where((n_iota < i) & (rank < MOBA_TOPK), 0.0, NEG_BIG))
    m_own, l_own, acc_own = [], [], []
    for h in heads:
        s = jnp.where(causal, s_own[h], NEG_BIG)
        m0 = jnp.max(s, axis=0, keepdims=True)
        p = jnp.exp(s - m0)
        m_own.append(m0)
        l_own.append(jnp.sum(p, axis=0, keepdims=True))
        acc_own.append(_dot(vt_own[h], p.astype(BF16)))
    s_first = [_dot(k_ref[0, h, 0].astype(BF16), qtb[h]) for h in heads]
    for h in heads:
        bias_ref[h] = biases[h]
        qs_ref[h] = qtb[h]
        acc_ref[h * hd:(h + 1) * hd, :] = acc_own[h]
        s_ref[h] = s_first[h]

    def body(j, carry):
        ms, ls = carry
        ss = [s_ref[h] + bias_ref[h, pl.ds(j, 1), :] for h in heads]
        nxt = jnp.minimum(j + 1, i - 1)
        s_next = [_dot(k_ref[0, h, nxt].astype(BF16), qs_ref[h]) for h in heads]
        vts = [vt_ref[0, h, j].astype(BF16) for h in heads]
        accs = [acc_ref[h * hd:(h + 1) * hd, :] for h in heads]
        new_ms, new_ls, new_accs = [], [], []
        for h in heads:
            m_new = jnp.maximum(ms[h], jnp.max(ss[h], axis=0, keepdims=True))
            p = jnp.exp(ss[h] - m_new)
            corr = jnp.exp(ms[h] - m_new)
            new_ms.append(m_new)
            new_ls.append(ls[h] * corr + jnp.sum(p, axis=0, keepdims=True))
            new_accs.append(accs[h] * corr + _dot(vts[h], p.astype(BF16)))
        for h in heads:
            acc_ref[h * hd:(h + 1) * hd, :] = new_accs[h]
            s_ref[h] = s_next[h]
        return tuple(new_ms), tuple(new_ls)

    _, ls = lax.fori_loop(0, i, body, (tuple(m_own), tuple(l_own)))
    for h in heads:
        rows = slice(h * hd, (h + 1) * hd)
        acc_ref[rows, :] = acc_ref[rows, :] / ls[h]
    eye = (kpos == qpos).astype(F32)
    o_ref[0] = _dot_nt(eye, acc_ref[...], precision=HI)


def _moba(qt, k, vt, *, seqlen):
    bsz, nh, nb, hd, blk = qt.shape
    return pl.pallas_call(
        functools.partial(_moba_kernel, nh=nh, nb=nb),
        out_shape=jax.ShapeDtypeStruct((bsz, seqlen, nh * hd), F32),
        grid=(bsz, nb),
        in_specs=[
            pl.BlockSpec((1, nh, 1, hd, blk), lambda b, i: (b, 0, i, 0, 0)),
            pl.BlockSpec((1, nh, nb, blk, hd), lambda b, i: (b, 0, 0, 0, 0)),
            pl.BlockSpec((1, nh, nb, hd, blk), lambda b, i: (b, 0, 0, 0, 0)),
        ],
        out_specs=pl.BlockSpec((1, blk, nh * hd), lambda b, i: (b, i, 0)),
        scratch_shapes=[pltpu.VMEM((nh, nb, hd), F32), pltpu.VMEM((nh, hd, blk), BF16),
                        pltpu.VMEM((nh, nb, blk), F32), pltpu.VMEM((nh * hd, blk), F32),
                        pltpu.VMEM((nh, blk, blk), F32)],
        compiler_params=_cparams("parallel", "arbitrary"),
        name="moba",
    )(qt, k, vt)


def _gdn_kernel(q_ref, k_ref, v_ref, z_ref, ab_ref, cw_ref, al_ref, dtb_ref, nw_ref, o_ref,
                g_ref, beta_ref, us_ref, ws_ref, qe_ref, kd_ref, at_ref, el_ref, s_ref, *, nh, seqlen):
    npair = nh // 2
    ch = GDN_CHUNK
    hd = HEAD_DIM
    gw = nh * hd
    pw = 2 * hd
    pairs = range(npair)

    rr = _iota((pw, pw), 0)
    cc = _iota((pw, pw), 1)
    same_head = (rr // hd) == (cc // hd)
    mask_incl = same_head & ((rr % ch) >= (cc % ch))
    mask_strict = same_head & ((rr % ch) > (cc % ch))
    eye = (rr == cc).astype(F32)
    tril_c = (_iota((ch, ch), 0) >= _iota((ch, ch), 1)).astype(F32)
    lane_c = _iota((ch, LANE), 1)
    first = lane_c < hd
    row_col = _iota((pw, 1), 0)
    lane_row = _iota((1, pw), 1)

    def stack(x):
        return jnp.concatenate([jnp.where(first, x, 0.0), jnp.where(first, 0.0, x)], axis=0)

    def fold(x):
        return x[:ch] + x[ch:]

    def head_sumsq(y):
        y2 = y * y
        sa = jnp.sum(jnp.where(first, y2, 0.0), axis=1, keepdims=True)
        sb = jnp.sum(jnp.where(first, 0.0, y2), axis=1, keepdims=True)
        return jnp.where(first, sa, sb)

    a = ab_ref[0]
    ea = a + dtb_ref[...]
    g_ref[...] = -jnp.exp(al_ref[...]) * (jnp.maximum(ea, 0.0) + jnp.log1p(jnp.exp(-jnp.abs(ea))))
    beta_ref[...] = jax.nn.sigmoid(a)

    def conv_silu(src, t, p, c, r0):
        lanes = slice(p * pw, (p + 1) * pw)
        h0 = pl.multiple_of(jnp.maximum(r0 - 8, 0), 8)
        hist = jnp.where(c > 0, src[0, pl.ds(h0, 8), lanes], 0.0)
        x = jnp.concatenate([hist, src[0, pl.ds(r0, ch), lanes]], axis=0)
        c0 = t * gw + p * pw
        acc = x * cw_ref[3:4, c0:c0 + pw]
        for tap in range(3):
            acc = acc + pltpu.roll(x, 3 - tap, 0) * cw_ref[tap:tap + 1, c0:c0 + pw]
        y = acc[8:]
        return y * jax.nn.sigmoid(y)

    def prep(cc, carry):
        chunks = [cc * GDN_PREP_CHUNKS + u for u in range(GDN_PREP_CHUNKS)]
        r0s = [pl.multiple_of(c * ch, ch) for c in chunks]
        gcs = [_dot(tril_c, g_ref[pl.ds(r0, ch), :], precision=HI) for r0 in r0s]
        bch = [beta_ref[pl.ds(r0, ch), :] for r0 in r0s]
        streams = [(u, p) for u in range(GDN_PREP_CHUNKS) for p in pairs]
        ns = range(len(streams))
        qs, ks, vs, gcol, bcol, decay = [], [], [], [], [], []
        for u, p in streams:
            q = conv_silu(q_ref, 0, p, chunks[u], r0s[u])
            k = conv_silu(k_ref, 1, p, chunks[u], r0s[u])
            v = conv_silu(v_ref, 2, p, chunks[u], r0s[u])
            q = q * (lax.rsqrt(head_sumsq(q) + RMS_EPS) * (hd ** -0.5))
            k = k * lax.rsqrt(head_sumsq(k) + RMS_EPS)
            qs.append(stack(q))
            ks.append(stack(k))
            vs.append(stack(v))
            ha, hb = 2 * p, 2 * p + 1
            gst = jnp.concatenate([jnp.where(lane_c == ha, gcs[u], 0.0), jnp.where(lane_c == hb, gcs[u], 0.0)], axis=0)
            bst = jnp.concatenate([jnp.where(lane_c == nh + ha, bch[u], 0.0),
                                   jnp.where(lane_c == nh + hb, bch[u], 0.0)], axis=0)
            gc = jnp.sum(gst, axis=1, keepdims=True)
            gcol.append(gc)
            bcol.append(jnp.sum(bst, axis=1, keepdims=True))
            gmat = jnp.broadcast_to(gc, (pw, pw))
            decay.append(jnp.where(mask_incl, jnp.exp(jnp.where(mask_incl, gmat - gmat.T, 0.0)), 0.0))
        kbs = [ks[n] * bcol[n] for n in ns]
        ksb = [ks[n].astype(BF16) for n in ns]
        pm = [jnp.where(mask_strict, _dot_nt(kbs[n].astype(BF16), ksb[n]) * decay[n], 0.0) for n in ns]
        tm = [eye - pm[n] for n in ns]
        pmb = [pm[n].astype(BF16) for n in ns]
        pm = [_dot(pmb[n], pmb[n]) for n in ns]
        for _ in range(4):
            pmb = [pm[n].astype(BF16) for n in ns]
            tm = [tm[n] + _dot(tm[n].astype(BF16), pmb[n]) for n in ns]
            pm = [_dot(pmb[n], pmb[n]) for n in ns]
        tm = [tm[n] + _dot(tm[n].astype(BF16), pm[n].astype(BF16)) for n in ns]
        out = []
        for n in ns:
            eg = jnp.exp(gcol[n])
            rhs = jnp.concatenate([vs[n] * bcol[n], kbs[n] * eg], axis=1).astype(BF16)
            sol = _dot(tm[n].astype(BF16), rhs)
            attn = jnp.where(mask_incl, _dot_nt(qs[n].astype(BF16), ksb[n]) * decay[n], 0.0)
            g_a = gcol[n][ch - 1:ch]
            g_b = gcol[n][2 * ch - 1:2 * ch]
            glast = jnp.where(row_col < ch, g_a, g_b)
            e_last = jnp.where(lane_row < hd, jnp.exp(g_a), jnp.exp(g_b))
            out.append((fold(sol[:, :pw]), fold(sol[:, pw:]).astype(BF16), fold(qs[n] * eg).astype(BF16),
                        fold(ks[n] * jnp.exp(glast - gcol[n])).astype(BF16), fold(attn).astype(BF16),
                        jnp.broadcast_to(e_last, (8, pw))))
        for n, (u, p) in enumerate(streams):
            rows = pl.ds(r0s[u], ch)
            us_ref[p, rows, :], ws_ref[p, rows, :], qe_ref[p, rows, :], kd_ref[p, rows, :], at_ref[p, rows, :] = out[n][:5]
            el_ref[p, pl.ds(pl.multiple_of(chunks[u] * 8, 8), 8), :] = out[n][5]
        return carry

    lax.fori_loop(0, seqlen // (ch * GDN_PREP_CHUNKS), prep, 0)

    s_ref[...] = jnp.zeros_like(s_ref)

    def scan(c, carry):
        r0 = pl.multiple_of(c * ch, ch)
        rows = pl.ds(r0, ch)
        lanes = [slice(p * pw, (p + 1) * pw) for p in pairs]
        states = [s_ref[p] for p in pairs]
        wq = [jnp.concatenate([ws_ref[p, rows, :], qe_ref[p, rows, :]], axis=0) for p in pairs]
        us = [us_ref[p, rows, :] for p in pairs]
        at = [at_ref[p, rows, :] for p in pairs]
        kd = [kd_ref[p, rows, :] for p in pairs]
        el = [el_ref[p, pl.ds(c * 8, 1), :] for p in pairs]
        zs = [z_ref[0, rows, lanes[p]] for p in pairs]
        wqs = [_dot(wq[p], states[p].astype(BF16)) for p in pairs]
        v_new = [us[p] - wqs[p][:ch] for p in pairs]
        upd = [_dot_tn(kd[p], v_new[p].astype(BF16)) for p in pairs]
        new_states = [states[p] * el[p] + jnp.where(same_head, upd[p], 0.0) for p in pairs]
        outs = []
        for p in pairs:
            o = wqs[p][ch:] + _dot(at[p], stack(v_new[p]).astype(BF16))
            ms = head_sumsq(o) * (1.0 / hd)
            outs.append(o * lax.rsqrt(ms + RMS_EPS) * nw_ref[...] * (zs[p] * jax.nn.sigmoid(zs[p])))
        for p in pairs:
            s_ref[p] = new_states[p]
            o_ref[0, rows, lanes[p]] = outs[p]
        return carry

    lax.fori_loop(0, seqlen // ch, scan, 0)


def _gdn_params(conv_w, a_log, dt_bias, norm_w):
    nh = a_log.shape[0]
    lane_pad = lambda v: jnp.pad(v.astype(F32), (0, LANE - nh)).reshape(1, LANE)
    return conv_w.astype(F32), lane_pad(a_log), lane_pad(dt_bias), jnp.tile(norm_w.astype(F32), 2).reshape(1, LANE)


def _gdn(q, k, v, z, ab, conv_w, al, dtb, nw):
    bsz, seqlen, gw = q.shape
    nh = gw // HEAD_DIM
    npair = nh // 2
    seq_spec = pl.BlockSpec((1, seqlen, gw), lambda b: (b, 0, 0))
    const = lambda b: (0, 0)
    return pl.pallas_call(
        functools.partial(_gdn_kernel, nh=nh, seqlen=seqlen),
        out_shape=jax.ShapeDtypeStruct((bsz, seqlen, gw), F32),
        grid=(bsz,),
        in_specs=[seq_spec, seq_spec, seq_spec, seq_spec,
                  pl.BlockSpec((1, seqlen, LANE), lambda b: (b, 0, 0)),
                  pl.BlockSpec(conv_w.shape, const),
                  pl.BlockSpec(al.shape, const),
                  pl.BlockSpec(dtb.shape, const),
                  pl.BlockSpec(nw.shape, const)],
        out_specs=seq_spec,
        scratch_shapes=[pltpu.VMEM((seqlen, LANE), F32)] * 2
        + [pltpu.VMEM((npair, seqlen, LANE), F32)]
        + [pltpu.VMEM((npair, seqlen, LANE), BF16)] * 4
        + [pltpu.VMEM((npair, seqlen // GDN_CHUNK * 8, LANE), F32)]
        + [pltpu.VMEM((npair, LANE, LANE), F32)],
        compiler_params=_cparams("parallel"),
        name="gdn",
    )(q, k, v, z, ab, conv_w, al, dtb, nw)


def _route(logits):
    n_exp = logits.shape[0]
    eidx = _iota(logits.shape, 0)
    m1 = jnp.max(logits, axis=0, keepdims=True)
    i1 = jnp.min(jnp.where(logits == m1, eidx, n_exp), axis=0, keepdims=True)
    lg2 = jnp.where(eidx == i1, -jnp.inf, logits)
    m2 = jnp.max(lg2, axis=0, keepdims=True)
    i2 = jnp.min(jnp.where(lg2 == m2, eidx, n_exp), axis=0, keepdims=True)
    e2 = jnp.exp(m2 - m1)
    g1 = 1.0 / (1.0 + e2)
    g2 = e2 / (1.0 + e2)
    gates = jnp.where(eidx == i1, g1, 0.0) + jnp.where(eidx == i2, g2, 0.0)
    mask = jnp.where((eidx == i1) | (eidx == i2), 1.0, 0.0)
    return jnp.concatenate([gates, mask], axis=0)


def _outproj_kernel(x_ref, ys_ref, ym_ref, yg_ref, w_ref, g_ref, b_ref, *rest, alpha, s5w, mw, n_exp):
    y = jnp.concatenate([ys_ref[...], ym_ref[0], yg_ref[0]], axis=-1).astype(BF16)
    x1 = _layer_norm(alpha * x_ref[...] + _dot(y, w_ref[...]), g_ref[...], b_ref[...])
    if n_exp:
        rw_ref, o_ref, r_ref, ot_ref = rest
        o_ref[...] = x1
        r_ref[...] = _route(_dot_nt(rw_ref[...], x1, precision=HI))
        for s in range(TOKEN_TILE_ROWS):
            ot_ref[:, s, :] = x1[:, s * LANE:(s + 1) * LANE]
    else:
        (o_ref,) = rest
        o_ref[...] = x1


def _outproj(x2, ys, ym, yg, w, g, b, router_w, *, bsz, seqlen, alpha, n_exp):
    tl = OUTPROJ_ROWS
    nt = seqlen // tl
    t, d = x2.shape
    s5w = ys.shape[1] // bsz
    mw = ym.shape[2]
    gw = yg.shape[2]
    const = lambda bi, j: (0, 0)
    row_spec = pl.BlockSpec((tl, d), lambda bi, j: (bi * nt + j, 0))
    in_specs = [row_spec,
                pl.BlockSpec((tl, s5w), lambda bi, j: (j, bi)),
                pl.BlockSpec((1, tl, mw), lambda bi, j: (bi, j, 0)),
                pl.BlockSpec((1, tl, gw), lambda bi, j: (bi, j, 0)),
                pl.BlockSpec(w.shape, const), pl.BlockSpec(g.shape, const), pl.BlockSpec(b.shape, const)]
    args = [x2, ys, ym, yg, w, g, b]
    out_shape = jax.ShapeDtypeStruct((t, d), F32)
    out_specs = row_spec
    if n_exp:
        in_specs.append(pl.BlockSpec(router_w.shape, const))
        args.append(router_w)
        assert d == TOKEN_TILE_ROWS * LANE
        out_shape = (out_shape, jax.ShapeDtypeStruct((2 * n_exp, t), F32),
                     jax.ShapeDtypeStruct((t, TOKEN_TILE_ROWS, LANE), F32))
        out_specs = (row_spec, pl.BlockSpec((2 * n_exp, tl), lambda bi, j: (0, bi * nt + j)),
                     pl.BlockSpec((tl, TOKEN_TILE_ROWS, LANE), lambda bi, j: (bi * nt + j, 0, 0)))
    return pl.pallas_call(
        functools.partial(_outproj_kernel, alpha=alpha, s5w=s5w, mw=mw, n_exp=n_exp),
        out_shape=out_shape,
        grid=(bsz, nt),
        in_specs=in_specs,
        out_specs=out_specs,
        compiler_params=_cparams("parallel", "arbitrary"),
        name="outproj_route" if n_exp else "outproj",
    )(*args)


def _swiglu_part(xb, wg_ref, wu_ref, wd_ref):
    out = None
    for c0 in range(0, wg_ref.shape[1], FFN_CHUNK):
        cols = slice(c0, c0 + FFN_CHUNK)
        hg = _dot(xb, wg_ref[:, cols].astype(BF16))
        hu = _dot(xb, wu_ref[:, cols].astype(BF16))
        h = (hg * jax.nn.sigmoid(hg)) * hu
        part = _dot(h.astype(BF16), wd_ref[cols, :].astype(BF16))
        out = part if out is None else out + part
    return out


def _ffn_dense_kernel(x_ref, wg_ref, wu_ref, wd_ref, g_ref, b_ref, o_ref, xb_ref, acc_ref, *, alpha):
    @pl.when(pl.program_id(1) == 0)
    def _():
        xb_ref[...] = x_ref[...].astype(BF16)
        acc_ref[...] = jnp.zeros_like(acc_ref)

    acc_ref[...] += _swiglu_part(xb_ref[...], wg_ref.at[0], wu_ref.at[0], wd_ref.at[0])

    @pl.when(pl.program_id(1) == pl.num_programs(1) - 1)
    def _():
        o_ref[...] = _layer_norm(alpha * x_ref[...] + acc_ref[...], g_ref[...], b_ref[...])


def _ffn_dense(x2, wg, wu, wd, layer, g, b, *, alpha):
    t, d = x2.shape
    ff = wg.shape[2]
    tm, tf = FFN_ROWS, FFN_COLS
    const = lambda i, f: (0, 0)
    return pl.pallas_call(
        functools.partial(_ffn_dense_kernel, alpha=alpha),
        out_shape=jax.ShapeDtypeStruct((t, d), F32),
        grid=(t // tm, ff // tf),
        in_specs=[pl.BlockSpec((tm, d), lambda i, f: (i, 0)),
                  pl.BlockSpec((1, d, tf), lambda i, f: (layer, 0, f)),
                  pl.BlockSpec((1, d, tf), lambda i, f: (layer, 0, f)),
                  pl.BlockSpec((1, tf, d), lambda i, f: (layer, f, 0)),
                  pl.BlockSpec(g.shape, const), pl.BlockSpec(b.shape, const)],
        out_specs=pl.BlockSpec((tm, d), lambda i, f: (i, 0)),
        scratch_shapes=[pltpu.VMEM((tm, d), BF16), pltpu.VMEM((tm, d), F32)],
        compiler_params=_cparams("parallel", "arbitrary"),
        name="ffn_dense",
    )(x2, wg, wu, wd, g, b)


def _ffn_moe_kernel(tile_ref, exp_ref, lo_ref, hi_ref, first_ref, x_ref, wg_ref, wu_ref, wd_ref, o_ref,
                    xt_ref, xb_ref, acc_ref, wgb_ref, wub_ref, wdb_ref):
    k = pl.program_id(0)
    f = pl.program_id(1)
    lo, hi = lo_ref[k], hi_ref[k]
    sub = MOE_SUB_ROWS
    d = o_ref.shape[1]

    nsub = o_ref.shape[0] // sub
    touched = sum(((lo < (sb + 1) * sub) & (hi > sb * sub)).astype(I32) for sb in range(nsub))
    whole = touched > nsub // 2

    def each_live_sub_block(fn):
        for sb in range(nsub):
            live = whole | ((lo < (sb + 1) * sub) & (hi > sb * sub))
            pl.when(live)(functools.partial(fn, sb, slice(sb * sub, (sb + 1) * sub)))

    @pl.when(hi > lo)
    def _():
        @pl.when((f == 0) & (first_ref[k] > 0))
        def _():
            o_ref[...] = jnp.zeros_like(o_ref)

        def load_rows(sb, rows):
            for s in range(TOKEN_TILE_ROWS):
                xt_ref[rows, s * LANE:(s + 1) * LANE] = x_ref[rows, s, :]
            xb_ref[rows, :] = xt_ref[rows, :].astype(BF16)
            acc_ref[rows, :] = jnp.zeros((sub, d), F32)

        pl.when(f == 0)(functools.partial(each_live_sub_block, load_rows))

        @pl.when(whole)
        def _():
            acc_ref[...] += _swiglu_part(xb_ref[...], wg_ref.at[0, 0], wu_ref.at[0, 0], wd_ref.at[0, 0])

        @pl.when(jnp.logical_not(whole))
        def _():
            wgb_ref[...] = wg_ref[0, 0].astype(BF16)
            wub_ref[...] = wu_ref[0, 0].astype(BF16)
            wdb_ref[...] = wd_ref[0, 0].astype(BF16)

            def compute(sb, rows):
                acc_ref[rows, :] += _swiglu_part(xb_ref[rows, :], wgb_ref, wub_ref, wdb_ref)

            each_live_sub_block(compute)

        def emit(sb, rows):
            row = _iota((sub, d), 0) + sb * sub
            o_ref[rows, :] = jnp.where((row >= lo) & (row < hi), acc_ref[rows, :], o_ref[rows, :])

        pl.when(f == pl.num_programs(1) - 1)(functools.partial(each_live_sub_block, emit))


def _ffn_moe(items, xs, wg, wu, wd, layer):
    p_rows = xs.shape[0]
    d = wg.shape[2]
    ff = wg.shape[3]
    tm, tf = MOE_ROWS, FFN_COLS
    nf = ff // tf
    n_items = items[0].shape[0]

    def fsel(k, f, lo, hi):
        return jnp.where(hi[k] > lo[k], f, nf - 1)

    grid_spec = pltpu.PrefetchScalarGridSpec(
        num_scalar_prefetch=5,
        grid=(n_items, nf),
        in_specs=[pl.BlockSpec((tm, TOKEN_TILE_ROWS, LANE), lambda k, f, tile, exp, lo, hi, first: (tile[k], 0, 0)),
                  pl.BlockSpec((1, 1, d, tf),
                               lambda k, f, tile, exp, lo, hi, first: (layer, exp[k], 0, fsel(k, f, lo, hi))),
                  pl.BlockSpec((1, 1, d, tf),
                               lambda k, f, tile, exp, lo, hi, first: (layer, exp[k], 0, fsel(k, f, lo, hi))),
                  pl.BlockSpec((1, 1, tf, d),
                               lambda k, f, tile, exp, lo, hi, first: (layer, exp[k], fsel(k, f, lo, hi), 0))],
        out_specs=pl.BlockSpec((tm, d), lambda k, f, tile, exp, lo, hi, first: (tile[k], 0)),
        scratch_shapes=[pltpu.VMEM((tm, d), F32), pltpu.VMEM((tm, d), BF16), pltpu.VMEM((tm, d), F32),
                        pltpu.VMEM((d, tf), BF16), pltpu.VMEM((d, tf), BF16), pltpu.VMEM((tf, d), BF16)],
    )
    return pl.pallas_call(
        _ffn_moe_kernel,
        out_shape=jax.ShapeDtypeStruct((p_rows, d), F32),
        grid_spec=grid_spec,
        compiler_params=_cparams("arbitrary", "arbitrary"),
        name="ffn_moe",
    )(*items, xs, wg, wu, wd)


def _tile_copy(src_ref, src_row, dst_ref, dst_row, sem):
    return pltpu.make_async_copy(src_ref.at[pl.ds(src_row, 1)], dst_ref.at[pl.ds(dst_row, 1)], sem)


def _dispatch_kernel(p0_ref, p1_ref, x_ref, xs_hbm, sem, *, rows):
    def start(r, carry):
        _tile_copy(x_ref, r, xs_hbm, p0_ref[0, 0, r], sem.at[0]).start(priority=0)
        _tile_copy(x_ref, r, xs_hbm, p1_ref[0, 0, r], sem.at[1]).start(priority=1)
        return carry

    lax.fori_loop(0, rows, start, 0, unroll=8)

    def wait(r, carry):
        _tile_copy(x_ref, r, xs_hbm, 0, sem.at[0]).wait()
        _tile_copy(x_ref, r, xs_hbm, 0, sem.at[1]).wait()
        return carry

    lax.fori_loop(0, rows, wait, 0, unroll=8)


def _dispatch(p0, p1, xt):
    nt, _, rows = p0.shape
    t = xt.shape[0]
    idx_spec = pl.BlockSpec((1, 1, rows), lambda i: (i, 0, 0), memory_space=pltpu.SMEM)
    return pl.pallas_call(
        functools.partial(_dispatch_kernel, rows=rows),
        out_shape=jax.ShapeDtypeStruct((t * TOP_K, TOKEN_TILE_ROWS, LANE), xt.dtype),
        grid=(nt,),
        in_specs=[idx_spec, idx_spec, pl.BlockSpec((rows, TOKEN_TILE_ROWS, LANE), lambda i: (i, 0, 0))],
        out_specs=pl.BlockSpec(memory_space=pl.ANY),
        scratch_shapes=[pltpu.SemaphoreType.DMA((2,))],
        compiler_params=_cparams("arbitrary"),
        name="moe_dispatch",
    )(p0, p1, xt)


def _combine_kernel(p0_ref, p1_ref, ys_hbm, x_ref, gt_ref, g_ref, b_ref, o_ref, y0_ref, y1_ref, sem, *, rows, alpha):
    def start(r, carry):
        _tile_copy(ys_hbm, p0_ref[0, 0, r], y0_ref, r, sem.at[0]).start(priority=0)
        _tile_copy(ys_hbm, p1_ref[0, 0, r], y1_ref, r, sem.at[1]).start(priority=1)
        return carry

    lax.fori_loop(0, rows, start, 0, unroll=8)

    def wait(r, carry):
        _tile_copy(ys_hbm, 0, y0_ref, r, sem.at[0]).wait()
        _tile_copy(ys_hbm, 0, y1_ref, r, sem.at[1]).wait()
        return carry

    lax.fori_loop(0, rows, wait, 0, unroll=8)
    gt = gt_ref[...]
    f = gt[:, 0:1] * y0_ref[...] + gt[:, 1:2] * y1_ref[...]
    o_ref[...] = _layer_norm(alpha * x_ref[...] + f, g_ref[...], b_ref[...])


def _combine(p0, p1, ys, x2, gates, g, b, *, alpha):
    nt, _, rows = p0.shape
    t, d = x2.shape
    const = lambda i: (0, 0)
    idx_spec = pl.BlockSpec((1, 1, rows), lambda i: (i, 0, 0), memory_space=pltpu.SMEM)
    tile_buf = pltpu.VMEM((rows, d), F32)
    return pl.pallas_call(
        functools.partial(_combine_kernel, rows=rows, alpha=alpha),
        out_shape=jax.ShapeDtypeStruct((t, d), F32),
        grid=(nt,),
        in_specs=[idx_spec, idx_spec,
                  pl.BlockSpec(memory_space=pl.ANY),
                  pl.BlockSpec((rows, d), lambda i: (i, 0)),
                  pl.BlockSpec((rows, LANE), lambda i: (i, 0)),
                  pl.BlockSpec(g.shape, const), pl.BlockSpec(b.shape, const)],
        out_specs=pl.BlockSpec((rows, d), lambda i: (i, 0)),
        scratch_shapes=[tile_buf, tile_buf, pltpu.SemaphoreType.DMA((2,))],
        compiler_params=_cparams("arbitrary"),
        name="moe_combine",
    )(p0, p1, ys, x2, gates, g, b)


def _dispatch_plan(route, n_exp):
    t = route.shape[1]
    tm = MOE_ROWS
    gates = route[:n_exp].T
    sel = route[n_exp:].T > 0.5
    seli = sel.astype(I32)
    rank = jnp.cumsum(seli, axis=0) - seli
    counts = jnp.sum(seli, axis=0)
    ends = jnp.cumsum(counts)
    starts = ends - counts
    pos = starts[None, :] + rank
    order = jnp.cumsum(seli, axis=1) - seli
    first = sel & (order == 0)
    second = sel & (order == 1)
    pick = lambda m, v: jnp.sum(jnp.where(m, v, 0), axis=1)
    p0, p1 = pick(first, pos), pick(second, pos)
    g01 = jnp.stack([pick(first, gates), pick(second, gates)], axis=1)
    g01 = jnp.pad(g01, ((0, 0), (0, LANE - TOP_K)))
    n_tiles = t * TOP_K // tm
    n_items = n_tiles + n_exp - 1
    tile_lo = jnp.arange(n_tiles, dtype=I32)[:, None] * tm
    ov_lo = jnp.maximum(starts[None, :], tile_lo)
    ov_hi = jnp.minimum(ends[None, :], tile_lo + tm)
    live = (ov_hi > ov_lo).reshape(-1)
    flat = jnp.arange(n_tiles * n_exp, dtype=I32)
    take = jnp.argsort(jnp.where(live, flat, n_tiles * n_exp))[:n_items]
    n_live = jnp.sum(live.astype(I32))
    idle = jnp.arange(n_items, dtype=I32) >= n_live
    last = take[jnp.maximum(n_live - 1, 0)]
    take = jnp.where(idle, last, take)
    item_tile = take // n_exp
    item_exp = take % n_exp
    item_lo = jnp.where(idle, 0, (ov_lo - tile_lo).reshape(-1)[take])
    item_hi = jnp.where(idle, 0, (ov_hi - tile_lo).reshape(-1)[take])
    prev_tile = jnp.concatenate([jnp.full((1,), -1, I32), item_tile[:-1]])
    item_first = (item_tile != prev_tile).astype(I32)
    items = tuple(v.astype(I32) for v in (item_tile, item_exp, item_lo, item_hi, item_first))
    shape3 = lambda v: v.astype(I32).reshape(-1, 1, GATHER_ROWS)
    return shape3(p0), shape3(p1), g01, items


def kernel(x, w_in, w_out, s5_lam_re, s5_lam_im, s5_log_dt, s5_b_re, s5_b_im, s5_c_re, s5_c_im, s5_d, s5_glu_w,
           s5_glu_b, gdn_conv_w, gdn_a_log, gdn_dt_bias, gdn_norm_w, ln1_g, ln1_b, ln2_g, ln2_b, ffn_w_gate,
           ffn_w_up, ffn_w_down, moe_router, moe_w_gate, moe_w_up, moe_w_down):
    bsz, seqlen, d = x.shape
    depth = w_in.shape[0]
    s5w = s5_glu_w.shape[1]
    gw = gdn_a_log.shape[1] * HEAD_DIM
    mw = w_out.shape[1] - s5w - gw
    n_exp = moe_router.shape[2]
    alpha = (2 * depth) ** 0.25
    row = lambda v: v.astype(F32).reshape(1, -1)

    x2 = x.reshape(bsz * seqlen, d)
    for l in range(depth):
        w, wt = _pack_inproj(w_in[l], s5w, mw, gw)
        u, mqt, mk, mvt, gq, gk, gv, gz, ab = _inproj(x2, w, wt, bsz=bsz, seqlen=seqlen, s5w=s5w, mw=mw, gw=gw)
        bblk, cblk, lam, dskip = _s5_params(s5_lam_re[l], s5_lam_im[l], s5_log_dt[l], s5_b_re[l], s5_b_im[l],
                                            s5_c_re[l], s5_c_im[l], s5_d[l])
        y_s5 = _s5(u.reshape(seqlen * bsz, s5w), bblk, cblk, lam, dskip, s5_glu_w[l].astype(BF16),
                   row(s5_glu_b[l]), bsz=bsz, seqlen=seqlen).reshape(seqlen, bsz * s5w)
        y_moba = _moba(mqt, mk, mvt, seqlen=seqlen)
        y_gdn = _gdn(gq, gk, gv, gz, ab, *_gdn_params(gdn_conv_w[l], gdn_a_log[l], gdn_dt_bias[l], gdn_norm_w[l]))
        routed = l % 2 == 1
        router_w = moe_router[l // 2].astype(F32).T if routed else None
        res = _outproj(x2, y_s5, y_moba, y_gdn, w_out[l].astype(BF16), row(ln1_g[l]), row(ln1_b[l]), router_w,
                       bsz=bsz, seqlen=seqlen, alpha=alpha, n_exp=n_exp if routed else 0)
        if routed:
            x1, route, x1t = res
            p0, p1, g01, items = _dispatch_plan(route, n_exp)
            xs = _dispatch(p0, p1, x1t)
            ys = _ffn_moe(items, xs, moe_w_gate, moe_w_up, moe_w_down, l // 2)
            x2 = _combine(p0, p1, ys, x1, g01, row(ln2_g[l]), row(ln2_b[l]), alpha=alpha)
        else:
            x2 = _ffn_dense(res, ffn_w_gate, ffn_w_up, ffn_w_down, l // 2, row(ln2_g[l]), row(ln2_b[l]), alpha=alpha)
    return x2.reshape(bsz, seqlen, d)
```
